```python
import math
import jax, jax.numpy as jnp
from jax import lax
import numpy as np

D_MODEL = 1024
BATCH = 8
SEQ = 2048
DEPTH = 1

GDN_HEADS = 8
GDN_DK = 128
GDN_DV = 128
GDN_CONV = 4
GDN_CHUNK = 64
DIL_PATTERN = ((128, 1), (512, 4), (2048, 16))
DIL_HEADS_PER_GROUP = 4
DIL_HD = 128
DIL_BLOCK = 128
N_EXPERTS = 64
TOP_K = 8
N_GROUPS = 8
TOPK_GROUPS = 4
D_EXPERT = 256
D_SHARED = 256
ROUTE_SCALE = 2.5
MOE_TOKEN_BLOCK = 512
EPS = 1e-6

GDN_QK_W = GDN_HEADS * GDN_DK
GDN_V_W = GDN_HEADS * GDN_DV
N_DIL_GROUPS = len(DIL_PATTERN)
DIL_HEADS = N_DIL_GROUPS * DIL_HEADS_PER_GROUP
DIL_W = DIL_HEADS * DIL_HD
DIL_OUT_W = DIL_HEADS_PER_GROUP * DIL_HD
SPLIT_WIDTHS = (GDN_QK_W, GDN_QK_W, GDN_V_W, GDN_V_W, GDN_HEADS, GDN_HEADS,
                DIL_W, DIL_W, DIL_W, D_MODEL, D_MODEL)
IN_W = sum(SPLIT_WIDTHS)
SPLIT_OFFSETS = tuple(sum(SPLIT_WIDTHS[:i + 1]) for i in range(len(SPLIT_WIDTHS) - 1))

kernel_name = "hybrid_gdn_dilated_moe_adaln"


def _rms(t, g):
    t32 = t.astype(jnp.float32)
    return t32 * lax.rsqrt(jnp.mean(t32 * t32, axis=-1, keepdims=True) + EPS) * g.astype(jnp.float32)


def _l2norm(t):
    return t * lax.rsqrt(jnp.sum(t * t, axis=-1, keepdims=True) + EPS)


def causal_depthwise_conv(x, w):
    K, C = w.shape
    return lax.conv_general_dilated(x, w[:, None, :], window_strides=(1,), padding=((K - 1, 0),),
                                    dimension_numbers=('NWC', 'WIO', 'NWC'), feature_group_count=C)


def chunk_gated_delta_rule(q, k, v, g, beta):
    B, H, S, dk = q.shape
    dv = v.shape[-1]
    C = GDN_CHUNK
    N = S // C
    q = q.reshape(B, H, N, C, dk)
    k = k.reshape(B, H, N, C, dk)
    v = v.reshape(B, H, N, C, dv)
    g = g.reshape(B, H, N, C)
    beta = beta.reshape(B, H, N, C)
    gc = jnp.cumsum(g, axis=-1)
    incl = jnp.tril(jnp.ones((C, C), bool))
    strict = jnp.tril(jnp.ones((C, C), bool), -1)
    decay = jnp.exp(jnp.where(incl, gc[..., :, None] - gc[..., None, :], -jnp.inf))
    k_beta = k * beta[..., None]
    a_mat = jnp.where(strict, jnp.einsum('bhnid,bhnjd->bhnij', k_beta, k) * decay, 0.0)
    lmat = a_mat + jnp.eye(C, dtype=a_mat.dtype)
    rhs = jnp.concatenate([v * beta[..., None], k_beta * jnp.exp(gc)[..., None]], axis=-1)
    sol = lax.linalg.triangular_solve(lmat, rhs, left_side=True, lower=True, unit_diagonal=True)
    u, w = sol[..., :dv], sol[..., dv:]
    qk = jnp.einsum('bhnid,bhnjd->bhnij', q, k) * decay
    q_dec = q * jnp.exp(gc)[..., None]
    k_dec = k * jnp.exp(gc[..., -1:] - gc)[..., None]
    g_tot = jnp.exp(gc[..., -1])

    def step(state, xs):
        u_n, w_n, qk_n, qd_n, kd_n, gt_n = xs
        v_new = u_n - jnp.einsum('bhck,bhkv->bhcv', w_n, state)
        o_n = jnp.einsum('bhck,bhkv->bhcv', qd_n, state) + jnp.einsum('bhij,bhjv->bhiv', qk_n, v_new)
        state = state * gt_n[..., None, None] + jnp.einsum('bhck,bhcv->bhkv', kd_n, v_new)
        return state, o_n

    xs = tuple(jnp.moveaxis(t, 2, 0) for t in (u, w, qk, q_dec, k_dec, g_tot))
    state0 = jnp.zeros((B, H, dk, dv), q.dtype)
    _, o = lax.scan(step, state0, xs)
    return jnp.moveaxis(o, 0, 2).reshape(B, H, S, dv)


def gated_deltanet(q, k, v, z, b, a, conv_w, a_log, dt_bias, norm_g):
    B, S, _ = q.shape
    f32 = jnp.float32
    qkv = jnp.concatenate([q, k, v], axis=-1).astype(f32)
    qkv = jax.nn.silu(causal_depthwise_conv(qkv, conv_w.astype(f32)))
    q, k, v = jnp.split(qkv, [GDN_QK_W, 2 * GDN_QK_W], axis=-1)

    def heads(t, d):
        return t.reshape(B, S, GDN_HEADS, d).transpose(0, 2, 1, 3)

    q = _l2norm(heads(q, GDN_DK)) * (GDN_DK ** -0.5)
    k = _l2norm(heads(k, GDN_DK))
    v = heads(v, GDN_DV)
    beta = jax.nn.sigmoid(b.astype(f32)).transpose(0, 2, 1)
    g = (-jnp.exp(a_log.astype(f32)) * jax.nn.softplus(a.astype(f32) + dt_bias.astype(f32))).transpose(0, 2, 1)
    o = chunk_gated_delta_rule(q, k, v, g, beta).transpose(0, 2, 1, 3)
    zg = jax.nn.silu(z.astype(f32).reshape(B, S, GDN_HEADS, GDN_DV))
    o = _rms(o, norm_g) * zg
    return o.reshape(B, S, GDN_V_W)


def dilated_window_attention(q, k, v, window, dilation):
    B, S, h, e = q.shape
    L = S // dilation
    W = window // dilation
    blk = math.gcd(L, DIL_BLOCK)
    nb = L // blk

    def sub(t):
        return t.reshape(B, L, dilation, h, e).transpose(0, 2, 3, 1, 4)

    qs = sub(q).reshape(B, dilation, h, nb, blk, e)
    key_idx = np.arange(nb)[:, None] * blk + np.arange(blk + W)[None, :]
    pad = ((0, 0), (0, 0), (0, 0), (W, 0), (0, 0))
    ks = jnp.pad(sub(k), pad)[:, :, :, key_idx]
    vs = jnp.pad(sub(v), pad)[:, :, :, key_idx]
    dist = np.arange(blk)[:, None] + W - np.arange(blk + W)[None, :]
    valid = (dist >= 0) & (dist <= W) & ((key_idx[:, None, :] - W) >= 0)
    s = jnp.einsum('bdhnqe,bdhnke->bdhnqk', qs, ks) * (e ** -0.5)
    s = jnp.where(valid, s, -jnp.inf)
    m = jnp.max(s, axis=-1, keepdims=True)
    p = jnp.exp(s - m)
    den = jnp.sum(p, axis=-1, keepdims=True)
    o = jnp.einsum('bdhnqk,bdhnke->bdhnqe', p, vs) / den
    lse = (m + jnp.log(den))[..., 0]
    o = o.reshape(B, dilation, h, L, e).transpose(0, 3, 1, 2, 4).reshape(B, S, h, e)
    lse = lse.reshape(B, dilation, h, L).transpose(0, 3, 1, 2).reshape(B, S, h)
    return o, lse


def dilated_mixture(q, k, v, q_norm_g, k_norm_g):
    B, S, _ = q.shape
    q = _rms(q.reshape(B, S, DIL_HEADS, DIL_HD), q_norm_g)
    k = _rms(k.reshape(B, S, DIL_HEADS, DIL_HD), k_norm_g)
    v = v.reshape(B, S, DIL_HEADS, DIL_HD).astype(jnp.float32)
    outs, lses = [], []
    for gi, (win, dil) in enumerate(DIL_PATTERN):
        hs = slice(gi * DIL_HEADS_PER_GROUP, (gi + 1) * DIL_HEADS_PER_GROUP)
        o, lse = dilated_window_attention(q[:, :, hs], k[:, :, hs], v[:, :, hs], win, dil)
        outs.append(o)
        lses.append(lse)
    wts = jax.nn.softmax(jnp.stack(lses), axis=0)
    y = jnp.einsum('gbsh,gbshe->bshe', wts, jnp.stack(outs))
    return y.reshape(B, S, DIL_OUT_W)


def moe_ffn(h, w_router, router_bias, w_gate, w_up, w_down, ws_gate, ws_up, ws_down):
    B, S, D = h.shape
    T = B * S
    f32 = jnp.float32
    ht = h.reshape(T, D)
    scores = jax.nn.sigmoid(ht.astype(f32) @ w_router.astype(f32))
    sel = scores + router_bias.astype(f32)
    grp = sel.reshape(T, N_GROUPS, N_EXPERTS // N_GROUPS)
    grp_score = jnp.sum(lax.top_k(grp, 2)[0], axis=-1)
    _, gidx = lax.top_k(grp_score, TOPK_GROUPS)
    gmask = jnp.sum(jax.nn.one_hot(gidx, N_GROUPS, dtype=f32), axis=1)
    emask = jnp.repeat(gmask, N_EXPERTS // N_GROUPS, axis=1) > 0
    _, eidx = lax.top_k(jnp.where(emask, sel, -jnp.inf), TOP_K)
    wk = jnp.take_along_axis(scores, eidx, axis=-1)
    wk = wk / jnp.sum(wk, axis=-1, keepdims=True) * ROUTE_SCALE
    combine = jnp.einsum('tk,tke->te', wk, jax.nn.one_hot(eidx, N_EXPERTS, dtype=f32)).astype(h.dtype)
    tb = math.gcd(T, MOE_TOKEN_BLOCK)

    def block(args):
        hb, cb = args
        a = jnp.einsum('td,edf->tef', hb, w_gate)
        u = jnp.einsum('td,edf->tef', hb, w_up)
        return jnp.einsum('tef,efd->td', jax.nn.silu(a) * u * cb[..., None], w_down)

    routed = lax.map(block, (ht.reshape(T // tb, tb, D), combine.reshape(T // tb, tb, N_EXPERTS))).reshape(T, D)
    shared = (jax.nn.silu(ht @ ws_gate) * (ht @ ws_up)) @ ws_down
    return (routed + shared).reshape(B, S, D)


def setup_inputs(seed: int = 0) -> dict:
    key = jax.random.key(seed)
    ks = jax.random.split(key, 24)
    L, D, E = DEPTH, D_MODEL, N_EXPERTS
    nrm = lambda k, shape, scale: jax.random.normal(k, shape, jnp.float32) * scale
    dt = jnp.exp(jax.random.uniform(ks[7], (L, GDN_HEADS), jnp.float32, math.log(1e-3), math.log(1e-1)))
    return {
        "x": nrm(ks[0], (BATCH, SEQ, D), 1.0),
        "c": nrm(ks[1], (BATCH, D), 1.0),
        "w_ada": nrm(ks[2], (L, D, 6 * D), 0.5 * D ** -0.5),
        "b_ada": nrm(ks[3], (L, 6 * D), 0.02),
        "g_mix": 1.0 + nrm(ks[4], (L, D), 0.05),
        "w_in": nrm(ks[5], (L, D, IN_W), D ** -0.5),
        "gdn_conv_w": nrm(ks[6], (L, GDN_CONV, 2 * GDN_QK_W + GDN_V_W), 0.5),
        "gdn_a_log": jnp.log(jax.random.uniform(ks[8], (L, GDN_HEADS), jnp.float32, 1.0, 16.0)),
        "gdn_dt_bias": dt + jnp.log(-jnp.expm1(-dt)),
        "gdn_norm_g": 1.0 + nrm(ks[9], (L, GDN_DV), 0.05),
        "dil_q_norm_g": 1.0 + nrm(ks[10], (L, DIL_HD), 0.05),
        "dil_k_norm_g": 1.0 + nrm(ks[11], (L, DIL_HD), 0.05),
        "w_up_gdn": nrm(ks[12], (L, GDN_V_W, D), GDN_V_W ** -0.5),
        "w_up_dil": nrm(ks[13], (L, DIL_OUT_W, D), DIL_OUT_W ** -0.5),
        "w_out": nrm(ks[14], (L, D, D), D ** -0.5),
        "g_ffn": 1.0 + nrm(ks[15], (L, D), 0.05),
        "w_router": nrm(ks[16], (L, D, E), D ** -0.5),
        "router_bias": nrm(ks[17], (L, E), 0.01),
        "w_exp_gate": nrm(ks[18], (L, E, D, D_EXPERT), D ** -0.5),
        "w_exp_up": nrm(ks[19], (L, E, D, D_EXPERT), D ** -0.5),
        "w_exp_down": nrm(ks[20], (L, E, D_EXPERT, D), D_EXPERT ** -0.5),
        "w_sh_gate": nrm(ks[21], (L, D, D_SHARED), D ** -0.5),
        "w_sh_up": nrm(ks[22], (L, D, D_SHARED), D ** -0.5),
        "w_sh_down": nrm(ks[23], (L, D_SHARED, D), D_SHARED ** -0.5),
    }


def reference(x, c, w_ada, b_ada, g_mix, w_in, gdn_conv_w, gdn_a_log, gdn_dt_bias, gdn_norm_g,
              dil_q_norm_g, dil_k_norm_g, w_up_gdn, w_up_dil, w_out, g_ffn, w_router, router_bias,
              w_exp_gate, w_exp_up, w_exp_down, w_sh_gate, w_sh_up, w_sh_down):
    cs = jax.nn.silu(c)
    for l in range(DEPTH):
        mod = (cs @ w_ada[l] + b_ada[l])[:, None, :]
        sh1, sc1, gt1, sh2, sc2, gt2 = jnp.split(mod, 6, axis=-1)
        h = _rms(x, g_mix[l]).astype(x.dtype) * (1.0 + sc1) + sh1
        proj = h @ w_in[l]
        (gq, gk, gv, gz, gb, ga, dil_q, dil_k, dil_v, gate_a, gate_b) = jnp.split(proj, SPLIT_OFFSETS, axis=-1)
        y_a = gated_deltanet(gq, gk, gv, gz, gb, ga, gdn_conv_w[l], gdn_a_log[l], gdn_dt_bias[l],
                             gdn_norm_g[l]).astype(x.dtype)
        y_b = dilated_mixture(dil_q, dil_k, dil_v, dil_q_norm_g[l], dil_k_norm_g[l]).astype(x.dtype)
        merged = jax.nn.sigmoid(gate_a) * (y_a @ w_up_gdn[l]) + jax.nn.sigmoid(gate_b) * (y_b @ w_up_dil[l])
        x = x + gt1 * (merged @ w_out[l])
        h2 = _rms(x, g_ffn[l]).astype(x.dtype) * (1.0 + sc2) + sh2
        x = x + gt2 * moe_ffn(h2, w_router[l], router_bias[l], w_exp_gate[l], w_exp_up[l], w_exp_down[l],
                              w_sh_gate[l], w_sh_up[l], w_sh_down[l])
    return x
```

```python
import functools

import jax
import jax.numpy as jnp
from jax import lax
from jax.experimental import pallas as pl
from jax.experimental.pallas import tpu as pltpu

F32 = jnp.float32
BF16 = jnp.bfloat16

D_MODEL = 1024
GDN_HEADS = 8
GDN_DK = 128
GDN_CONV = 4
GDN_CHUNK = 64
DIL_PATTERN = ((128, 1), (512, 4), (2048, 16))
DIL_HEADS_PER_GROUP = 4
DIL_HD = 128
DIL_BLOCK = 128
N_EXPERTS = 64
TOP_K = 8
N_GROUPS = 8
TOPK_GROUPS = 4
D_EXPERT = 256
ROUTE_SCALE = 2.5
EPS = 1e-6

GDN_W = GDN_HEADS * GDN_DK
DIL_HEADS = len(DIL_PATTERN) * DIL_HEADS_PER_GROUP
DIL_W = DIL_HEADS * DIL_HD
DIL_OUT_W = DIL_HEADS_PER_GROUP * DIL_HD
COL_GQ, COL_GK, COL_GV, COL_GZ = 0, GDN_W, 2 * GDN_W, 3 * GDN_W
COL_GA = 4 * GDN_W
COL_GB = COL_GA + D_MODEL
COL_DQ = COL_GB + D_MODEL
COL_DK = COL_DQ + DIL_W
COL_DV = COL_DK + DIL_W
PROJ_W = COL_DV + DIL_W
LANES = 128

VMEM_LIMIT = 56 * 1024 * 1024


def _params(*sem):
    return pltpu.CompilerParams(dimension_semantics=sem, vmem_limit_bytes=VMEM_LIMIT)


def _sigmoid(x):
    return 1.0 / (1.0 + jnp.exp(-x))


def _silu(x):
    return x * _sigmoid(x)


def _softplus(x):
    return jnp.maximum(x, 0.0) + jnp.log(1.0 + jnp.exp(-jnp.abs(x)))


def _mm(a, b):
    return jnp.dot(a.astype(BF16), b.astype(BF16), preferred_element_type=F32)


def _mm_nt(a, b):
    return lax.dot_general(a.astype(BF16), b.astype(BF16), (((1,), (1,)), ((), ())),
                           preferred_element_type=F32)


def _mm_tn(a, b):
    return lax.dot_general(a.astype(BF16), b.astype(BF16), (((0,), (0,)), ((), ())),
                           preferred_element_type=F32)


def _mm_f32(a, b):
    return jnp.dot(a, b, preferred_element_type=F32, precision=lax.Precision.HIGHEST)


def _ada_kernel(c_ref, w_ref, b_ref, o_ref):
    cs = _silu(c_ref[...])
    o_ref[...] = _mm_f32(cs, w_ref[...]) + b_ref[...]


def _ada(c, w_ada, b_ada):
    bsz, d = c.shape
    n = w_ada.shape[1]
    tn = 1536
    return pl.pallas_call(
        _ada_kernel,
        grid=(n // tn,),
        in_specs=[pl.BlockSpec((bsz, d), lambda j: (0, 0)),
                  pl.BlockSpec((d, tn), lambda j: (0, j)),
                  pl.BlockSpec((1, tn), lambda j: (0, j))],
        out_specs=pl.BlockSpec((bsz, tn), lambda j: (0, j)),
        out_shape=jax.ShapeDtypeStruct((bsz, n), F32),
        compiler_params=_params("parallel"),
        name="ada",
    )(c, w_ada, b_ada.reshape(1, n))


def _inproj_kernel(x_ref, mod_ref, g_ref, w_ref, wba_ref, alog_ref, dtb_ref,
                   proj_ref, gates_ref, h_scr):
    j = pl.program_id(2)

    @pl.when(j == 0)
    def _():
        x = x_ref[0]
        sh1 = mod_ref[0, 0:1, :]
        sc1 = mod_ref[0, 1:2, :]
        ms = jnp.mean(x * x, axis=-1, keepdims=True)
        h = x * lax.rsqrt(ms + EPS) * g_ref[...] * (1.0 + sc1) + sh1
        hb = h.astype(BF16)
        h_scr[...] = hb
        ba = jnp.dot(hb, wba_ref[...], preferred_element_type=F32)
        lane = lax.broadcasted_iota(jnp.int32, ba.shape, 1)
        beta = _sigmoid(ba)
        g = -jnp.exp(alog_ref[...]) * _softplus(ba + dtb_ref[...])
        gates_ref[0] = jnp.where(lane < GDN_HEADS, beta, g)

    proj_ref[0] = jnp.dot(h_scr[...], w_ref[...], preferred_element_type=F32).astype(BF16)


def _inproj(x, mod3, g_mix, w_main, w_ba, alog_vec, dtb_vec):
    bsz, s, d = x.shape
    tm, tn = 1024, 768
    return pl.pallas_call(
        _inproj_kernel,
        grid=(bsz, s // tm, PROJ_W // tn),
        in_specs=[pl.BlockSpec((1, tm, d), lambda b, i, j: (b, i, 0)),
                  pl.BlockSpec((1, 6, d), lambda b, i, j: (b, 0, 0)),
                  pl.BlockSpec((1, d), lambda b, i, j: (0, 0)),
                  pl.BlockSpec((d, tn), lambda b, i, j: (0, j)),
                  pl.BlockSpec((d, LANES), lambda b, i, j: (0, 0)),
                  pl.BlockSpec((1, LANES), lambda b, i, j: (0, 0)),
                  pl.BlockSpec((1, LANES), lambda b, i, j: (0, 0))],
        out_specs=[pl.BlockSpec((1, tm, tn), lambda b, i, j: (b, i, j)),
                   pl.BlockSpec((1, tm, LANES), lambda b, i, j: (b, i, 0))],
        out_shape=[jax.ShapeDtypeStruct((bsz, s, PROJ_W), BF16),
                   jax.ShapeDtypeStruct((bsz, s, LANES), F32)],
        scratch_shapes=[pltpu.VMEM((tm, d), BF16)],
        compiler_params=_params("parallel", "parallel", "arbitrary"),
        name="inproj",
    )(x, mod3, g_mix.reshape(1, d), w_main, w_ba, alog_vec, dtb_vec)


GDN_HB = 4
GDN_SB = 256
HALO = 8


def _chunk_cumsum(g, chunk):
    row = lax.broadcasted_iota(jnp.int32, g.shape, 0) % chunk
    sft = 1
    while sft < chunk:
        g = g + jnp.where(row >= sft, pltpu.roll(g, sft, 0), 0.0)
        sft *= 2
    return g


def _unit_lower_inverse(a):
    c = a.shape[0]
    eye = (lax.broadcasted_iota(jnp.int32, (c, c), 0) == lax.broadcasted_iota(jnp.int32, (c, c), 1)).astype(F32)
    t = eye - a
    p = a
    n = 2
    while n < c:
        p = _mm_f32(p, p)
        t = t + _mm_f32(t, p)
        n *= 2
    return t


def _gdn_kernel(q_ref, k_ref, v_ref, z_ref, gates_ref, cw_ref, ng_ref, o_ref,
                qbuf, kbuf, vbuf, state):
    hblk = pl.program_id(1)
    sblk = pl.program_id(2)
    c = GDN_CHUNK
    sb = GDN_SB
    w = GDN_HB * GDN_DK

    @pl.when(sblk == 0)
    def _():
        state[...] = jnp.zeros_like(state)
        for buf in (qbuf, kbuf, vbuf):
            buf[0:HALO, :] = jnp.zeros((HALO, w), F32)

    @pl.when(sblk > 0)
    def _():
        for buf in (qbuf, kbuf, vbuf):
            buf[0:HALO, :] = buf[sb:sb + HALO, :]

    conv = []
    for idx, (ref, buf) in enumerate(((q_ref, qbuf), (k_ref, kbuf), (v_ref, vbuf))):
        buf[HALO:HALO + sb, :] = ref[0].astype(F32)
        acc = jnp.zeros((sb, w), F32)
        for j in range(GDN_CONV):
            start = HALO - (GDN_CONV - 1) + j
            acc = acc + buf[start:start + sb, :] * cw_ref[idx, j:j + 1, :]
        conv.append(_silu(acc))
    qc, kc, vc = conv

    gates = gates_ref[0]
    row = lax.broadcasted_iota(jnp.int32, (c, c), 0)
    col = lax.broadcasted_iota(jnp.int32, (c, c), 1)
    eye = row == col
    incl = col <= row
    strict = col < row

    for hh in range(GDN_HB):
        hs = slice(hh * GDN_DK, (hh + 1) * GDN_DK)
        q_h = qc[:, hs]
        k_h = kc[:, hs]
        v_h = vc[:, hs]
        q_h = q_h * lax.rsqrt(jnp.sum(q_h * q_h, axis=-1, keepdims=True) + EPS) * (GDN_DK ** -0.5)
        k_h = k_h * lax.rsqrt(jnp.sum(k_h * k_h, axis=-1, keepdims=True) + EPS)
        lane = lax.broadcasted_iota(jnp.int32, gates.shape, 1)
        hidx = hblk * GDN_HB + hh
        beta_b = jnp.broadcast_to(
            jnp.sum(jnp.where(lane == hidx, gates, 0.0), axis=-1, keepdims=True), (sb, GDN_DK))
        g_b = jnp.broadcast_to(
            jnp.sum(jnp.where(lane == hidx + GDN_HEADS, gates, 0.0), axis=-1, keepdims=True), (sb, GDN_DK))
        gc_b = _chunk_cumsum(g_b, c)
        z_h = z_ref[0, :, hs].astype(F32)

        st = state[hh]
        for n in range(sb // c):
            rs = slice(n * c, (n + 1) * c)
            qn, kn, vn = q_h[rs], k_h[rs], v_h[rs]
            beta_n = beta_b[rs]
            gc_n = gc_b[rs]
            gc_col = gc_n[:, :c]
            gc_row = jnp.sum(jnp.where(eye, gc_col, 0.0), axis=0, keepdims=True)
            decay = jnp.exp(jnp.where(incl, gc_col - gc_row, -jnp.inf))
            egc = jnp.exp(gc_n)
            gc_last = gc_n[c - 1:c, :]
            kb = kn * beta_n
            scores = _mm_nt(jnp.concatenate([kb, qn], axis=0), kn)
            a_mat = jnp.where(strict, scores[:c] * decay, 0.0)
            qk = scores[c:] * decay
            t_inv = _unit_lower_inverse(a_mat)
            sol = _mm_f32(t_inv, jnp.concatenate([vn * beta_n, kb * egc], axis=1))
            u_n, w_n = sol[:, :GDN_DK], sol[:, GDN_DK:]
            ws = _mm(jnp.concatenate([w_n, qn * egc], axis=0), st)
            v_new = u_n - ws[:c]
            o_n = ws[c:] + _mm(qk, v_new)
            k_dec = kn * jnp.exp(gc_last - gc_n)
            st = st * jnp.exp(gc_last) + _mm_tn(k_dec, v_new)
            y = o_n * lax.rsqrt(jnp.mean(o_n * o_n, axis=-1, keepdims=True) + EPS) * ng_ref[...]
            o_ref[0, rs, hs] = (y * _silu(z_h[rs])).astype(o_ref.dtype)
        state[hh] = st


def _gdn(proj, gates, conv_w3, norm_g):
    bsz, s, _ = proj.shape
    w = GDN_HB * GDN_DK
    nhb = GDN_HEADS // GDN_HB

    def col_spec(col0):
        blk0 = col0 // w
        return pl.BlockSpec((1, GDN_SB, w), lambda b, h, i: (b, i, blk0 + h))

    return pl.pallas_call(
        _gdn_kernel,
        grid=(bsz, nhb, s // GDN_SB),
        in_specs=[col_spec(COL_GQ), col_spec(COL_GK), col_spec(COL_GV), col_spec(COL_GZ),
                  pl.BlockSpec((1, GDN_SB, LANES), lambda b, h, i: (b, i, 0)),
                  pl.BlockSpec((3, GDN_CONV, w), lambda b, h, i: (0, 0, h)),
                  pl.BlockSpec((1, GDN_DK), lambda b, h, i: (0, 0))],
        out_specs=pl.BlockSpec((1, GDN_SB, w), lambda b, h, i: (b, i, h)),
        out_shape=jax.ShapeDtypeStruct((bsz, s, GDN_W), BF16),
        scratch_shapes=[pltpu.VMEM((GDN_SB + HALO, w), F32),
                        pltpu.VMEM((GDN_SB + HALO, w), F32),
                        pltpu.VMEM((GDN_SB + HALO, w), F32),
                        pltpu.VMEM((GDN_HB, GDN_DK, GDN_DK), F32)],
        compiler_params=_params("parallel", "parallel", "arbitrary"),
        name="gdn",
    )(proj, proj, proj, proj, gates, conv_w3, norm_g.reshape(1, GDN_DK))


def _dil_kernel(q0, q1, q2, k0, k1, k2, v0, v1, v2, qg_ref, kg_ref, o_ref, qs, ks, vs, os_, ls):
    s = qs.shape[1]
    blk = DIL_BLOCK
    ngrp = len(DIL_PATTERN)
    for gi, (q_ref, k_ref, v_ref) in enumerate(((q0, k0, v0), (q1, k1, v1), (q2, k2, v2))):
        q = q_ref[0].astype(F32)
        k = k_ref[0].astype(F32)
        q = q * lax.rsqrt(jnp.mean(q * q, axis=-1, keepdims=True) + EPS) * qg_ref[...]
        k = k * lax.rsqrt(jnp.mean(k * k, axis=-1, keepdims=True) + EPS) * kg_ref[...]
        qs[gi] = q * (DIL_HD ** -0.5)
        ks[gi] = k
        vs[gi] = v_ref[0].astype(F32)

    qi = lax.broadcasted_iota(jnp.int32, (blk, blk), 0)
    kj = lax.broadcasted_iota(jnp.int32, (blk, blk), 1)
    cur_ok = kj <= qi
    prev_ok = kj >= qi

    for gi, (win, dil) in enumerate(DIL_PATTERN):
        assert win // dil == blk and (s // dil) % blk == 0
        nb = (s // dil) // blk
        for r in range(dil):
            for n in range(nb):
                def rows(m, dil=dil, r=r):
                    if dil == 1:
                        return pl.ds(m * blk, blk)
                    return pl.ds(m * blk * dil + r, blk, stride=dil)
                qb = qs[gi, rows(n), :]
                kc_ = ks[gi, rows(n), :]
                vc_ = vs[gi, rows(n), :]
                s_cur = jnp.where(cur_ok, _mm_nt(qb, kc_), -jnp.inf)
                if n > 0:
                    kp = ks[gi, rows(n - 1), :]
                    vp = vs[gi, rows(n - 1), :]
                    s_prev = jnp.where(prev_ok, _mm_nt(qb, kp), -jnp.inf)
                    m = jnp.maximum(jnp.max(s_cur, axis=-1, keepdims=True),
                                    jnp.max(s_prev, axis=-1, keepdims=True))
                    p_cur = jnp.exp(s_cur - m)
                    p_prev = jnp.exp(s_prev - m)
                    den = jnp.sum(p_cur, axis=-1, keepdims=True) + jnp.sum(p_prev, axis=-1, keepdims=True)
                    o = _mm(p_cur, vc_) + _mm(p_prev, vp)
                else:
                    m = jnp.max(s_cur, axis=-1, keepdims=True)
                    p_cur = jnp.exp(s_cur - m)
                    den = jnp.sum(p_cur, axis=-1, keepdims=True)
                    o = _mm(p_cur, vc_)
                os_[gi, rows(n), :] = o / den
                ls[gi, rows(n), :] = jnp.broadcast_to(m + jnp.log(den), (blk, DIL_HD))

    lse = [ls[gi] for gi in range(ngrp)]
    mx = functools.reduce(jnp.maximum, lse)
    ex = [jnp.exp(l - mx) for l in lse]
    tot = functools.reduce(lambda a, b: a + b, ex)
    y = functools.reduce(lambda a, b: a + b, [ex[gi] / tot * os_[gi] for gi in range(ngrp)])
    o_ref[0] = y.astype(o_ref.dtype)


def _dil(proj, q_norm_g, k_norm_g):
    bsz, s, _ = proj.shape
    ngrp = len(DIL_PATTERN)

    def specs(col0):
        blk0 = col0 // DIL_HD
        return [pl.BlockSpec((1, s, DIL_HD), lambda b, h, g=g: (b, 0, blk0 + g * DIL_HEADS_PER_GROUP + h))
                for g in range(ngrp)]

    gain = pl.BlockSpec((1, DIL_HD), lambda b, h: (0, 0))
    return pl.pallas_call(
        _dil_kernel,
        grid=(bsz, DIL_HEADS_PER_GROUP),
        in_specs=specs(COL_DQ) + specs(COL_DK) + specs(COL_DV) + [gain, gain],
        out_specs=pl.BlockSpec((1, s, DIL_HD), lambda b, h: (b, 0, h)),
        out_shape=jax.ShapeDtypeStruct((bsz, s, DIL_OUT_W), BF16),
        scratch_shapes=[pltpu.VMEM((ngrp, s, DIL_HD), F32) for _ in range(5)],
        compiler_params=_params("parallel", "parallel"),
        name="dilattn",
    )(*([proj] * 9), q_norm_g.reshape(1, DIL_HD), k_norm_g.reshape(1, DIL_HD))


def _route(logits, bias):
    e, tm = logits.shape
    per = e // N_GROUPS
    scores = _sigmoid(logits)
    sel = scores + bias
    neg = -jnp.inf
    sub = lax.broadcasted_iota(jnp.int32, (per, tm), 0)
    gs_rows = []
    for g in range(N_GROUPS):
        blk = sel[g * per:(g + 1) * per, :]
        m1 = jnp.max(blk, axis=0, keepdims=True)
        i1 = jnp.min(jnp.where(blk == m1, sub, per), axis=0, keepdims=True)
        m2 = jnp.max(jnp.where(sub == i1, neg, blk), axis=0, keepdims=True)
        gs_rows.append(m1 + m2)
    gs = jnp.concatenate(gs_rows, axis=0)
    gi = lax.broadcasted_iota(jnp.int32, (N_GROUPS, tm), 0)
    gsel = jnp.zeros((N_GROUPS, tm), F32)
    for _ in range(TOPK_GROUPS):
        m = jnp.max(gs, axis=0, keepdims=True)
        idx = jnp.min(jnp.where(gs == m, gi, N_GROUPS), axis=0, keepdims=True)
        hit = gi == idx
        gsel = jnp.where(hit, 1.0, gsel)
        gs = jnp.where(hit, neg, gs)
    cand = jnp.concatenate(
        [jnp.where(gsel[g:g + 1, :] > 0.0, sel[g * per:(g + 1) * per, :], neg) for g in range(N_GROUPS)], axis=0)
    ei = lax.broadcasted_iota(jnp.int32, (e, tm), 0)
    chosen = jnp.zeros((e, tm), F32)
    for _ in range(TOP_K):
        m = jnp.max(cand, axis=0, keepdims=True)
        idx = jnp.min(jnp.where(cand == m, ei, e), axis=0, keepdims=True)
        hit = ei == idx
        chosen = jnp.where(hit, scores, chosen)
        cand = jnp.where(hit, neg, cand)
    return chosen / jnp.sum(chosen, axis=0, keepdims=True) * ROUTE_SCALE


def _merge_kernel(x_ref, mod_ref, ya_ref, yb_ref, ga_ref, gb_ref, wg_ref, wd_ref, wo_ref,
                  gf_ref, wr_ref, rb_ref, x1_ref, h2_ref, comb_ref):
    gt1 = mod_ref[0, 2:3, :]
    sh2 = mod_ref[0, 3:4, :]
    sc2 = mod_ref[0, 4:5, :]
    ua = jnp.dot(ya_ref[0], wg_ref[...], preferred_element_type=F32)
    ub = jnp.dot(yb_ref[0], wd_ref[...], preferred_element_type=F32)
    merged = _sigmoid(ga_ref[0].astype(F32)) * ua + _sigmoid(gb_ref[0].astype(F32)) * ub
    x1 = x_ref[0] + gt1 * _mm(merged, wo_ref[...])
    x1_ref[0] = x1
    h2 = x1 * lax.rsqrt(jnp.mean(x1 * x1, axis=-1, keepdims=True) + EPS) * gf_ref[...] * (1.0 + sc2) + sh2
    h2_ref[0] = h2.astype(BF16)
    logits = lax.dot_general(wr_ref[...], h2, (((1,), (1,)), ((), ())),
                             preferred_element_type=F32, precision=lax.Precision.HIGHEST)
    comb = _route(logits, rb_ref[...])
    comb = jnp.concatenate([comb, jnp.zeros((LANES - N_EXPERTS, comb.shape[1]), F32)], axis=0)
    comb_ref[0] = comb.T


def _merge(x, mod3, y_a, y_b, proj, w_up_gdn, w_up_dil, w_out, g_ffn, w_router_t, router_bias):
    bsz, s, d = x.shape
    tm = 512
    row = lambda w: pl.BlockSpec((1, tm, w), lambda b, i: (b, i, 0))
    full = lambda a: pl.BlockSpec(a.shape, lambda b, i: (0,) * a.ndim)
    return pl.pallas_call(
        _merge_kernel,
        grid=(bsz, s // tm),
        in_specs=[row(d),
                  pl.BlockSpec((1, 6, d), lambda b, i: (b, 0, 0)),
                  row(GDN_W), row(DIL_OUT_W),
                  pl.BlockSpec((1, tm, d), lambda b, i: (b, i, COL_GA // d)),
                  pl.BlockSpec((1, tm, d), lambda b, i: (b, i, COL_GB // d)),
                  full(w_up_gdn), full(w_up_dil), full(w_out), full(g_ffn), full(w_router_t),
                  full(router_bias)],
        out_specs=[row(d), row(d), row(LANES)],
        out_shape=[jax.ShapeDtypeStruct((bsz, s, d), F32),
                   jax.ShapeDtypeStruct((bsz, s, d), BF16),
                   jax.ShapeDtypeStruct((bsz, s, LANES), F32)],
        compiler_params=_params("parallel", "parallel"),
        name="merge_router",
    )(x, mod3, y_a, y_b, proj, proj, w_up_gdn, w_up_dil, w_out, g_ffn, w_router_t, router_bias)


def _moe_kernel(h_ref, comb_ref, x1_ref, mod_ref, wg_ref, wu_ref, wd_ref, sg_ref, su_ref, sd_ref, o_ref):
    e = pl.program_id(2)
    h = h_ref[0]

    def swiglu(wg, wu):
        a = jnp.dot(h, wg, preferred_element_type=F32)
        u = jnp.dot(h, wu, preferred_element_type=F32)
        return _silu(a) * u

    @pl.when(e == 0)
    def _():
        o_ref[0] = _mm(swiglu(sg_ref[...], su_ref[...]), sd_ref[...])

    comb = comb_ref[0]
    lane = lax.broadcasted_iota(jnp.int32, comb.shape, 1)
    cb = jnp.sum(jnp.where(lane == e, comb, 0.0), axis=-1, keepdims=True)
    o_ref[0] += _mm(swiglu(wg_ref[0], wu_ref[0]) * cb, wd_ref[0])

    @pl.when(e == pl.num_programs(2) - 1)
    def _():
        gt2 = mod_ref[0, 5:6, :]
        o_ref[0] = x1_ref[0] + gt2 * o_ref[0]


def _moe(h2, comb, x1, mod3, wg, wu, wd, sg, su, sd):
    bsz, s, d = x1.shape
    ne, _, de = wg.shape
    tm = 1024
    row = lambda w: pl.BlockSpec((1, tm, w), lambda b, i, e: (b, i, 0))
    full = lambda a: pl.BlockSpec(a.shape, lambda b, i, e: (0,) * a.ndim)
    return pl.pallas_call(
        _moe_kernel,
        grid=(bsz, s // tm, ne),
        in_specs=[row(d), row(LANES), row(d),
                  pl.BlockSpec((1, 6, d), lambda b, i, e: (b, 0, 0)),
                  pl.BlockSpec((1, d, de), lambda b, i, e: (e, 0, 0)),
                  pl.BlockSpec((1, d, de), lambda b, i, e: (e, 0, 0)),
                  pl.BlockSpec((1, de, d), lambda b, i, e: (e, 0, 0)),
                  full(sg), full(su), full(sd)],
        out_specs=row(d),
        out_shape=jax.ShapeDtypeStruct((bsz, s, d), F32),
        compiler_params=_params("parallel", "parallel", "arbitrary"),
        name="moe",
    )(h2, comb, x1, mod3, wg, wu, wd, sg, su, sd)


def _layer(x, cmod, g_mix, w_in, conv_w, a_log, dt_bias, norm_g, qn_g, kn_g, w_up_gdn, w_up_dil, w_out,
           g_ffn, w_router, router_bias, w_eg, w_eu, w_ed, w_sg, w_su, w_sd):
    bsz, s, d = x.shape
    mod3 = cmod.reshape(bsz, 6, d)
    o_ba = 4 * GDN_W
    o_dq = o_ba + 2 * GDN_HEADS
    o_ga = o_dq + 3 * DIL_W
    w_main = jnp.concatenate([w_in[:, :o_ba], w_in[:, o_ga:], w_in[:, o_dq:o_ga]], axis=1).astype(BF16)
    pad_hi = LANES - 2 * GDN_HEADS
    w_ba = jnp.pad(w_in[:, o_ba:o_dq], ((0, 0), (0, pad_hi))).astype(BF16)
    alog_vec = jnp.pad(a_log, (GDN_HEADS, pad_hi)).reshape(1, LANES)
    dtb_vec = jnp.pad(dt_bias, (GDN_HEADS, pad_hi)).reshape(1, LANES)
    proj, gates = _inproj(x, mod3, g_mix, w_main, w_ba, alog_vec, dtb_vec)

    conv_w3 = conv_w.reshape(GDN_CONV, 3, GDN_W).transpose(1, 0, 2)
    y_a = _gdn(proj, gates, conv_w3, norm_g)
    y_b = _dil(proj, qn_g, kn_g)

    x1, h2, comb = _merge(x, mod3, y_a, y_b, proj, w_up_gdn.astype(BF16), w_up_dil.astype(BF16),
                          w_out.astype(BF16), g_ffn.reshape(1, d), w_router.T,
                          router_bias.reshape(N_EXPERTS, 1))
    return _moe(h2, comb, x1, mod3, w_eg.astype(BF16), w_eu.astype(BF16), w_ed.astype(BF16),
                w_sg.astype(BF16), w_su.astype(BF16), w_sd.astype(BF16))


def kernel(x, c, w_ada, b_ada, g_mix, w_in, gdn_conv_w, gdn_a_log, gdn_dt_bias, gdn_norm_g, dil_q_norm_g, dil_k_norm_g, w_up_gdn, w_up_dil, w_out, g_ffn, w_router, router_bias, w_exp_gate, w_exp_up, w_exp_down, w_sh_gate, w_sh_up, w_sh_down):
    for l in range(w_ada.shape[0]):
        cmod = _ada(c, w_ada[l], b_ada[l])
        x = _layer(x, cmod, g_mix[l], w_in[l], gdn_conv_w[l], gdn_a_log[l], gdn_dt_bias[l], gdn_norm_g[l],
                   dil_q_norm_g[l], dil_k_norm_g[l], w_up_gdn[l], w_up_dil[l], w_out[l], g_ffn[l],
                   w_router[l], router_bias[l], w_exp_gate[l], w_exp_up[l], w_exp_down[l],
                   w_sh_gate[l], w_sh_up[l], w_sh_down[l])
    return x
```

```python
import functools

import jax
import jax.numpy as jnp
from jax import lax
from jax.experimental import pallas as pl
from jax.experimental.pallas import tpu as pltpu

F32 = jnp.float32
BF16 = jnp.bfloat16

D_MODEL = 1024
GDN_HEADS = 8
GDN_DK = 128
GDN_CONV = 4
DIL_PATTERN = ((128, 1), (512, 4), (2048, 16))
DIL_HEADS_PER_GROUP = 4
DIL_HD = 128
DIL_BLOCK = 128
N_EXPERTS = 64
TOP_K = 8
N_GROUPS = 8
TOPK_GROUPS = 4
D_EXPERT = 256
ROUTE_SCALE = 2.5
EPS = 1e-6

GDN_W = GDN_HEADS * GDN_DK
DIL_HEADS = len(DIL_PATTERN) * DIL_HEADS_PER_GROUP
DIL_W = DIL_HEADS * DIL_HD
DIL_OUT_W = DIL_HEADS_PER_GROUP * DIL_HD
COL_GQ, COL_GK, COL_GV, COL_GZ = 0, GDN_W, 2 * GDN_W, 3 * GDN_W
COL_GA = 4 * GDN_W
COL_GB = COL_GA + D_MODEL
COL_DQ = COL_GB + D_MODEL
COL_DK = COL_DQ + DIL_W
COL_DV = COL_DK + DIL_W
PROJ_W = COL_DV + DIL_W
LANES = 128

VMEM_LIMIT = 56 * 1024 * 1024


def _params(*sem):
    return pltpu.CompilerParams(dimension_semantics=sem, vmem_limit_bytes=VMEM_LIMIT)


def _sigmoid(x):
    return 1.0 / (1.0 + jnp.exp(-x))


def _silu(x):
    return x * _sigmoid(x)


def _softplus(x):
    return jnp.maximum(x, 0.0) + jnp.log(1.0 + jnp.exp(-jnp.abs(x)))


def _mm(a, b):
    return jnp.dot(a.astype(BF16), b.astype(BF16), preferred_element_type=F32)


def _mm_nt(a, b):
    return lax.dot_general(a.astype(BF16), b.astype(BF16), (((1,), (1,)), ((), ())),
                           preferred_element_type=F32)


def _mm_tn(a, b):
    return lax.dot_general(a.astype(BF16), b.astype(BF16), (((0,), (0,)), ((), ())),
                           preferred_element_type=F32)


def _mm_f32(a, b):
    return jnp.dot(a, b, preferred_element_type=F32, precision=lax.Precision.HIGHEST)


def _ada_kernel(c_ref, w_ref, b_ref, o_ref):
    cs = _silu(c_ref[...])
    o_ref[...] = _mm_f32(cs, w_ref[...]) + b_ref[...]


def _ada(c, w_ada, b_ada):
    bsz, d = c.shape
    n = w_ada.shape[1]
    tn = 1536
    return pl.pallas_call(
        _ada_kernel,
        grid=(n // tn,),
        in_specs=[pl.BlockSpec((bsz, d), lambda j: (0, 0)),
                  pl.BlockSpec((d, tn), lambda j: (0, j)),
                  pl.BlockSpec((1, tn), lambda j: (0, j))],
        out_specs=pl.BlockSpec((bsz, tn), lambda j: (0, j)),
        out_shape=jax.ShapeDtypeStruct((bsz, n), F32),
        compiler_params=_params("parallel"),
        name="ada",
    )(c, w_ada, b_ada.reshape(1, n))


def _inproj_kernel(x_ref, mod_ref, g_ref, w_ref, wba_ref, alog_ref, dtb_ref,
                   proj_ref, gates_ref, h_scr):
    j = pl.program_id(2)

    @pl.when(j == 0)
    def _():
        x = x_ref[0]
        sh1 = mod_ref[0, 0:1, :]
        sc1 = mod_ref[0, 1:2, :]
        ms = jnp.mean(x * x, axis=-1, keepdims=True)
        h = x * lax.rsqrt(ms + EPS) * g_ref[...] * (1.0 + sc1) + sh1
        hb = h.astype(BF16)
        h_scr[...] = hb
        ba = jnp.dot(hb, wba_ref[...], preferred_element_type=F32)
        lane = lax.broadcasted_iota(jnp.int32, ba.shape, 1)
        beta = _sigmoid(ba)
        g = -jnp.exp(alog_ref[...]) * _softplus(ba + dtb_ref[...])
        gates_ref[0] = jnp.where(lane < GDN_HEADS, beta, g)

    proj_ref[0] = jnp.dot(h_scr[...], w_ref[...], preferred_element_type=F32).astype(BF16)


def _inproj(x, mod3, g_mix, w_main, w_ba, alog_vec, dtb_vec):
    bsz, s, d = x.shape
    tm, tn = 1024, 768
    return pl.pallas_call(
        _inproj_kernel,
        grid=(bsz, s // tm, PROJ_W // tn),
        in_specs=[pl.BlockSpec((1, tm, d), lambda b, i, j: (b, i, 0)),
                  pl.BlockSpec((1, 6, d), lambda b, i, j: (b, 0, 0)),
                  pl.BlockSpec((1, d), lambda b, i, j: (0, 0)),
                  pl.BlockSpec((d, tn), lambda b, i, j: (0, j)),
                  pl.BlockSpec((d, LANES), lambda b, i, j: (0, 0)),
                  pl.BlockSpec((1, LANES), lambda b, i, j: (0, 0)),
                  pl.BlockSpec((1, LANES), lambda b, i, j: (0, 0))],
        out_specs=[pl.BlockSpec((1, tm, tn), lambda b, i, j: (b, i, j)),
                   pl.BlockSpec((1, tm, LANES), lambda b, i, j: (b, i, 0))],
        out_shape=[jax.ShapeDtypeStruct((bsz, s, PROJ_W), BF16),
                   jax.ShapeDtypeStruct((bsz, s, LANES), F32)],
        scratch_shapes=[pltpu.VMEM((tm, d), BF16)],
        compiler_params=_params("parallel", "parallel", "arbitrary"),
        name="inproj",
    )(x, mod3, g_mix.reshape(1, d), w_main, w_ba, alog_vec, dtb_vec)


GDN_C = 128
HALO = 8


def _cumsum_rows(g):
    row = lax.broadcasted_iota(jnp.int32, g.shape, 0)
    sft = 1
    while sft < g.shape[0]:
        g = g + jnp.where(row >= sft, pltpu.roll(g, sft, 0), 0.0)
        sft *= 2
    return g


def _unit_lower_inverses(mats):
    n = mats[0].shape[0]
    row = lax.broadcasted_iota(jnp.int32, (n, n), 0)
    col = lax.broadcasted_iota(jnp.int32, (n, n), 1)
    eye = (row == col).astype(F32)
    ts = None
    b = 1
    while b < n:
        off = ((row // (2 * b)) == (col // (2 * b))) & ((row // b) % 2 == 1) & ((col // b) % 2 == 0)
        if b == 1:
            ts = [eye - jnp.where(off, a, 0.0) for a in mats]
        else:
            tb = [t.astype(BF16) for t in ts]
            inner = [jnp.dot(jnp.where(off, a, 0.0).astype(BF16), t, preferred_element_type=F32)
                     for a, t in zip(mats, tb)]
            ts = [t - jnp.dot(t16, i.astype(BF16), preferred_element_type=F32)
                  for t, t16, i in zip(ts, tb, inner)]
        b *= 2
    return ts


def _gdn_kernel(q_ref, k_ref, v_ref, z_ref, gates_ref, cw_ref, ng_ref, o_ref,
                qbuf, kbuf, vbuf, state):
    sblk = pl.program_id(1)
    c = GDN_C
    nh = GDN_HEADS
    w = nh * GDN_DK

    @pl.when(sblk == 0)
    def _():
        state[...] = jnp.zeros_like(state)
        for buf in (qbuf, kbuf, vbuf):
            buf[0:HALO, :] = jnp.zeros((HALO, w), F32)

    @pl.when(sblk > 0)
    def _():
        for buf in (qbuf, kbuf, vbuf):
            buf[0:HALO, :] = buf[c:c + HALO, :]

    conv = []
    for idx, (ref, buf) in enumerate(((q_ref, qbuf), (k_ref, kbuf), (v_ref, vbuf))):
        buf[HALO:HALO + c, :] = ref[0].astype(F32)
        acc = jnp.zeros((c, w), F32)
        for j in range(GDN_CONV):
            start = HALO - (GDN_CONV - 1) + j
            acc = acc + buf[start:start + c, :] * cw_ref[idx, j:j + 1, :]
        conv.append(_silu(acc))
    qc, kc, vc = conv

    gates = gates_ref[0]
    gcum = _cumsum_rows(gates)
    row = lax.broadcasted_iota(jnp.int32, (c, c), 0)
    col = lax.broadcasted_iota(jnp.int32, (c, c), 1)
    eye = row == col
    incl = col <= row
    strict = col < row

    heads = range(nh)
    hsl = [slice(h * GDN_DK, (h + 1) * GDN_DK) for h in heads]
    q = [qc[:, s] for s in hsl]
    k = [kc[:, s] for s in hsl]
    v = [vc[:, s] for s in hsl]
    q = [x * lax.rsqrt(jnp.sum(x * x, axis=-1, keepdims=True) + EPS) * (GDN_DK ** -0.5) for x in q]
    k = [x * lax.rsqrt(jnp.sum(x * x, axis=-1, keepdims=True) + EPS) for x in k]
    beta = [jnp.broadcast_to(gates[:, h:h + 1], (c, GDN_DK)) for h in heads]
    gc = [jnp.broadcast_to(gcum[:, nh + h:nh + h + 1], (c, GDN_DK)) for h in heads]
    gc_row = [jnp.sum(jnp.where(eye, x, 0.0), axis=0, keepdims=True) for x in gc]
    decay = [jnp.exp(jnp.where(incl, x - y, -jnp.inf)) for x, y in zip(gc, gc_row)]
    egc = [jnp.exp(x) for x in gc]
    kb = [x * b for x, b in zip(k, beta)]
    scores = [_mm_nt(jnp.concatenate([a, b], axis=0), x) for a, b, x in zip(kb, q, k)]
    a_mat = [jnp.where(strict, s[:c] * d, 0.0) for s, d in zip(scores, decay)]
    qk = [s[c:] * d for s, d in zip(scores, decay)]
    t_inv = _unit_lower_inverses(a_mat)
    sol = [_mm(t, jnp.concatenate([x * b, y * e], axis=1))
           for t, x, b, y, e in zip(t_inv, v, beta, kb, egc)]

    st = [state[h] for h in heads]
    ws = [_mm(jnp.concatenate([s[:, GDN_DK:], x * e], axis=0), m)
          for s, x, e, m in zip(sol, q, egc, st)]
    v_new = [s[:, :GDN_DK] - x[:c] for s, x in zip(sol, ws)]
    o = [x[c:] + _mm(a, b) for x, a, b in zip(ws, qk, v_new)]
    gc_last = [x[c - 1:c, :] for x in gc]
    k_dec = [x * jnp.exp(l - g) for x, l, g in zip(k, gc_last, gc)]
    for h in heads:
        state[h] = st[h] * jnp.exp(gc_last[h]) + _mm_tn(k_dec[h], v_new[h])
        y = o[h] * lax.rsqrt(jnp.mean(o[h] * o[h], axis=-1, keepdims=True) + EPS) * ng_ref[...]
        o_ref[0, :, hsl[h]] = (y * _silu(z_ref[0, :, hsl[h]].astype(F32))).astype(o_ref.dtype)


def _gdn(proj, gates, conv_w3, norm_g):
    bsz, s, _ = proj.shape
    w = GDN_W

    def col_spec(col0):
        return pl.BlockSpec((1, GDN_C, w), lambda b, i: (b, i, col0 // w))

    return pl.pallas_call(
        _gdn_kernel,
        grid=(bsz, s // GDN_C),
        in_specs=[col_spec(COL_GQ), col_spec(COL_GK), col_spec(COL_GV), col_spec(COL_GZ),
                  pl.BlockSpec((1, GDN_C, LANES), lambda b, i: (b, i, 0)),
                  pl.BlockSpec((3, GDN_CONV, w), lambda b, i: (0, 0, 0)),
                  pl.BlockSpec((1, GDN_DK), lambda b, i: (0, 0))],
        out_specs=pl.BlockSpec((1, GDN_C, w), lambda b, i: (b, i, 0)),
        out_shape=jax.ShapeDtypeStruct((bsz, s, GDN_W), BF16),
        scratch_shapes=[pltpu.VMEM((GDN_C + HALO, w), F32),
                        pltpu.VMEM((GDN_C + HALO, w), F32),
                        pltpu.VMEM((GDN_C + HALO, w), F32),
                        pltpu.VMEM((GDN_HEADS, GDN_DK, GDN_DK), F32)],
        compiler_params=_params("parallel", "arbitrary"),
        name="gdn",
    )(proj, proj, proj, proj, gates, conv_w3, norm_g.reshape(1, GDN_DK))


def _dil_kernel(q0, q1, q2, k0, k1, k2, v0, v1, v2, qg_ref, kg_ref, o_ref, qs, ks, vs, os_, ls):
    s = qs.shape[1]
    blk = DIL_BLOCK
    ngrp = len(DIL_PATTERN)
    for gi, (q_ref, k_ref, v_ref) in enumerate(((q0, k0, v0), (q1, k1, v1), (q2, k2, v2))):
        q = q_ref[0].astype(F32)
        k = k_ref[0].astype(F32)
        q = q * lax.rsqrt(jnp.mean(q * q, axis=-1, keepdims=True) + EPS) * qg_ref[...]
        k = k * lax.rsqrt(jnp.mean(k * k, axis=-1, keepdims=True) + EPS) * kg_ref[...]
        qs[gi] = q * (DIL_HD ** -0.5)
        ks[gi] = k
        vs[gi] = v_ref[0].astype(F32)

    qi = lax.broadcasted_iota(jnp.int32, (blk, blk), 0)
    kj = lax.broadcasted_iota(jnp.int32, (blk, blk), 1)
    cur_ok = kj <= qi
    prev_ok = kj >= qi

    for gi, (win, dil) in enumerate(DIL_PATTERN):
        assert win // dil == blk and (s // dil) % blk == 0
        nb = (s // dil) // blk
        for r in range(dil):
            for n in range(nb):
                def rows(m, dil=dil, r=r):
                    if dil == 1:
                        return pl.ds(m * blk, blk)
                    return pl.ds(m * blk * dil + r, blk, stride=dil)
                qb = qs[gi, rows(n), :]
                kc_ = ks[gi, rows(n), :]
                vc_ = vs[gi, rows(n), :]
                s_cur = jnp.where(cur_ok, _mm_nt(qb, kc_), -jnp.inf)
                if n > 0:
                    kp = ks[gi, rows(n - 1), :]
                    vp = vs[gi, rows(n - 1), :]
                    s_prev = jnp.where(prev_ok, _mm_nt(qb, kp), -jnp.inf)
                    m = jnp.maximum(jnp.max(s_cur, axis=-1, keepdims=True),
                                    jnp.max(s_prev, axis=-1, keepdims=True))
                    p_cur = jnp.exp(s_cur - m)
                    p_prev = jnp.exp(s_prev - m)
                    den = jnp.sum(p_cur, axis=-1, keepdims=True) + jnp.sum(p_prev, axis=-1, keepdims=True)
                    o = _mm(p_cur, vc_) + _mm(p_prev, vp)
                else:
                    m = jnp.max(s_cur, axis=-1, keepdims=True)
                    p_cur = jnp.exp(s_cur - m)
                    den = jnp.sum(p_cur, axis=-1, keepdims=True)
                    o = _mm(p_cur, vc_)
                os_[gi, rows(n), :] = o / den
                ls[gi, rows(n), :] = jnp.broadcast_to(m + jnp.log(den), (blk, DIL_HD))

    lse = [ls[gi] for gi in range(ngrp)]
    mx = functools.reduce(jnp.maximum, lse)
    ex = [jnp.exp(l - mx) for l in lse]
    tot = functools.reduce(lambda a, b: a + b, ex)
    y = functools.reduce(lambda a, b: a + b, [ex[gi] / tot * os_[gi] for gi in range(ngrp)])
    o_ref[0] = y.astype(o_ref.dtype)


def _dil(proj, q_norm_g, k_norm_g):
    bsz, s, _ = proj.shape
    ngrp = len(DIL_PATTERN)

    def specs(col0):
        blk0 = col0 // DIL_HD
        return [pl.BlockSpec((1, s, DIL_HD), lambda b, h, g=g: (b, 0, blk0 + g * DIL_HEADS_PER_GROUP + h))
                for g in range(ngrp)]

    gain = pl.BlockSpec((1, DIL_HD), lambda b, h: (0, 0))
    return pl.pallas_call(
        _dil_kernel,
        grid=(bsz, DIL_HEADS_PER_GROUP),
        in_specs=specs(COL_DQ) + specs(COL_DK) + specs(COL_DV) + [gain, gain],
        out_specs=pl.BlockSpec((1, s, DIL_HD), lambda b, h: (b, 0, h)),
        out_shape=jax.ShapeDtypeStruct((bsz, s, DIL_OUT_W), BF16),
        scratch_shapes=[pltpu.VMEM((ngrp, s, DIL_HD), F32) for _ in range(5)],
        compiler_params=_params("parallel", "parallel"),
        name="dilattn",
    )(*([proj] * 9), q_norm_g.reshape(1, DIL_HD), k_norm_g.reshape(1, DIL_HD))


def _route(logits, bias):
    e, tm = logits.shape
    per = e // N_GROUPS
    scores = _sigmoid(logits)
    sel = scores + bias
    neg = -jnp.inf
    sub = lax.broadcasted_iota(jnp.int32, (per, tm), 0)
    gs_rows = []
    for g in range(N_GROUPS):
        blk = sel[g * per:(g + 1) * per, :]
        m1 = jnp.max(blk, axis=0, keepdims=True)
        i1 = jnp.min(jnp.where(blk == m1, sub, per), axis=0, keepdims=True)
        m2 = jnp.max(jnp.where(sub == i1, neg, blk), axis=0, keepdims=True)
        gs_rows.append(m1 + m2)
    gs = jnp.concatenate(gs_rows, axis=0)
    gi = lax.broadcasted_iota(jnp.int32, (N_GROUPS, tm), 0)
    gsel = jnp.zeros((N_GROUPS, tm), F32)
    for _ in range(TOPK_GROUPS):
        m = jnp.max(gs, axis=0, keepdims=True)
        idx = jnp.min(jnp.where(gs == m, gi, N_GROUPS), axis=0, keepdims=True)
        hit = gi == idx
        gsel = jnp.where(hit, 1.0, gsel)
        gs = jnp.where(hit, neg, gs)
    cand = jnp.concatenate(
        [jnp.where(gsel[g:g + 1, :] > 0.0, sel[g * per:(g + 1) * per, :], neg) for g in range(N_GROUPS)], axis=0)
    ei = lax.broadcasted_iota(jnp.int32, (e, tm), 0)
    chosen = jnp.zeros((e, tm), F32)
    for _ in range(TOP_K):
        m = jnp.max(cand, axis=0, keepdims=True)
        idx = jnp.min(jnp.where(cand == m, ei, e), axis=0, keepdims=True)
        hit = ei == idx
        chosen = jnp.where(hit, scores, chosen)
        cand = jnp.where(hit, neg, cand)
    return chosen / jnp.sum(chosen, axis=0, keepdims=True) * ROUTE_SCALE


def _merge_kernel(x_ref, mod_ref, ya_ref, yb_ref, ga_ref, gb_ref, wg_ref, wd_ref, wo_ref,
                  gf_ref, wr_ref, rb_ref, x1_ref, h2_ref, comb_ref):
    gt1 = mod_ref[0, 2:3, :]
    sh2 = mod_ref[0, 3:4, :]
    sc2 = mod_ref[0, 4:5, :]
    ua = jnp.dot(ya_ref[0], wg_ref[...], preferred_element_type=F32)
    ub = jnp.dot(yb_ref[0], wd_ref[...], preferred_element_type=F32)
    merged = _sigmoid(ga_ref[0].astype(F32)) * ua + _sigmoid(gb_ref[0].astype(F32)) * ub
    x1 = x_ref[0] + gt1 * _mm(merged, wo_ref[...])
    x1_ref[0] = x1
    h2 = x1 * lax.rsqrt(jnp.mean(x1 * x1, axis=-1, keepdims=True) + EPS) * gf_ref[...] * (1.0 + sc2) + sh2
    h2_ref[0] = h2.astype(BF16)
    logits = lax.dot_general(wr_ref[...], h2, (((1,), (1,)), ((), ())),
                             preferred_element_type=F32, precision=lax.Precision.HIGHEST)
    comb = _route(logits, rb_ref[...])
    comb = jnp.concatenate([comb, jnp.zeros((LANES - N_EXPERTS, comb.shape[1]), F32)], axis=0)
    comb_ref[0] = comb.T


def _merge(x, mod3, y_a, y_b, proj, w_up_gdn, w_up_dil, w_out, g_ffn, w_router_t, router_bias):
    bsz, s, d = x.shape
    tm = 512
    row = lambda w: pl.BlockSpec((1, tm, w), lambda b, i: (b, i, 0))
    full = lambda a: pl.BlockSpec(a.shape, lambda b, i: (0,) * a.ndim)
    return pl.pallas_call(
        _merge_kernel,
        grid=(bsz, s // tm),
        in_specs=[row(d),
                  pl.BlockSpec((1, 6, d), lambda b, i: (b, 0, 0)),
                  row(GDN_W), row(DIL_OUT_W),
                  pl.BlockSpec((1, tm, d), lambda b, i: (b, i, COL_GA // d)),
                  pl.BlockSpec((1, tm, d), lambda b, i: (b, i, COL_GB // d)),
                  full(w_up_gdn), full(w_up_dil), full(w_out), full(g_ffn), full(w_router_t),
                  full(router_bias)],
        out_specs=[row(d), row(d), row(LANES)],
        out_shape=[jax.ShapeDtypeStruct((bsz, s, d), F32),
                   jax.ShapeDtypeStruct((bsz, s, d), BF16),
                   jax.ShapeDtypeStruct((bsz, s, LANES), F32)],
        compiler_params=_params("parallel", "parallel"),
        name="merge_router",
    )(x, mod3, y_a, y_b, proj, proj, w_up_gdn, w_up_dil, w_out, g_ffn, w_router_t, router_bias)


def _moe_kernel(h_ref, comb_ref, x1_ref, mod_ref, wg_ref, wu_ref, wd_ref, sg_ref, su_ref, sd_ref, o_ref):
    e = pl.program_id(2)
    h = h_ref[0]

    def swiglu(wg, wu):
        a = jnp.dot(h, wg, preferred_element_type=F32)
        u = jnp.dot(h, wu, preferred_element_type=F32)
        return _silu(a) * u

    @pl.when(e == 0)
    def _():
        o_ref[0] = _mm(swiglu(sg_ref[...], su_ref[...]), sd_ref[...])

    comb = comb_ref[0]
    lane = lax.broadcasted_iota(jnp.int32, comb.shape, 1)
    cb = jnp.sum(jnp.where(lane == e, comb, 0.0), axis=-1, keepdims=True)
    o_ref[0] += _mm(swiglu(wg_ref[0], wu_ref[0]) * cb, wd_ref[0])

    @pl.when(e == pl.num_programs(2) - 1)
    def _():
        gt2 = mod_ref[0, 5:6, :]
        o_ref[0] = x1_ref[0] + gt2 * o_ref[0]


def _moe(h2, comb, x1, mod3, wg, wu, wd, sg, su, sd):
    bsz, s, d = x1.shape
    ne, _, de = wg.shape
    tm = 1024
    row = lambda w: pl.BlockSpec((1, tm, w), lambda b, i, e: (b, i, 0))
    full = lambda a: pl.BlockSpec(a.shape, lambda b, i, e: (0,) * a.ndim)
    return pl.pallas_call(
        _moe_kernel,
        grid=(bsz, s // tm, ne),
        in_specs=[row(d), row(LANES), row(d),
                  pl.BlockSpec((1, 6, d), lambda b, i, e: (b, 0, 0)),
                  pl.BlockSpec((1, d, de), lambda b, i, e: (e, 0, 0)),
                  pl.BlockSpec((1, d, de), lambda b, i, e: (e, 0, 0)),
                  pl.BlockSpec((1, de, d), lambda b, i, e: (e, 0, 0)),
                  full(sg), full(su), full(sd)],
        out_specs=row(d),
        out_shape=jax.ShapeDtypeStruct((bsz, s, d), F32),
        compiler_params=_params("parallel", "parallel", "arbitrary"),
        name="moe",
    )(h2, comb, x1, mod3, wg, wu, wd, sg, su, sd)


def _layer(x, cmod, g_mix, w_in, conv_w, a_log, dt_bias, norm_g, qn_g, kn_g, w_up_gdn, w_up_dil, w_out,
           g_ffn, w_router, router_bias, w_eg, w_eu, w_ed, w_sg, w_su, w_sd):
    bsz, s, d = x.shape
    mod3 = cmod.reshape(bsz, 6, d)
    o_ba = 4 * GDN_W
    o_dq = o_ba + 2 * GDN_HEADS
    o_ga = o_dq + 3 * DIL_W
    w_main = jnp.concatenate([w_in[:, :o_ba], w_in[:, o_ga:], w_in[:, o_dq:o_ga]], axis=1).astype(BF16)
    pad_hi = LANES - 2 * GDN_HEADS
    w_ba = jnp.pad(w_in[:, o_ba:o_dq], ((0, 0), (0, pad_hi))).astype(BF16)
    alog_vec = jnp.pad(a_log, (GDN_HEADS, pad_hi)).reshape(1, LANES)
    dtb_vec = jnp.pad(dt_bias, (GDN_HEADS, pad_hi)).reshape(1, LANES)
    proj, gates = _inproj(x, mod3, g_mix, w_main, w_ba, alog_vec, dtb_vec)

    conv_w3 = conv_w.reshape(GDN_CONV, 3, GDN_W).transpose(1, 0, 2)
    y_a = _gdn(proj, gates, conv_w3, norm_g)
    y_b = _dil(proj, qn_g, kn_g)

    x1, h2, comb = _merge(x, mod3, y_a, y_b, proj, w_up_gdn.astype(BF16), w_up_dil.astype(BF16),
                          w_out.astype(BF16), g_ffn.reshape(1, d), w_router.T,
                          router_bias.reshape(N_EXPERTS, 1))
    return _moe(h2, comb, x1, mod3, w_eg.astype(BF16), w_eu.astype(BF16), w_ed.astype(BF16),
                w_sg.astype(BF16), w_su.astype(BF16), w_sd.astype(BF16))


def kernel(x, c, w_ada, b_ada, g_mix, w_in, gdn_conv_w, gdn_a_log, gdn_dt_bias, gdn_norm_g, dil_q_norm_g, dil_k_norm_g, w_up_gdn, w_up_dil, w_out, g_ffn, w_router, router_bias, w_exp_gate, w_exp_up, w_exp_down, w_sh_gate, w_sh_up, w_sh_down):
    for l in range(w_ada.shape[0]):
        cmod = _ada(c, w_ada[l], b_ada[l])
        x = _layer(x, cmod, g_mix[l], w_in[l], gdn_conv_w[l], gdn_a_log[l], gdn_dt_bias[l], gdn_norm_g[l],
                   dil_q_norm_g[l], dil_k_norm_g[l], w_up_gdn[l], w_up_dil[l], w_out[l], g_ffn[l],
                   w_router[l], router_bias[l], w_exp_gate[l], w_exp_up[l], w_exp_down[l],
                   w_sh_gate[l], w_sh_up[l], w_sh_down[l])
    return x
```

```python
import functools

import jax
import jax.numpy as jnp
from jax import lax
from jax.experimental import pallas as pl
from jax.experimental.pallas import tpu as pltpu

F32 = jnp.float32
BF16 = jnp.bfloat16

D_MODEL = 1024
GDN_HEADS = 8
GDN_DK = 128
GDN_CONV = 4
DIL_PATTERN = ((128, 1), (512, 4), (2048, 16))
DIL_HEADS_PER_GROUP = 4
DIL_HD = 128
DIL_BLOCK = 128
N_EXPERTS = 64
TOP_K = 8
N_GROUPS = 8
TOPK_GROUPS = 4
D_EXPERT = 256
ROUTE_SCALE = 2.5
EPS = 1e-6

GDN_W = GDN_HEADS * GDN_DK
DIL_HEADS = len(DIL_PATTERN) * DIL_HEADS_PER_GROUP
DIL_W = DIL_HEADS * DIL_HD
DIL_OUT_W = DIL_HEADS_PER_GROUP * DIL_HD
COL_GQ, COL_GK, COL_GV, COL_GZ = 0, GDN_W, 2 * GDN_W, 3 * GDN_W
COL_GA = 4 * GDN_W
COL_GB = COL_GA + D_MODEL
COL_DQ = COL_GB + D_MODEL
COL_DK = COL_DQ + DIL_W
COL_DV = COL_DK + DIL_W
PROJ_W = COL_DV + DIL_W
LANES = 128

VMEM_LIMIT = 56 * 1024 * 1024


def _params(*sem):
    return pltpu.CompilerParams(dimension_semantics=sem, vmem_limit_bytes=VMEM_LIMIT)


def _sigmoid(x):
    return 1.0 / (1.0 + jnp.exp(-x))


def _silu(x):
    return x * _sigmoid(x)


def _softplus(x):
    return jnp.maximum(x, 0.0) + jnp.log(1.0 + jnp.exp(-jnp.abs(x)))


def _mm(a, b):
    return jnp.dot(a.astype(BF16), b.astype(BF16), preferred_element_type=F32)


def _mm_nt(a, b):
    return lax.dot_general(a.astype(BF16), b.astype(BF16), (((1,), (1,)), ((), ())),
                           preferred_element_type=F32)


def _mm_tn(a, b):
    return lax.dot_general(a.astype(BF16), b.astype(BF16), (((0,), (0,)), ((), ())),
                           preferred_element_type=F32)


def _mm_f32(a, b):
    return jnp.dot(a, b, preferred_element_type=F32, precision=lax.Precision.HIGHEST)


def _ada_kernel(c_ref, w_ref, b_ref, o_ref):
    cs = _silu(c_ref[...])
    o_ref[...] = _mm_f32(cs, w_ref[...]) + b_ref[...]


def _ada(c, w_ada, b_ada):
    bsz, d = c.shape
    n = w_ada.shape[1]
    tn = 1536
    return pl.pallas_call(
        _ada_kernel,
        grid=(n // tn,),
        in_specs=[pl.BlockSpec((bsz, d), lambda j: (0, 0)),
                  pl.BlockSpec((d, tn), lambda j: (0, j)),
                  pl.BlockSpec((1, tn), lambda j: (0, j))],
        out_specs=pl.BlockSpec((bsz, tn), lambda j: (0, j)),
        out_shape=jax.ShapeDtypeStruct((bsz, n), F32),
        compiler_params=_params("parallel"),
        name="ada",
    )(c, w_ada, b_ada.reshape(1, n))


def _inproj_kernel(x_ref, mod_ref, g_ref, w_ref, wba_ref, alog_ref, dtb_ref,
                   proj_ref, gates_ref, h_scr):
    j = pl.program_id(2)

    @pl.when(j == 0)
    def _():
        x = x_ref[0]
        sh1 = mod_ref[0, 0:1, :]
        sc1 = mod_ref[0, 1:2, :]
        ms = jnp.mean(x * x, axis=-1, keepdims=True)
        h = x * lax.rsqrt(ms + EPS) * g_ref[...] * (1.0 + sc1) + sh1
        hb = h.astype(BF16)
        h_scr[...] = hb
        ba = jnp.dot(hb, wba_ref[...], preferred_element_type=F32)
        lane = lax.broadcasted_iota(jnp.int32, ba.shape, 1)
        beta = _sigmoid(ba)
        g = -jnp.exp(alog_ref[...]) * _softplus(ba + dtb_ref[...])
        gates_ref[0] = jnp.where(lane < GDN_HEADS, beta, g)

    proj_ref[0] = jnp.dot(h_scr[...], w_ref[...], preferred_element_type=F32).astype(BF16)


def _inproj(x, mod3, g_mix, w_main, w_ba, alog_vec, dtb_vec):
    bsz, s, d = x.shape
    tm, tn = 1024, 768
    return pl.pallas_call(
        _inproj_kernel,
        grid=(bsz, s // tm, PROJ_W // tn),
        in_specs=[pl.BlockSpec((1, tm, d), lambda b, i, j: (b, i, 0)),
                  pl.BlockSpec((1, 6, d), lambda b, i, j: (b, 0, 0)),
                  pl.BlockSpec((1, d), lambda b, i, j: (0, 0)),
                  pl.BlockSpec((d, tn), lambda b, i, j: (0, j)),
                  pl.BlockSpec((d, LANES), lambda b, i, j: (0, 0)),
                  pl.BlockSpec((1, LANES), lambda b, i, j: (0, 0)),
                  pl.BlockSpec((1, LANES), lambda b, i, j: (0, 0))],
        out_specs=[pl.BlockSpec((1, tm, tn), lambda b, i, j: (b, i, j)),
                   pl.BlockSpec((1, tm, LANES), lambda b, i, j: (b, i, 0))],
        out_shape=[jax.ShapeDtypeStruct((bsz, s, PROJ_W), BF16),
                   jax.ShapeDtypeStruct((bsz, s, LANES), F32)],
        scratch_shapes=[pltpu.VMEM((tm, d), BF16)],
        compiler_params=_params("parallel", "parallel", "arbitrary"),
        name="inproj",
    )(x, mod3, g_mix.reshape(1, d), w_main, w_ba, alog_vec, dtb_vec)


GDN_C = 128
HALO = 8


def _cumsum_rows(g):
    row = lax.broadcasted_iota(jnp.int32, g.shape, 0)
    sft = 1
    while sft < g.shape[0]:
        g = g + jnp.where(row >= sft, pltpu.roll(g, sft, 0), 0.0)
        sft *= 2
    return g


def _unit_lower_inverses(mats):
    n = mats[0].shape[0]
    row = lax.broadcasted_iota(jnp.int32, (n, n), 0)
    col = lax.broadcasted_iota(jnp.int32, (n, n), 1)
    eye = (row == col).astype(F32)
    ts = None
    b = 1
    while b < n:
        off = ((row // (2 * b)) == (col // (2 * b))) & ((row // b) % 2 == 1) & ((col // b) % 2 == 0)
        if b == 1:
            ts = [eye - jnp.where(off, a, 0.0) for a in mats]
        else:
            tb = [t.astype(BF16) for t in ts]
            inner = [jnp.dot(jnp.where(off, a, 0.0).astype(BF16), t, preferred_element_type=F32)
                     for a, t in zip(mats, tb)]
            ts = [t - jnp.dot(t16, i.astype(BF16), preferred_element_type=F32)
                  for t, t16, i in zip(ts, tb, inner)]
        b *= 2
    return ts


def _gdn_kernel(q_ref, k_ref, v_ref, z_ref, gates_ref, cw_ref, ng_ref, o_ref,
                qbuf, kbuf, vbuf, state):
    sblk = pl.program_id(1)
    c = GDN_C
    nh = GDN_HEADS
    w = nh * GDN_DK

    @pl.when(sblk == 0)
    def _():
        state[...] = jnp.zeros_like(state)
        for buf in (qbuf, kbuf, vbuf):
            buf[0:HALO, :] = jnp.zeros((HALO, w), F32)

    @pl.when(sblk > 0)
    def _():
        for buf in (qbuf, kbuf, vbuf):
            buf[0:HALO, :] = buf[c:c + HALO, :]

    conv = []
    for idx, (ref, buf) in enumerate(((q_ref, qbuf), (k_ref, kbuf), (v_ref, vbuf))):
        buf[HALO:HALO + c, :] = ref[0].astype(F32)
        acc = jnp.zeros((c, w), F32)
        for j in range(GDN_CONV):
            start = HALO - (GDN_CONV - 1) + j
            acc = acc + buf[start:start + c, :] * cw_ref[idx, j:j + 1, :]
        conv.append(_silu(acc))
    qc, kc, vc = conv

    gates = gates_ref[0]
    gcum = _cumsum_rows(gates)
    row = lax.broadcasted_iota(jnp.int32, (c, c), 0)
    col = lax.broadcasted_iota(jnp.int32, (c, c), 1)
    eye = row == col
    incl = col <= row
    strict = col < row

    heads = range(nh)
    hsl = [slice(h * GDN_DK, (h + 1) * GDN_DK) for h in heads]
    q = [qc[:, s] for s in hsl]
    k = [kc[:, s] for s in hsl]
    v = [vc[:, s] for s in hsl]
    q = [x * lax.rsqrt(jnp.sum(x * x, axis=-1, keepdims=True) + EPS) * (GDN_DK ** -0.5) for x in q]
    k = [x * lax.rsqrt(jnp.sum(x * x, axis=-1, keepdims=True) + EPS) for x in k]
    beta = [jnp.broadcast_to(gates[:, h:h + 1], (c, GDN_DK)) for h in heads]
    gc = [jnp.broadcast_to(gcum[:, nh + h:nh + h + 1], (c, GDN_DK)) for h in heads]
    gc_row = [jnp.sum(jnp.where(eye, x, 0.0), axis=0, keepdims=True) for x in gc]
    decay = [jnp.exp(jnp.where(incl, x - y, -jnp.inf)) for x, y in zip(gc, gc_row)]
    egc = [jnp.exp(x) for x in gc]
    kb = [x * b for x, b in zip(k, beta)]
    scores = [_mm_nt(jnp.concatenate([a, b], axis=0), x) for a, b, x in zip(kb, q, k)]
    a_mat = [jnp.where(strict, s[:c] * d, 0.0) for s, d in zip(scores, decay)]
    qk = [s[c:] * d for s, d in zip(scores, decay)]
    t_inv = _unit_lower_inverses(a_mat)
    sol = [_mm(t, jnp.concatenate([x * b, y * e], axis=1))
           for t, x, b, y, e in zip(t_inv, v, beta, kb, egc)]

    st = [state[h] for h in heads]
    ws = [_mm(jnp.concatenate([s[:, GDN_DK:], x * e], axis=0), m)
          for s, x, e, m in zip(sol, q, egc, st)]
    v_new = [s[:, :GDN_DK] - x[:c] for s, x in zip(sol, ws)]
    o = [x[c:] + _mm(a, b) for x, a, b in zip(ws, qk, v_new)]
    gc_last = [x[c - 1:c, :] for x in gc]
    k_dec = [x * jnp.exp(l - g) for x, l, g in zip(k, gc_last, gc)]
    for h in heads:
        state[h] = st[h] * jnp.exp(gc_last[h]) + _mm_tn(k_dec[h], v_new[h])
        y = o[h] * lax.rsqrt(jnp.mean(o[h] * o[h], axis=-1, keepdims=True) + EPS) * ng_ref[...]
        o_ref[0, :, hsl[h]] = (y * _silu(z_ref[0, :, hsl[h]].astype(F32))).astype(o_ref.dtype)


def _gdn(proj, gates, conv_w3, norm_g):
    bsz, s, _ = proj.shape
    w = GDN_W

    def col_spec(col0):
        return pl.BlockSpec((1, GDN_C, w), lambda b, i: (b, i, col0 // w))

    return pl.pallas_call(
        _gdn_kernel,
        grid=(bsz, s // GDN_C),
        in_specs=[col_spec(COL_GQ), col_spec(COL_GK), col_spec(COL_GV), col_spec(COL_GZ),
                  pl.BlockSpec((1, GDN_C, LANES), lambda b, i: (b, i, 0)),
                  pl.BlockSpec((3, GDN_CONV, w), lambda b, i: (0, 0, 0)),
                  pl.BlockSpec((1, GDN_DK), lambda b, i: (0, 0))],
        out_specs=pl.BlockSpec((1, GDN_C, w), lambda b, i: (b, i, 0)),
        out_shape=jax.ShapeDtypeStruct((bsz, s, GDN_W), BF16),
        scratch_shapes=[pltpu.VMEM((GDN_C + HALO, w), F32),
                        pltpu.VMEM((GDN_C + HALO, w), F32),
                        pltpu.VMEM((GDN_C + HALO, w), F32),
                        pltpu.VMEM((GDN_HEADS, GDN_DK, GDN_DK), F32)],
        compiler_params=_params("parallel", "arbitrary"),
        name="gdn",
    )(proj, proj, proj, proj, gates, conv_w3, norm_g.reshape(1, GDN_DK))


def _dil_kernel(q0, q1, q2, k0, k1, k2, v0, v1, v2, qg_ref, kg_ref, o_ref, qs, ks, vs, os_, ls):
    s = qs.shape[1]
    blk = DIL_BLOCK
    ngrp = len(DIL_PATTERN)
    for gi, (q_ref, k_ref, v_ref) in enumerate(((q0, k0, v0), (q1, k1, v1), (q2, k2, v2))):
        q = q_ref[0].astype(F32)
        k = k_ref[0].astype(F32)
        q = q * lax.rsqrt(jnp.mean(q * q, axis=-1, keepdims=True) + EPS) * qg_ref[...]
        k = k * lax.rsqrt(jnp.mean(k * k, axis=-1, keepdims=True) + EPS) * kg_ref[...]
        qs[gi] = q * (DIL_HD ** -0.5)
        ks[gi] = k
        vs[gi] = v_ref[0].astype(F32)

    qi = lax.broadcasted_iota(jnp.int32, (blk, blk), 0)
    kj = lax.broadcasted_iota(jnp.int32, (blk, blk), 1)
    cur_ok = kj <= qi
    prev_ok = kj >= qi

    for gi, (win, dil) in enumerate(DIL_PATTERN):
        assert win // dil == blk and (s // dil) % blk == 0
        nb = (s // dil) // blk
        for r in range(dil):
            for n in range(nb):
                def rows(m, dil=dil, r=r):
                    if dil == 1:
                        return pl.ds(m * blk, blk)
                    return pl.ds(m * blk * dil + r, blk, stride=dil)
                qb = qs[gi, rows(n), :]
                kc_ = ks[gi, rows(n), :]
                vc_ = vs[gi, rows(n), :]
                s_cur = jnp.where(cur_ok, _mm_nt(qb, kc_), -jnp.inf)
                if n > 0:
                    kp = ks[gi, rows(n - 1), :]
                    vp = vs[gi, rows(n - 1), :]
                    s_prev = jnp.where(prev_ok, _mm_nt(qb, kp), -jnp.inf)
                    m = jnp.maximum(jnp.max(s_cur, axis=-1, keepdims=True),
                                    jnp.max(s_prev, axis=-1, keepdims=True))
                    p_cur = jnp.exp(s_cur - m)
                    p_prev = jnp.exp(s_prev - m)
                    den = jnp.sum(p_cur, axis=-1, keepdims=True) + jnp.sum(p_prev, axis=-1, keepdims=True)
                    o = _mm(p_cur, vc_) + _mm(p_prev, vp)
                else:
                    m = jnp.max(s_cur, axis=-1, keepdims=True)
                    p_cur = jnp.exp(s_cur - m)
                    den = jnp.sum(p_cur, axis=-1, keepdims=True)
                    o = _mm(p_cur, vc_)
                os_[gi, rows(n), :] = o / den
                ls[gi, rows(n), :] = jnp.broadcast_to(m + jnp.log(den), (blk, DIL_HD))

    lse = [ls[gi] for gi in range(ngrp)]
    mx = functools.reduce(jnp.maximum, lse)
    ex = [jnp.exp(l - mx) for l in lse]
    tot = functools.reduce(lambda a, b: a + b, ex)
    y = functools.reduce(lambda a, b: a + b, [ex[gi] / tot * os_[gi] for gi in range(ngrp)])
    o_ref[0] = y.astype(o_ref.dtype)


def _dil(proj, q_norm_g, k_norm_g):
    bsz, s, _ = proj.shape
    ngrp = len(DIL_PATTERN)

    def specs(col0):
        blk0 = col0 // DIL_HD
        return [pl.BlockSpec((1, s, DIL_HD), lambda b, h, g=g: (b, 0, blk0 + g * DIL_HEADS_PER_GROUP + h))
                for g in range(ngrp)]

    gain = pl.BlockSpec((1, DIL_HD), lambda b, h: (0, 0))
    return pl.pallas_call(
        _dil_kernel,
        grid=(bsz, DIL_HEADS_PER_GROUP),
        in_specs=specs(COL_DQ) + specs(COL_DK) + specs(COL_DV) + [gain, gain],
        out_specs=pl.BlockSpec((1, s, DIL_HD), lambda b, h: (b, 0, h)),
        out_shape=jax.ShapeDtypeStruct((bsz, s, DIL_OUT_W), BF16),
        scratch_shapes=[pltpu.VMEM((ngrp, s, DIL_HD), F32) for _ in range(5)],
        compiler_params=_params("parallel", "parallel"),
        name="dilattn",
    )(*([proj] * 9), q_norm_g.reshape(1, DIL_HD), k_norm_g.reshape(1, DIL_HD))


def _route(logits, bias):
    e, tm = logits.shape
    per = e // N_GROUPS
    scores = _sigmoid(logits)
    sel = scores + bias
    neg = -jnp.inf
    sub = lax.broadcasted_iota(jnp.int32, (per, tm), 0)
    gs_rows = []
    for g in range(N_GROUPS):
        blk = sel[g * per:(g + 1) * per, :]
        m1 = jnp.max(blk, axis=0, keepdims=True)
        i1 = jnp.min(jnp.where(blk == m1, sub, per), axis=0, keepdims=True)
        m2 = jnp.max(jnp.where(sub == i1, neg, blk), axis=0, keepdims=True)
        gs_rows.append(m1 + m2)
    gs = jnp.concatenate(gs_rows, axis=0)
    gi = lax.broadcasted_iota(jnp.int32, (N_GROUPS, tm), 0)
    gsel = jnp.zeros((N_GROUPS, tm), F32)
    for _ in range(TOPK_GROUPS):
        m = jnp.max(gs, axis=0, keepdims=True)
        idx = jnp.min(jnp.where(gs == m, gi, N_GROUPS), axis=0, keepdims=True)
        hit = gi == idx
        gsel = jnp.where(hit, 1.0, gsel)
        gs = jnp.where(hit, neg, gs)
    cand = jnp.concatenate(
        [jnp.where(gsel[g:g + 1, :] > 0.0, sel[g * per:(g + 1) * per, :], neg) for g in range(N_GROUPS)], axis=0)
    ei = lax.broadcasted_iota(jnp.int32, (e, tm), 0)
    chosen = jnp.zeros((e, tm), F32)
    for _ in range(TOP_K):
        m = jnp.max(cand, axis=0, keepdims=True)
        idx = jnp.min(jnp.where(cand == m, ei, e), axis=0, keepdims=True)
        hit = ei == idx
        chosen = jnp.where(hit, scores, chosen)
        cand = jnp.where(hit, neg, cand)
    return chosen / jnp.sum(chosen, axis=0, keepdims=True) * ROUTE_SCALE


def _merge_kernel(x_ref, mod_ref, ya_ref, yb_ref, ga_ref, gb_ref, wg_ref, wd_ref, wo_ref,
                  gf_ref, wr_ref, rb_ref, x1_ref, h2_ref, comb_ref, rank_ref, cnt_ref):
    gt1 = mod_ref[0, 2:3, :]
    sh2 = mod_ref[0, 3:4, :]
    sc2 = mod_ref[0, 4:5, :]
    ua = jnp.dot(ya_ref[0], wg_ref[...], preferred_element_type=F32)
    ub = jnp.dot(yb_ref[0], wd_ref[...], preferred_element_type=F32)
    merged = _sigmoid(ga_ref[0].astype(F32)) * ua + _sigmoid(gb_ref[0].astype(F32)) * ub
    x1 = x_ref[0] + gt1 * _mm(merged, wo_ref[...])
    x1_ref[0] = x1
    h2 = x1 * lax.rsqrt(jnp.mean(x1 * x1, axis=-1, keepdims=True) + EPS) * gf_ref[...] * (1.0 + sc2) + sh2
    h2_ref[0] = h2.astype(BF16)
    logits = lax.dot_general(wr_ref[...], h2, (((1,), (1,)), ((), ())),
                             preferred_element_type=F32, precision=lax.Precision.HIGHEST)
    comb = _route(logits, rb_ref[...])
    comb_ref[0] = comb
    tb = MOE_TB
    before = (lax.broadcasted_iota(jnp.int32, (tb, tb), 0) < lax.broadcasted_iota(jnp.int32, (tb, tb), 1))
    before = before.astype(BF16)
    for j in range(comb.shape[1] // tb):
        picked = (comb[:, j * tb:(j + 1) * tb] > 0.0).astype(F32)
        rank_ref[0, :, j * tb:(j + 1) * tb] = jnp.dot(picked.astype(BF16), before, preferred_element_type=F32)
        cnt_ref[0, j] = jnp.broadcast_to(jnp.sum(picked, axis=1, keepdims=True), (N_EXPERTS, LANES))


def _merge(x, mod3, y_a, y_b, proj, w_up_gdn, w_up_dil, w_out, g_ffn, w_router_t, router_bias):
    bsz, s, d = x.shape
    tm = 512
    row = lambda w: pl.BlockSpec((1, tm, w), lambda b, i: (b, i, 0))
    full = lambda a: pl.BlockSpec(a.shape, lambda b, i: (0,) * a.ndim)
    ex = pl.BlockSpec((1, N_EXPERTS, tm), lambda b, i: (b, 0, i))
    return pl.pallas_call(
        _merge_kernel,
        grid=(bsz, s // tm),
        in_specs=[row(d),
                  pl.BlockSpec((1, 6, d), lambda b, i: (b, 0, 0)),
                  row(GDN_W), row(DIL_OUT_W),
                  pl.BlockSpec((1, tm, d), lambda b, i: (b, i, COL_GA // d)),
                  pl.BlockSpec((1, tm, d), lambda b, i: (b, i, COL_GB // d)),
                  full(w_up_gdn), full(w_up_dil), full(w_out), full(g_ffn), full(w_router_t),
                  full(router_bias)],
        out_specs=[row(d), row(d), ex, ex,
                   pl.BlockSpec((1, tm // MOE_TB, N_EXPERTS, LANES), lambda b, i: (b, i, 0, 0))],
        out_shape=[jax.ShapeDtypeStruct((bsz, s, d), F32),
                   jax.ShapeDtypeStruct((bsz, s, d), BF16),
                   jax.ShapeDtypeStruct((bsz, N_EXPERTS, s), F32),
                   jax.ShapeDtypeStruct((bsz, N_EXPERTS, s), F32),
                   jax.ShapeDtypeStruct((bsz, s // MOE_TB, N_EXPERTS, LANES), F32)],
        compiler_params=_params("parallel", "parallel"),
        name="merge_router",
    )(x, mod3, y_a, y_b, proj, proj, w_up_gdn, w_up_dil, w_out, g_ffn, w_router_t, router_bias)


MOE_TB = 256
MOE_CH = 16
MOE_RB = MOE_TB * TOP_K + N_EXPERTS * MOE_CH
MOE_CPB = MOE_RB // MOE_CH
MOE_TM = 512
MOE_CPT = MOE_TM // MOE_CH


def _moe_plan(cnt):
    nb, ne = cnt.shape
    nch = (cnt + MOE_CH - 1) // MOE_CH
    seg0 = jnp.cumsum(nch, axis=1) - nch
    nch_e = nch.T
    cum_e = jnp.cumsum(nch_e, axis=1)
    tot_e = cum_e[:, -1]
    tiles_e = (tot_e + MOE_CPT - 1) // MOE_CPT
    tile_end = jnp.cumsum(tiles_e)
    tile_off = tile_end - tiles_e
    ntiles = tile_end[-1]
    max_tiles = (nb * MOE_CPB) // MOE_CPT + ne
    i = jnp.arange(max_tiles, dtype=jnp.int32)
    te = jnp.minimum(jnp.searchsorted(tile_end, i, side="right"), ne - 1).astype(jnp.int32)
    te = jnp.where(i < ntiles, te, te[jnp.maximum(ntiles - 1, 0)])
    q = (i - tile_off[te])[:, None] * MOE_CPT + jnp.arange(MOE_CPT, dtype=jnp.int32)[None, :]
    valid = (q < tot_e[te][:, None]) & (i < ntiles)[:, None]
    cum_t = cum_e[te]
    blk = jnp.minimum(jnp.sum(cum_t[:, None, :] <= q[:, :, None], axis=-1), nb - 1)
    before = jnp.take_along_axis(cum_t, blk, axis=1) - nch_e[te[:, None], blk]
    src = blk * MOE_CPB + seg0[blk, te[:, None]] + (q - before)
    tbl = jnp.where(valid, src, -1).astype(jnp.int32).reshape(-1)
    return (seg0.astype(jnp.int32).reshape(-1), nch.astype(jnp.int32).reshape(-1), te, tbl,
            ntiles.astype(jnp.int32).reshape(1))


def _fill_onehot(p_scr, seg0_ref, nch_ref, blk, rank_ref, comb_ref, weighted):
    p_scr[...] = jnp.zeros_like(p_scr)
    sub = lax.broadcasted_iota(jnp.int32, (MOE_CH, MOE_TB), 0).astype(F32)

    def per_expert(e, carry):
        n = nch_ref[blk * N_EXPERTS + e]
        c0 = seg0_ref[blk * N_EXPERTS + e]
        rank = rank_ref[0, pl.ds(e, 1), :]
        wgt = comb_ref[0, pl.ds(e, 1), :]

        def per_chunk(g, carry2):
            hit = (rank - (g * MOE_CH).astype(F32) == sub) & (wgt > 0.0)
            val = jnp.where(hit, wgt if weighted else 1.0, 0.0)
            p_scr[pl.ds(pl.multiple_of((c0 + g) * MOE_CH, MOE_CH), MOE_CH), :] = val.astype(BF16)
            return carry2

        return lax.fori_loop(0, n, per_chunk, carry)

    lax.fori_loop(0, N_EXPERTS, per_expert, 0)


def _dispatch_kernel(seg0_ref, nch_ref, h_ref, rank_ref, comb_ref, x_ref, p_scr):
    blk = pl.program_id(0) * pl.num_programs(1) + pl.program_id(1)
    _fill_onehot(p_scr, seg0_ref, nch_ref, blk, rank_ref, comb_ref, weighted=False)
    h = h_ref[0]
    for r0 in range(0, MOE_RB, MOE_TM):
        rows = slice(r0, min(r0 + MOE_TM, MOE_RB))
        x_ref[0, rows, :] = jnp.dot(p_scr[rows, :], h, preferred_element_type=F32).astype(BF16)


def _dispatch(h2, rank_t, comb_t, seg0, nch):
    bsz, s, d = h2.shape
    nsb = s // MOE_TB
    return pl.pallas_call(
        _dispatch_kernel,
        grid_spec=pltpu.PrefetchScalarGridSpec(
            num_scalar_prefetch=2,
            grid=(bsz, nsb),
            in_specs=[pl.BlockSpec((1, MOE_TB, d), lambda b, i, *_: (b, i, 0)),
                      pl.BlockSpec((1, N_EXPERTS, MOE_TB), lambda b, i, *_: (b, 0, i)),
                      pl.BlockSpec((1, N_EXPERTS, MOE_TB), lambda b, i, *_: (b, 0, i))],
            out_specs=pl.BlockSpec((1, MOE_RB, d), lambda b, i, *_: (b * nsb + i, 0, 0)),
            scratch_shapes=[pltpu.VMEM((MOE_RB, MOE_TB), BF16)]),
        out_shape=jax.ShapeDtypeStruct((bsz * nsb, MOE_RB, d), BF16),
        compiler_params=_params("arbitrary", "arbitrary"),
        name="moe_dispatch",
    )(seg0, nch, h2, rank_t, comb_t)


def _ffn_kernel(te_ref, tbl_ref, nt_ref, x_hbm, wg_ref, wu_ref, wd_ref, y_hbm,
                xbuf, ybuf, wgb, wub, wdb, sem_in, sem_out):
    i = pl.program_id(0)
    nt = nt_ref[0]
    slot = i % 2

    def chunk_copy_in(t, s, j):
        return pltpu.make_async_copy(x_hbm.at[jnp.maximum(tbl_ref[t * MOE_CPT + j], 0)],
                                     xbuf.at[s, pl.ds(j * MOE_CH, MOE_CH)], sem_in.at[s])

    def chunk_copy_out(t, s, j):
        return pltpu.make_async_copy(ybuf.at[s, pl.ds(j * MOE_CH, MOE_CH)],
                                     y_hbm.at[jnp.maximum(tbl_ref[t * MOE_CPT + j], 0)], sem_out.at[s])

    def for_valid_chunks(t, fn):
        for j in range(MOE_CPT):
            @pl.when(tbl_ref[t * MOE_CPT + j] >= 0)
            def _():
                fn(j)

    @pl.when(i == 0)
    def _():
        xbuf[...] = jnp.zeros_like(xbuf)
        for_valid_chunks(0, lambda j: chunk_copy_in(0, 0, j).start())

    @pl.when(i + 1 < nt)
    def _():
        for_valid_chunks(i + 1, lambda j: chunk_copy_in(i + 1, 1 - slot, j).start())

    @pl.when(i < nt)
    def _():
        for_valid_chunks(i, lambda j: chunk_copy_in(i, slot, j).wait())

        @pl.when(i >= 2)
        def _():
            for_valid_chunks(i - 2, lambda j: chunk_copy_out(i - 2, slot, j).wait())

        changed = jnp.logical_or(i == 0, te_ref[i] != te_ref[jnp.maximum(i - 1, 0)])

        @pl.when(changed)
        def _():
            wgb[...] = wg_ref[0].astype(BF16)
            wub[...] = wu_ref[0].astype(BF16)
            wdb[...] = wd_ref[0].astype(BF16)

        x = xbuf[slot]
        a = jnp.dot(x, wgb[...], preferred_element_type=F32)
        u = jnp.dot(x, wub[...], preferred_element_type=F32)
        ybuf[slot] = jnp.dot((_silu(a) * u).astype(BF16), wdb[...], preferred_element_type=F32).astype(BF16)
        for_valid_chunks(i, lambda j: chunk_copy_out(i, slot, j).start())

        @pl.when(i == nt - 1)
        def _():
            @pl.when(i >= 1)
            def _():
                for_valid_chunks(i - 1, lambda j: chunk_copy_out(i - 1, 1 - slot, j).wait())
            for_valid_chunks(i, lambda j: chunk_copy_out(i, slot, j).wait())


def _ffn(x_rows, w_eg, w_eu, w_ed, te, tbl, ntiles):
    nblk, rb, d = x_rows.shape
    ne, _, de = w_eg.shape
    xc = x_rows.reshape(nblk * MOE_CPB, MOE_CH, d)
    max_tiles = te.shape[0]
    y = pl.pallas_call(
        _ffn_kernel,
        grid_spec=pltpu.PrefetchScalarGridSpec(
            num_scalar_prefetch=3,
            grid=(max_tiles,),
            in_specs=[pl.BlockSpec(memory_space=pl.ANY),
                      pl.BlockSpec((1, d, de), lambda i, te, tbl, nt: (te[i], 0, 0)),
                      pl.BlockSpec((1, d, de), lambda i, te, tbl, nt: (te[i], 0, 0)),
                      pl.BlockSpec((1, de, d), lambda i, te, tbl, nt: (te[i], 0, 0))],
            out_specs=pl.BlockSpec(memory_space=pl.ANY),
            scratch_shapes=[pltpu.VMEM((2, MOE_TM, d), BF16), pltpu.VMEM((2, MOE_TM, d), BF16),
                            pltpu.VMEM((d, de), BF16), pltpu.VMEM((d, de), BF16), pltpu.VMEM((de, d), BF16),
                            pltpu.SemaphoreType.DMA((2,)), pltpu.SemaphoreType.DMA((2,))]),
        out_shape=jax.ShapeDtypeStruct(xc.shape, BF16),
        input_output_aliases={3: 0},
        compiler_params=_params("arbitrary"),
        name="moe_ffn",
    )(te, tbl, ntiles, xc, w_eg, w_eu, w_ed)
    return y.reshape(nblk, rb, d)


def _combine_kernel(seg0_ref, nch_ref, y_ref, rank_ref, comb_ref, h_ref, x1_ref, mod_ref,
                    sg_ref, su_ref, sd_ref, o_ref, p_scr):
    blk = pl.program_id(0) * pl.num_programs(1) + pl.program_id(1)
    _fill_onehot(p_scr, seg0_ref, nch_ref, blk, rank_ref, comb_ref, weighted=True)
    routed = lax.dot_general(p_scr[...], y_ref[0], (((0,), (0,)), ((), ())), preferred_element_type=F32)
    h = h_ref[0]
    a = jnp.dot(h, sg_ref[...], preferred_element_type=F32)
    u = jnp.dot(h, su_ref[...], preferred_element_type=F32)
    shared = _mm(_silu(a) * u, sd_ref[...])
    o_ref[0] = x1_ref[0] + mod_ref[0, 5:6, :] * (routed + shared)


def _combine(y_rows, rank_t, comb_t, h2, x1, mod3, sg, su, sd, seg0, nch):
    bsz, s, d = x1.shape
    nsb = s // MOE_TB
    tok = lambda w: pl.BlockSpec((1, MOE_TB, w), lambda b, i, *_: (b, i, 0))
    ex = pl.BlockSpec((1, N_EXPERTS, MOE_TB), lambda b, i, *_: (b, 0, i))
    full = lambda a: pl.BlockSpec(a.shape, lambda b, i, *_: (0,) * a.ndim)
    return pl.pallas_call(
        _combine_kernel,
        grid_spec=pltpu.PrefetchScalarGridSpec(
            num_scalar_prefetch=2,
            grid=(bsz, nsb),
            in_specs=[pl.BlockSpec((1, MOE_RB, d), lambda b, i, *_: (b * nsb + i, 0, 0)),
                      ex, ex, tok(d), tok(d),
                      pl.BlockSpec((1, 6, d), lambda b, i, *_: (b, 0, 0)),
                      full(sg), full(su), full(sd)],
            out_specs=tok(d),
            scratch_shapes=[pltpu.VMEM((MOE_RB, MOE_TB), BF16)]),
        out_shape=jax.ShapeDtypeStruct((bsz, s, d), F32),
        compiler_params=_params("arbitrary", "arbitrary"),
        name="moe_combine",
    )(seg0, nch, y_rows, rank_t, comb_t, h2, x1, mod3, sg, su, sd)


def _moe(h2, comb_t, rank_t, cnt, x1, mod3, w_eg, w_eu, w_ed, sg, su, sd):
    seg0, nch, te, tbl, ntiles = _moe_plan(cnt)
    x_rows = _dispatch(h2, rank_t, comb_t, seg0, nch)
    y_rows = _ffn(x_rows, w_eg, w_eu, w_ed, te, tbl, ntiles)
    return _combine(y_rows, rank_t, comb_t, h2, x1, mod3, sg, su, sd, seg0, nch)


def _layer(x, cmod, g_mix, w_in, conv_w, a_log, dt_bias, norm_g, qn_g, kn_g, w_up_gdn, w_up_dil, w_out,
           g_ffn, w_router, router_bias, w_eg, w_eu, w_ed, w_sg, w_su, w_sd):
    bsz, s, d = x.shape
    mod3 = cmod.reshape(bsz, 6, d)
    o_ba = 4 * GDN_W
    o_dq = o_ba + 2 * GDN_HEADS
    o_ga = o_dq + 3 * DIL_W
    w_main = jnp.concatenate([w_in[:, :o_ba], w_in[:, o_ga:], w_in[:, o_dq:o_ga]], axis=1).astype(BF16)
    pad_hi = LANES - 2 * GDN_HEADS
    w_ba = jnp.pad(w_in[:, o_ba:o_dq], ((0, 0), (0, pad_hi))).astype(BF16)
    alog_vec = jnp.pad(a_log, (GDN_HEADS, pad_hi)).reshape(1, LANES)
    dtb_vec = jnp.pad(dt_bias, (GDN_HEADS, pad_hi)).reshape(1, LANES)
    proj, gates = _inproj(x, mod3, g_mix, w_main, w_ba, alog_vec, dtb_vec)

    conv_w3 = conv_w.reshape(GDN_CONV, 3, GDN_W).transpose(1, 0, 2)
    y_a = _gdn(proj, gates, conv_w3, norm_g)
    y_b = _dil(proj, qn_g, kn_g)

    x1, h2, comb_t, rank_t, cnt = _merge(x, mod3, y_a, y_b, proj, w_up_gdn.astype(BF16),
                                         w_up_dil.astype(BF16), w_out.astype(BF16), g_ffn.reshape(1, d),
                                         w_router.T, router_bias.reshape(N_EXPERTS, 1))
    cnt = cnt[..., 0].astype(jnp.int32).reshape(-1, N_EXPERTS)
    return _moe(h2, comb_t, rank_t, cnt, x1, mod3, w_eg, w_eu, w_ed,
                w_sg.astype(BF16), w_su.astype(BF16), w_sd.astype(BF16))


def kernel(x, c, w_ada, b_ada, g_mix, w_in, gdn_conv_w, gdn_a_log, gdn_dt_bias, gdn_norm_g, dil_q_norm_g, dil_k_norm_g, w_up_gdn, w_up_dil, w_out, g_ffn, w_router, router_bias, w_exp_gate, w_exp_up, w_exp_down, w_sh_gate, w_sh_up, w_sh_down):
    for l in range(w_ada.shape[0]):
        cmod = _ada(c, w_ada[l], b_ada[l])
        x = _layer(x, cmod, g_mix[l], w_in[l], gdn_conv_w[l], gdn_a_log[l], gdn_dt_bias[l], gdn_norm_g[l],
                   dil_q_norm_g[l], dil_k_norm_g[l], w_up_gdn[l], w_up_dil[l], w_out[l], g_ffn[l],
                   w_router[l], router_bias[l], w_exp_gate[l], w_exp_up[l], w_exp_down[l],
                   w_sh_gate[l], w_sh_up[l], w_sh_down[l])
    return x
```

```python
import functools

import jax
import jax.numpy as jnp
from jax import lax
from jax.experimental import pallas as pl
from jax.experimental.pallas import tpu as pltpu

F32 = jnp.float32
BF16 = jnp.bfloat16

D_MODEL = 1024
GDN_HEADS = 8
GDN_DK = 128
GDN_CONV = 4
DIL_PATTERN = ((128, 1), (512, 4), (2048, 16))
DIL_HEADS_PER_GROUP = 4
DIL_HD = 128
DIL_BLOCK = 128
DIL_BATCH = 4
N_EXPERTS = 64
TOP_K = 8
N_GROUPS = 8
TOPK_GROUPS = 4
D_EXPERT = 256
ROUTE_SCALE = 2.5
EPS = 1e-6

GDN_W = GDN_HEADS * GDN_DK
DIL_HEADS = len(DIL_PATTERN) * DIL_HEADS_PER_GROUP
DIL_W = DIL_HEADS * DIL_HD
DIL_OUT_W = DIL_HEADS_PER_GROUP * DIL_HD
COL_GQ, COL_GK, COL_GV, COL_GZ = 0, GDN_W, 2 * GDN_W, 3 * GDN_W
COL_GA = 4 * GDN_W
COL_GB = COL_GA + D_MODEL
COL_DQ = COL_GB + D_MODEL
COL_DK = COL_DQ + DIL_W
COL_DV = COL_DK + DIL_W
PROJ_W = COL_DV + DIL_W
LANES = 128

VMEM_LIMIT = 56 * 1024 * 1024


def _params(*sem):
    return pltpu.CompilerParams(dimension_semantics=sem, vmem_limit_bytes=VMEM_LIMIT)


def _sigmoid(x):
    return 1.0 / (1.0 + jnp.exp(-x))


def _silu(x):
    return x * _sigmoid(x)


def _softplus(x):
    return jnp.maximum(x, 0.0) + jnp.log(1.0 + jnp.exp(-jnp.abs(x)))


def _mm(a, b):
    return jnp.dot(a.astype(BF16), b.astype(BF16), preferred_element_type=F32)


def _mm_nt(a, b):
    return lax.dot_general(a.astype(BF16), b.astype(BF16), (((1,), (1,)), ((), ())),
                           preferred_element_type=F32)


def _mm_tn(a, b):
    return lax.dot_general(a.astype(BF16), b.astype(BF16), (((0,), (0,)), ((), ())),
                           preferred_element_type=F32)


def _mm_f32(a, b):
    return jnp.dot(a, b, preferred_element_type=F32, precision=lax.Precision.HIGHEST)


def _ada_kernel(c_ref, w_ref, b_ref, o_ref):
    cs = _silu(c_ref[...])
    o_ref[...] = _mm_f32(cs, w_ref[...]) + b_ref[...]


def _ada(c, w_ada, b_ada):
    bsz, d = c.shape
    n = w_ada.shape[1]
    tn = 1536
    return pl.pallas_call(
        _ada_kernel,
        grid=(n // tn,),
        in_specs=[pl.BlockSpec((bsz, d), lambda j: (0, 0)),
                  pl.BlockSpec((d, tn), lambda j: (0, j)),
                  pl.BlockSpec((1, tn), lambda j: (0, j))],
        out_specs=pl.BlockSpec((bsz, tn), lambda j: (0, j)),
        out_shape=jax.ShapeDtypeStruct((bsz, n), F32),
        compiler_params=_params("parallel"),
        name="ada",
    )(c, w_ada, b_ada.reshape(1, n))


def _inproj_kernel(x_ref, mod_ref, g_ref, w_ref, wba_ref, alog_ref, dtb_ref,
                   proj_ref, gates_ref, h_scr):
    j = pl.program_id(2)

    @pl.when(j == 0)
    def _():
        x = x_ref[0]
        sh1 = mod_ref[0, 0:1, :]
        sc1 = mod_ref[0, 1:2, :]
        ms = jnp.mean(x * x, axis=-1, keepdims=True)
        h = x * lax.rsqrt(ms + EPS) * g_ref[...] * (1.0 + sc1) + sh1
        hb = h.astype(BF16)
        h_scr[...] = hb
        ba = jnp.dot(hb, wba_ref[...], preferred_element_type=F32)
        lane = lax.broadcasted_iota(jnp.int32, ba.shape, 1)
        beta = _sigmoid(ba)
        g = -jnp.exp(alog_ref[...]) * _softplus(ba + dtb_ref[...])
        gates_ref[0] = jnp.where(lane < GDN_HEADS, beta, g)

    proj_ref[0] = jnp.dot(h_scr[...], w_ref[...], preferred_element_type=F32).astype(BF16)


def _inproj(x, mod3, g_mix, w_main, w_ba, alog_vec, dtb_vec):
    bsz, s, d = x.shape
    tm, tn = 1024, 768
    return pl.pallas_call(
        _inproj_kernel,
        grid=(bsz, s // tm, PROJ_W // tn),
        in_specs=[pl.BlockSpec((1, tm, d), lambda b, i, j: (b, i, 0)),
                  pl.BlockSpec((1, 6, d), lambda b, i, j: (b, 0, 0)),
                  pl.BlockSpec((1, d), lambda b, i, j: (0, 0)),
                  pl.BlockSpec((d, tn), lambda b, i, j: (0, j)),
                  pl.BlockSpec((d, LANES), lambda b, i, j: (0, 0)),
                  pl.BlockSpec((1, LANES), lambda b, i, j: (0, 0)),
                  pl.BlockSpec((1, LANES), lambda b, i, j: (0, 0))],
        out_specs=[pl.BlockSpec((1, tm, tn), lambda b, i, j: (b, i, j)),
                   pl.BlockSpec((1, tm, LANES), lambda b, i, j: (b, i, 0))],
        out_shape=[jax.ShapeDtypeStruct((bsz, s, PROJ_W), BF16),
                   jax.ShapeDtypeStruct((bsz, s, LANES), F32)],
        scratch_shapes=[pltpu.VMEM((tm, d), BF16)],
        compiler_params=_params("parallel", "parallel", "arbitrary"),
        name="inproj",
    )(x, mod3, g_mix.reshape(1, d), w_main, w_ba, alog_vec, dtb_vec)


GDN_C = 128
HALO = 8


def _cumsum_rows(g):
    row = lax.broadcasted_iota(jnp.int32, g.shape, 0)
    sft = 1
    while sft < g.shape[0]:
        g = g + jnp.where(row >= sft, pltpu.roll(g, sft, 0), 0.0)
        sft *= 2
    return g


def _unit_lower_inverses(mats):
    n = mats[0].shape[0]
    row = lax.broadcasted_iota(jnp.int32, (n, n), 0)
    col = lax.broadcasted_iota(jnp.int32, (n, n), 1)
    eye = (row == col).astype(F32)
    ts = None
    b = 1
    while b < n:
        off = ((row // (2 * b)) == (col // (2 * b))) & ((row // b) % 2 == 1) & ((col // b) % 2 == 0)
        if b == 1:
            ts = [eye - jnp.where(off, a, 0.0) for a in mats]
        else:
            tb = [t.astype(BF16) for t in ts]
            inner = [jnp.dot(jnp.where(off, a, 0.0).astype(BF16), t, preferred_element_type=F32)
                     for a, t in zip(mats, tb)]
            ts = [t - jnp.dot(t16, i.astype(BF16), preferred_element_type=F32)
                  for t, t16, i in zip(ts, tb, inner)]
        b *= 2
    return ts


def _gdn_kernel(q_ref, k_ref, v_ref, z_ref, gates_ref, cw_ref, ng_ref, o_ref,
                qbuf, kbuf, vbuf, state):
    sblk = pl.program_id(1)
    c = GDN_C
    nh = GDN_HEADS
    w = nh * GDN_DK

    @pl.when(sblk == 0)
    def _():
        state[...] = jnp.zeros_like(state)
        for buf in (qbuf, kbuf, vbuf):
            buf[0:HALO, :] = jnp.zeros((HALO, w), F32)

    @pl.when(sblk > 0)
    def _():
        for buf in (qbuf, kbuf, vbuf):
            buf[0:HALO, :] = buf[c:c + HALO, :]

    conv = []
    for idx, (ref, buf) in enumerate(((q_ref, qbuf), (k_ref, kbuf), (v_ref, vbuf))):
        buf[HALO:HALO + c, :] = ref[0].astype(F32)
        acc = jnp.zeros((c, w), F32)
        for j in range(GDN_CONV):
            start = HALO - (GDN_CONV - 1) + j
            acc = acc + buf[start:start + c, :] * cw_ref[idx, j:j + 1, :]
        conv.append(_silu(acc))
    qc, kc, vc = conv

    gates = gates_ref[0]
    gcum = _cumsum_rows(gates)
    row = lax.broadcasted_iota(jnp.int32, (c, c), 0)
    col = lax.broadcasted_iota(jnp.int32, (c, c), 1)
    eye = row == col
    incl = col <= row
    strict = col < row

    heads = range(nh)
    hsl = [slice(h * GDN_DK, (h + 1) * GDN_DK) for h in heads]
    q = [qc[:, s] for s in hsl]
    k = [kc[:, s] for s in hsl]
    v = [vc[:, s] for s in hsl]
    q = [x * lax.rsqrt(jnp.sum(x * x, axis=-1, keepdims=True) + EPS) * (GDN_DK ** -0.5) for x in q]
    k = [x * lax.rsqrt(jnp.sum(x * x, axis=-1, keepdims=True) + EPS) for x in k]
    beta = [jnp.broadcast_to(gates[:, h:h + 1], (c, GDN_DK)) for h in heads]
    gc = [jnp.broadcast_to(gcum[:, nh + h:nh + h + 1], (c, GDN_DK)) for h in heads]
    gc_row = [jnp.sum(jnp.where(eye, x, 0.0), axis=0, keepdims=True) for x in gc]
    decay = [jnp.exp(jnp.where(incl, x - y, -jnp.inf)) for x, y in zip(gc, gc_row)]
    egc = [jnp.exp(x) for x in gc]
    kb = [x * b for x, b in zip(k, beta)]
    scores = [_mm_nt(jnp.concatenate([a, b], axis=0), x) for a, b, x in zip(kb, q, k)]
    a_mat = [jnp.where(strict, s[:c] * d, 0.0) for s, d in zip(scores, decay)]
    qk = [s[c:] * d for s, d in zip(scores, decay)]
    t_inv = _unit_lower_inverses(a_mat)
    sol = [_mm(t, jnp.concatenate([x * b, y * e], axis=1))
           for t, x, b, y, e in zip(t_inv, v, beta, kb, egc)]

    st = [state[h] for h in heads]
    ws = [_mm(jnp.concatenate([s[:, GDN_DK:], x * e], axis=0), m)
          for s, x, e, m in zip(sol, q, egc, st)]
    v_new = [s[:, :GDN_DK] - x[:c] for s, x in zip(sol, ws)]
    o = [x[c:] + _mm(a, b) for x, a, b in zip(ws, qk, v_new)]
    gc_last = [x[c - 1:c, :] for x in gc]
    k_dec = [x * jnp.exp(l - g) for x, l, g in zip(k, gc_last, gc)]
    for h in heads:
        state[h] = st[h] * jnp.exp(gc_last[h]) + _mm_tn(k_dec[h], v_new[h])
        y = o[h] * lax.rsqrt(jnp.mean(o[h] * o[h], axis=-1, keepdims=True) + EPS) * ng_ref[...]
        o_ref[0, :, hsl[h]] = (y * _silu(z_ref[0, :, hsl[h]].astype(F32))).astype(o_ref.dtype)


def _gdn(proj, gates, conv_w3, norm_g):
    bsz, s, _ = proj.shape
    w = GDN_W

    def col_spec(col0):
        return pl.BlockSpec((1, GDN_C, w), lambda b, i: (b, i, col0 // w))

    return pl.pallas_call(
        _gdn_kernel,
        grid=(bsz, s // GDN_C),
        in_specs=[col_spec(COL_GQ), col_spec(COL_GK), col_spec(COL_GV), col_spec(COL_GZ),
                  pl.BlockSpec((1, GDN_C, LANES), lambda b, i: (b, i, 0)),
                  pl.BlockSpec((3, GDN_CONV, w), lambda b, i: (0, 0, 0)),
                  pl.BlockSpec((1, GDN_DK), lambda b, i: (0, 0))],
        out_specs=pl.BlockSpec((1, GDN_C, w), lambda b, i: (b, i, 0)),
        out_shape=jax.ShapeDtypeStruct((bsz, s, GDN_W), BF16),
        scratch_shapes=[pltpu.VMEM((GDN_C + HALO, w), F32),
                        pltpu.VMEM((GDN_C + HALO, w), F32),
                        pltpu.VMEM((GDN_C + HALO, w), F32),
                        pltpu.VMEM((GDN_HEADS, GDN_DK, GDN_DK), F32)],
        compiler_params=_params("parallel", "arbitrary"),
        name="gdn",
    )(proj, proj, proj, proj, gates, conv_w3, norm_g.reshape(1, GDN_DK))


def _dil_kernel(q0, q1, q2, k0, k1, k2, v0, v1, v2, qg_ref, kg_ref, o_ref, qs, ks, vs, os_, ls):
    s = qs.shape[1]
    blk = DIL_BLOCK
    ngrp = len(DIL_PATTERN)
    for gi, (q_ref, k_ref, v_ref) in enumerate(((q0, k0, v0), (q1, k1, v1), (q2, k2, v2))):
        q = q_ref[0].astype(F32)
        k = k_ref[0].astype(F32)
        q = q * lax.rsqrt(jnp.mean(q * q, axis=-1, keepdims=True) + EPS) * qg_ref[...]
        k = k * lax.rsqrt(jnp.mean(k * k, axis=-1, keepdims=True) + EPS) * kg_ref[...]
        qs[gi] = q * (DIL_HD ** -0.5)
        ks[gi] = k
        vs[gi] = v_ref[0].astype(F32)

    qi = lax.broadcasted_iota(jnp.int32, (blk, blk), 0)
    kj = lax.broadcasted_iota(jnp.int32, (blk, blk), 1)
    cur_ok = kj <= qi
    prev_ok = kj >= qi

    items = []
    for gi, (win, dil) in enumerate(DIL_PATTERN):
        assert win // dil == blk and (s // dil) % blk == 0
        for r in range(dil):
            for n in range((s // dil) // blk):
                items.append((gi, dil, r, n))

    def rows(dil, r, m):
        if dil == 1:
            return pl.ds(m * blk, blk)
        return pl.ds(m * blk * dil + r, blk, stride=dil)

    for b0 in range(0, len(items), DIL_BATCH):
        batch = items[b0:b0 + DIL_BATCH]
        qb = [qs[gi, rows(dil, r, n), :] for gi, dil, r, n in batch]
        kc = [ks[gi, rows(dil, r, n), :] for gi, dil, r, n in batch]
        kp = [ks[gi, rows(dil, r, n - 1), :] if n > 0 else None for gi, dil, r, n in batch]
        s_cur = [jnp.where(cur_ok, _mm_nt(q, k), -jnp.inf) for q, k in zip(qb, kc)]
        s_prev = [None if k is None else jnp.where(prev_ok, _mm_nt(q, k), -jnp.inf) for q, k in zip(qb, kp)]
        m = [jnp.max(a, axis=-1, keepdims=True) if b is None else
             jnp.maximum(jnp.max(a, axis=-1, keepdims=True), jnp.max(b, axis=-1, keepdims=True))
             for a, b in zip(s_cur, s_prev)]
        p_cur = [jnp.exp(a - mm) for a, mm in zip(s_cur, m)]
        p_prev = [None if b is None else jnp.exp(b - mm) for b, mm in zip(s_prev, m)]
        den = [jnp.sum(a, axis=-1, keepdims=True) if b is None else
               jnp.sum(a, axis=-1, keepdims=True) + jnp.sum(b, axis=-1, keepdims=True)
               for a, b in zip(p_cur, p_prev)]
        o = [_mm(p, vs[gi, rows(dil, r, n), :]) for p, (gi, dil, r, n) in zip(p_cur, batch)]
        o = [a if p is None else a + _mm(p, vs[gi, rows(dil, r, n - 1), :])
             for a, p, (gi, dil, r, n) in zip(o, p_prev, batch)]
        for (gi, dil, r, n), a, d, mm in zip(batch, o, den, m):
            os_[gi, rows(dil, r, n), :] = a / d
            ls[gi, rows(dil, r, n), :] = jnp.broadcast_to(mm + jnp.log(d), (blk, DIL_HD))

    lse = [ls[gi] for gi in range(ngrp)]
    mx = functools.reduce(jnp.maximum, lse)
    ex = [jnp.exp(l - mx) for l in lse]
    tot = functools.reduce(lambda a, b: a + b, ex)
    y = functools.reduce(lambda a, b: a + b, [ex[gi] / tot * os_[gi] for gi in range(ngrp)])
    o_ref[0] = y.astype(o_ref.dtype)


def _dil(proj, q_norm_g, k_norm_g):
    bsz, s, _ = proj.shape
    ngrp = len(DIL_PATTERN)

    def specs(col0):
        blk0 = col0 // DIL_HD
        return [pl.BlockSpec((1, s, DIL_HD), lambda b, h, g=g: (b, 0, blk0 + g * DIL_HEADS_PER_GROUP + h))
                for g in range(ngrp)]

    gain = pl.BlockSpec((1, DIL_HD), lambda b, h: (0, 0))
    return pl.pallas_call(
        _dil_kernel,
        grid=(bsz, DIL_HEADS_PER_GROUP),
        in_specs=specs(COL_DQ) + specs(COL_DK) + specs(COL_DV) + [gain, gain],
        out_specs=pl.BlockSpec((1, s, DIL_HD), lambda b, h: (b, 0, h)),
        out_shape=jax.ShapeDtypeStruct((bsz, s, DIL_OUT_W), BF16),
        scratch_shapes=[pltpu.VMEM((ngrp, s, DIL_HD), F32) for _ in range(5)],
        compiler_params=_params("parallel", "parallel"),
        name="dilattn",
    )(*([proj] * 9), q_norm_g.reshape(1, DIL_HD), k_norm_g.reshape(1, DIL_HD))


def _route(logits, bias):
    e, tm = logits.shape
    per = e // N_GROUPS
    scores = _sigmoid(logits)
    sel = scores + bias
    neg = -jnp.inf
    sub = lax.broadcasted_iota(jnp.int32, (per, tm), 0)
    gs_rows = []
    for g in range(N_GROUPS):
        blk = sel[g * per:(g + 1) * per, :]
        m1 = jnp.max(blk, axis=0, keepdims=True)
        i1 = jnp.min(jnp.where(blk == m1, sub, per), axis=0, keepdims=True)
        m2 = jnp.max(jnp.where(sub == i1, neg, blk), axis=0, keepdims=True)
        gs_rows.append(m1 + m2)
    gs = jnp.concatenate(gs_rows, axis=0)
    gi = lax.broadcasted_iota(jnp.int32, (N_GROUPS, tm), 0)
    gsel = jnp.zeros((N_GROUPS, tm), F32)
    for _ in range(TOPK_GROUPS):
        m = jnp.max(gs, axis=0, keepdims=True)
        idx = jnp.min(jnp.where(gs == m, gi, N_GROUPS), axis=0, keepdims=True)
        hit = gi == idx
        gsel = jnp.where(hit, 1.0, gsel)
        gs = jnp.where(hit, neg, gs)
    cand = jnp.concatenate(
        [jnp.where(gsel[g:g + 1, :] > 0.0, sel[g * per:(g + 1) * per, :], neg) for g in range(N_GROUPS)], axis=0)
    ei = lax.broadcasted_iota(jnp.int32, (e, tm), 0)
    chosen = jnp.zeros((e, tm), F32)
    for _ in range(TOP_K):
        m = jnp.max(cand, axis=0, keepdims=True)
        idx = jnp.min(jnp.where(cand == m, ei, e), axis=0, keepdims=True)
        hit = ei == idx
        chosen = jnp.where(hit, scores, chosen)
        cand = jnp.where(hit, neg, cand)
    return chosen / jnp.sum(chosen, axis=0, keepdims=True) * ROUTE_SCALE


def _merge_kernel(x_ref, mod_ref, ya_ref, yb_ref, ga_ref, gb_ref, wg_ref, wd_ref, wo_ref,
                  gf_ref, wr_ref, rb_ref, x1_ref, h2_ref, comb_ref, rank_ref, cnt_ref):
    gt1 = mod_ref[0, 2:3, :]
    sh2 = mod_ref[0, 3:4, :]
    sc2 = mod_ref[0, 4:5, :]
    ua = jnp.dot(ya_ref[0], wg_ref[...], preferred_element_type=F32)
    ub = jnp.dot(yb_ref[0], wd_ref[...], preferred_element_type=F32)
    merged = _sigmoid(ga_ref[0].astype(F32)) * ua + _sigmoid(gb_ref[0].astype(F32)) * ub
    x1 = x_ref[0] + gt1 * _mm(merged, wo_ref[...])
    x1_ref[0] = x1
    h2 = x1 * lax.rsqrt(jnp.mean(x1 * x1, axis=-1, keepdims=True) + EPS) * gf_ref[...] * (1.0 + sc2) + sh2
    h2_ref[0] = h2.astype(BF16)
    logits = lax.dot_general(wr_ref[...], h2, (((1,), (1,)), ((), ())),
                             preferred_element_type=F32, precision=lax.Precision.HIGHEST)
    comb = _route(logits, rb_ref[...])
    comb_ref[0] = comb
    tb = MOE_TB
    before = (lax.broadcasted_iota(jnp.int32, (tb, tb), 0) < lax.broadcasted_iota(jnp.int32, (tb, tb), 1))
    before = before.astype(BF16)
    for j in range(comb.shape[1] // tb):
        picked = (comb[:, j * tb:(j + 1) * tb] > 0.0).astype(F32)
        rank_ref[0, :, j * tb:(j + 1) * tb] = jnp.dot(picked.astype(BF16), before, preferred_element_type=F32)
        cnt_ref[0, j] = jnp.broadcast_to(jnp.sum(picked, axis=1, keepdims=True), (N_EXPERTS, LANES))


def _merge(x, mod3, y_a, y_b, proj, w_up_gdn, w_up_dil, w_out, g_ffn, w_router_t, router_bias):
    bsz, s, d = x.shape
    tm = 512
    row = lambda w: pl.BlockSpec((1, tm, w), lambda b, i: (b, i, 0))
    full = lambda a: pl.BlockSpec(a.shape, lambda b, i: (0,) * a.ndim)
    ex = pl.BlockSpec((1, N_EXPERTS, tm), lambda b, i: (b, 0, i))
    return pl.pallas_call(
        _merge_kernel,
        grid=(bsz, s // tm),
        in_specs=[row(d),
                  pl.BlockSpec((1, 6, d), lambda b, i: (b, 0, 0)),
                  row(GDN_W), row(DIL_OUT_W),
                  pl.BlockSpec((1, tm, d), lambda b, i: (b, i, COL_GA // d)),
                  pl.BlockSpec((1, tm, d), lambda b, i: (b, i, COL_GB // d)),
                  full(w_up_gdn), full(w_up_dil), full(w_out), full(g_ffn), full(w_router_t),
                  full(router_bias)],
        out_specs=[row(d), row(d), ex, ex,
                   pl.BlockSpec((1, tm // MOE_TB, N_EXPERTS, LANES), lambda b, i: (b, i, 0, 0))],
        out_shape=[jax.ShapeDtypeStruct((bsz, s, d), F32),
                   jax.ShapeDtypeStruct((bsz, s, d), BF16),
                   jax.ShapeDtypeStruct((bsz, N_EXPERTS, s), F32),
                   jax.ShapeDtypeStruct((bsz, N_EXPERTS, s), F32),
                   jax.ShapeDtypeStruct((bsz, s // MOE_TB, N_EXPERTS, LANES), F32)],
        compiler_params=_params("parallel", "parallel"),
        name="merge_router",
    )(x, mod3, y_a, y_b, proj, proj, w_up_gdn, w_up_dil, w_out, g_ffn, w_router_t, router_bias)


MOE_TB = 256
MOE_CH = 16
MOE_RB = MOE_TB * TOP_K + N_EXPERTS * MOE_CH
MOE_CPB = MOE_RB // MOE_CH
MOE_TM = 512
MOE_CPT = MOE_TM // MOE_CH


def _moe_plan(cnt):
    nb, ne = cnt.shape
    i32 = jnp.int32
    nch = (cnt + MOE_CH - 1) // MOE_CH
    seg0 = jnp.cumsum(nch, axis=1) - nch
    pad = jnp.zeros((nb, LANES - ne), i32)
    meta = jnp.stack([jnp.concatenate([seg0 * MOE_CH, pad + MOE_RB], axis=1),
                      jnp.concatenate([nch * MOE_CH, pad], axis=1)], axis=1).astype(F32)
    meta = jnp.pad(meta, ((0, 0), (0, 6), (0, 0)))
    nch_e = nch.T
    cum_e = jnp.cumsum(nch_e, axis=1)
    tot_e = cum_e[:, -1]
    tiles_e = (tot_e + MOE_CPT - 1) // MOE_CPT
    tile_end = jnp.cumsum(tiles_e)
    ntiles = tile_end[-1]
    max_tiles = (nb * MOE_CPB) // MOE_CPT + ne
    i = jnp.arange(max_tiles, dtype=i32)
    te = jnp.sum((tile_end[None, :] <= i[:, None]).astype(i32), axis=1)
    te = jnp.minimum(te, jnp.sum((tile_end <= ntiles - 1).astype(i32)))
    te = jnp.minimum(te, ne - 1)
    oh_te = (te[:, None] == jnp.arange(ne, dtype=i32)[None, :]).astype(i32)
    pick = lambda tab: jnp.sum(oh_te[:, :, None] * tab[None, :, :], axis=1)
    tile_off_t = jnp.sum(oh_te * (tile_end - tiles_e)[None, :], axis=1)
    tot_t = jnp.sum(oh_te * tot_e[None, :], axis=1)
    cum_t, nch_t, seg0_t = pick(cum_e), pick(nch_e), pick(seg0.T)
    q = (i - tile_off_t)[:, None] * MOE_CPT + jnp.arange(MOE_CPT, dtype=i32)[None, :]
    valid = (q < tot_t[:, None]) & (i < ntiles)[:, None]
    blk = jnp.minimum(jnp.sum((cum_t[:, None, :] <= q[:, :, None]).astype(i32), axis=-1), nb - 1)
    oh_b = (blk[:, :, None] == jnp.arange(nb, dtype=i32)[None, None, :]).astype(i32)
    before = jnp.sum(oh_b * (cum_t - nch_t)[:, None, :], axis=-1)
    src = blk * MOE_CPB + jnp.sum(oh_b * seg0_t[:, None, :], axis=-1) + (q - before)
    spare = nb * MOE_CPB + (i % 2)[:, None] * MOE_CPT + jnp.arange(MOE_CPT, dtype=i32)[None, :]
    tbl_in = jnp.where(valid, src, nb * MOE_CPB + 2 * MOE_CPT).astype(i32).reshape(-1)
    tbl_out = jnp.where(valid, src, spare).astype(i32).reshape(-1)
    return meta, te.astype(i32), tbl_in, tbl_out, ntiles.astype(i32).reshape(1)


def _onehot_rows(meta_ref, rhs, r0, nrows, weighted):
    start = meta_ref[0, 0:1, :]
    plen = meta_ref[0, 1:2, :]
    r = (lax.broadcasted_iota(jnp.int32, (nrows, LANES), 0) + r0).astype(F32)
    owner = jnp.where((r >= start) & (r < start + plen), 1.0, 0.0)
    pos = r[:, :1] - jnp.sum(owner * start, axis=1, keepdims=True)
    got = jnp.dot(owner[:, :N_EXPERTS].astype(BF16), rhs, preferred_element_type=F32)
    tb = MOE_TB
    hit = (got[:, :tb] == pos) & (got[:, tb:2 * tb] > 0.5)
    return jnp.where(hit, got[:, 2 * tb:] if weighted else 1.0, 0.0).astype(BF16)


def _route_rhs(rank_ref, comb_ref):
    comb = comb_ref[0]
    picked = jnp.where(comb > 0.0, 1.0, 0.0)
    return jnp.concatenate([rank_ref[0], picked, comb], axis=1).astype(BF16)


def _dispatch_kernel(meta_ref, h_ref, rank_ref, comb_ref, x_ref):
    @pl.when(pl.program_id(0) < pl.num_programs(0) - 1)
    def _():
        rhs = _route_rhs(rank_ref, comb_ref)
        h = h_ref[0]
        for r0 in range(0, MOE_RB, MOE_TM):
            n = min(MOE_TM, MOE_RB - r0)
            p = _onehot_rows(meta_ref, rhs, r0, n, weighted=False)
            x_ref[0, r0:r0 + n, :] = jnp.dot(p, h, preferred_element_type=F32).astype(BF16)

    @pl.when(pl.program_id(0) == pl.num_programs(0) - 1)
    def _():
        x_ref[...] = jnp.zeros_like(x_ref)


def _dispatch(h2, rank_t, comb_t, meta):
    bsz, s, d = h2.shape
    nsb = s // MOE_TB
    nblk = bsz * nsb
    tok = lambda i: (jnp.minimum(i, nblk - 1) // nsb, jnp.minimum(i, nblk - 1) % nsb)
    ex = pl.BlockSpec((1, N_EXPERTS, MOE_TB), lambda i: (tok(i)[0], 0, tok(i)[1]))
    return pl.pallas_call(
        _dispatch_kernel,
        grid=(nblk + 1,),
        in_specs=[pl.BlockSpec((1, 8, LANES), lambda i: (jnp.minimum(i, nblk - 1), 0, 0)),
                  pl.BlockSpec((1, MOE_TB, d), lambda i: (tok(i)[0], tok(i)[1], 0)),
                  ex, ex],
        out_specs=pl.BlockSpec((1, MOE_RB, d), lambda i: (i, 0, 0)),
        out_shape=jax.ShapeDtypeStruct((nblk + 1, MOE_RB, d), BF16),
        compiler_params=_params("parallel"),
        name="moe_dispatch",
    )(meta, h2, rank_t, comb_t)


def _ffn_kernel(te_ref, tin_ref, tout_ref, nt_ref, x_hbm, wg_ref, wu_ref, wd_ref, y_hbm,
                xbuf, ybuf, wgb, wub, wdb, sem_in, sem_out):
    i = pl.program_id(0)
    nt = nt_ref[0]
    slot = i % 2

    def copy_in(t, s, j):
        return pltpu.make_async_copy(x_hbm.at[tin_ref[t * MOE_CPT + j]],
                                     xbuf.at[s, pl.ds(j * MOE_CH, MOE_CH)], sem_in.at[s])

    def copy_out(t, s, j):
        return pltpu.make_async_copy(ybuf.at[s, pl.ds(j * MOE_CH, MOE_CH)],
                                     y_hbm.at[tout_ref[t * MOE_CPT + j]], sem_out.at[s])

    @pl.when(i == 0)
    def _():
        for j in range(MOE_CPT):
            copy_in(0, 0, j).start()

    @pl.when(i + 1 < nt)
    def _():
        for j in range(MOE_CPT):
            copy_in(i + 1, 1 - slot, j).start()

    @pl.when(i < nt)
    def _():
        for j in range(MOE_CPT):
            copy_in(i, slot, j).wait()

        @pl.when(i >= 2)
        def _():
            for j in range(MOE_CPT):
                copy_out(i - 2, slot, j).wait()

        @pl.when(jnp.logical_or(i == 0, te_ref[i] != te_ref[jnp.maximum(i - 1, 0)]))
        def _():
            wgb[...] = wg_ref[0].astype(BF16)
            wub[...] = wu_ref[0].astype(BF16)
            wdb[...] = wd_ref[0].astype(BF16)

        x = xbuf[slot]
        a = jnp.dot(x, wgb[...], preferred_element_type=F32)
        u = jnp.dot(x, wub[...], preferred_element_type=F32)
        ybuf[slot] = jnp.dot((_silu(a) * u).astype(BF16), wdb[...], preferred_element_type=F32).astype(BF16)
        for j in range(MOE_CPT):
            copy_out(i, slot, j).start()

        @pl.when(i == nt - 1)
        def _():
            @pl.when(i >= 1)
            def _():
                for j in range(MOE_CPT):
                    copy_out(i - 1, 1 - slot, j).wait()
            for j in range(MOE_CPT):
                copy_out(i, slot, j).wait()


def _ffn(x_rows, w_eg, w_eu, w_ed, te, tbl_in, tbl_out, ntiles):
    nblk, rb, d = x_rows.shape
    ne, _, de = w_eg.shape
    xc = x_rows.reshape(nblk * MOE_CPB, MOE_CH, d)
    max_tiles = te.shape[0]
    y = pl.pallas_call(
        _ffn_kernel,
        grid_spec=pltpu.PrefetchScalarGridSpec(
            num_scalar_prefetch=4,
            grid=(max_tiles,),
            in_specs=[pl.BlockSpec(memory_space=pl.ANY),
                      pl.BlockSpec((1, d, de), lambda i, te, *_: (te[i], 0, 0)),
                      pl.BlockSpec((1, d, de), lambda i, te, *_: (te[i], 0, 0)),
                      pl.BlockSpec((1, de, d), lambda i, te, *_: (te[i], 0, 0))],
            out_specs=pl.BlockSpec(memory_space=pl.ANY),
            scratch_shapes=[pltpu.VMEM((2, MOE_TM, d), BF16), pltpu.VMEM((2, MOE_TM, d), BF16),
                            pltpu.VMEM((d, de), BF16), pltpu.VMEM((d, de), BF16), pltpu.VMEM((de, d), BF16),
                            pltpu.SemaphoreType.DMA((2,)), pltpu.SemaphoreType.DMA((2,))]),
        out_shape=jax.ShapeDtypeStruct(xc.shape, BF16),
        input_output_aliases={4: 0},
        compiler_params=_params("arbitrary"),
        name="moe_ffn",
    )(te, tbl_in, tbl_out, ntiles, xc, w_eg, w_eu, w_ed)
    return y.reshape(nblk, rb, d)


def _combine_kernel(meta_ref, y_ref, rank_ref, comb_ref, h_ref, x1_ref, mod_ref,
                    sg_ref, su_ref, sd_ref, o_ref):
    rhs = _route_rhs(rank_ref, comb_ref)
    h = h_ref[0]
    a = jnp.dot(h, sg_ref[...], preferred_element_type=F32)
    u = jnp.dot(h, su_ref[...], preferred_element_type=F32)
    acc = _mm(_silu(a) * u, sd_ref[...])
    for r0 in range(0, MOE_RB, MOE_TM):
        n = min(MOE_TM, MOE_RB - r0)
        p = _onehot_rows(meta_ref, rhs, r0, n, weighted=True)
        acc = acc + lax.dot_general(p, y_ref[0, r0:r0 + n, :], (((0,), (0,)), ((), ())),
                                    preferred_element_type=F32)
    o_ref[0] = x1_ref[0] + mod_ref[0, 5:6, :] * acc


def _combine(y_rows, rank_t, comb_t, h2, x1, mod3, sg, su, sd, meta):
    bsz, s, d = x1.shape
    nsb = s // MOE_TB
    tok = lambda w: pl.BlockSpec((1, MOE_TB, w), lambda b, i: (b, i, 0))
    ex = pl.BlockSpec((1, N_EXPERTS, MOE_TB), lambda b, i: (b, 0, i))
    full = lambda a: pl.BlockSpec(a.shape, lambda b, i: (0,) * a.ndim)
    return pl.pallas_call(
        _combine_kernel,
        grid=(bsz, nsb),
        in_specs=[pl.BlockSpec((1, 8, LANES), lambda b, i: (b * nsb + i, 0, 0)),
                  pl.BlockSpec((1, MOE_RB, d), lambda b, i: (b * nsb + i, 0, 0)),
                  ex, ex, tok(d), tok(d),
                  pl.BlockSpec((1, 6, d), lambda b, i: (b, 0, 0)),
                  full(sg), full(su), full(sd)],
        out_specs=tok(d),
        out_shape=jax.ShapeDtypeStruct((bsz, s, d), F32),
        compiler_params=_params("parallel", "parallel"),
        name="moe_combine",
    )(meta, y_rows, rank_t, comb_t, h2, x1, mod3, sg, su, sd)


def _moe(h2, comb_t, rank_t, cnt, x1, mod3, w_eg, w_eu, w_ed, sg, su, sd):
    meta, te, tbl_in, tbl_out, ntiles = _moe_plan(cnt)
    x_rows = _dispatch(h2, rank_t, comb_t, meta)
    y_rows = _ffn(x_rows, w_eg, w_eu, w_ed, te, tbl_in, tbl_out, ntiles)
    return _combine(y_rows, rank_t, comb_t, h2, x1, mod3, sg, su, sd, meta)


def _layer(x, cmod, g_mix, w_in, conv_w, a_log, dt_bias, norm_g, qn_g, kn_g, w_up_gdn, w_up_dil, w_out,
           g_ffn, w_router, router_bias, w_eg, w_eu, w_ed, w_sg, w_su, w_sd):
    bsz, s, d = x.shape
    mod3 = cmod.reshape(bsz, 6, d)
    o_ba = 4 * GDN_W
    o_dq = o_ba + 2 * GDN_HEADS
    o_ga = o_dq + 3 * DIL_W
    w_main = jnp.concatenate([w_in[:, :o_ba], w_in[:, o_ga:], w_in[:, o_dq:o_ga]], axis=1).astype(BF16)
    pad_hi = LANES - 2 * GDN_HEADS
    w_ba = jnp.pad(w_in[:, o_ba:o_dq], ((0, 0), (0, pad_hi))).astype(BF16)
    alog_vec = jnp.pad(a_log, (GDN_HEADS, pad_hi)).reshape(1, LANES)
    dtb_vec = jnp.pad(dt_bias, (GDN_HEADS, pad_hi)).reshape(1, LANES)
    proj, gates = _inproj(x, mod3, g_mix, w_main, w_ba, alog_vec, dtb_vec)

    conv_w3 = conv_w.reshape(GDN_CONV, 3, GDN_W).transpose(1, 0, 2)
    y_a = _gdn(proj, gates, conv_w3, norm_g)
    y_b = _dil(proj, qn_g, kn_g)

    x1, h2, comb_t, rank_t, cnt = _merge(x, mod3, y_a, y_b, proj, w_up_gdn.astype(BF16),
                                         w_up_dil.astype(BF16), w_out.astype(BF16), g_ffn.reshape(1, d),
                                         w_router.T, router_bias.reshape(N_EXPERTS, 1))
    cnt = cnt[..., 0].astype(jnp.int32).reshape(-1, N_EXPERTS)
    return _moe(h2, comb_t, rank_t, cnt, x1, mod3, w_eg, w_eu, w_ed,
                w_sg.astype(BF16), w_su.astype(BF16), w_sd.astype(BF16))


def kernel(x, c, w_ada, b_ada, g_mix, w_in, gdn_conv_w, gdn_a_log, gdn_dt_bias, gdn_norm_g, dil_q_norm_g, dil_k_norm_g, w_up_gdn, w_up_dil, w_out, g_ffn, w_router, router_bias, w_exp_gate, w_exp_up, w_exp_down, w_sh_gate, w_sh_up, w_sh_down):
    for l in range(w_ada.shape[0]):
        cmod = _ada(c, w_ada[l], b_ada[l])
        x = _layer(x, cmod, g_mix[l], w_in[l], gdn_conv_w[l], gdn_a_log[l], gdn_dt_bias[l], gdn_norm_g[l],
                   dil_q_norm_g[l], dil_k_norm_g[l], w_up_gdn[l], w_up_dil[l], w_out[l], g_ffn[l],
                   w_router[l], router_bias[l], w_exp_gate[l], w_exp_up[l], w_exp_down[l],
                   w_sh_gate[l], w_sh_up[l], w_sh_down[l])
    return x
```

```python
import functools

import jax
import jax.numpy as jnp
from jax import lax
from jax.experimental import pallas as pl
from jax.experimental.pallas import tpu as pltpu

F32 = jnp.float32
BF16 = jnp.bfloat16

D_MODEL = 1024
GDN_HEADS = 8
GDN_DK = 128
GDN_CONV = 4
DIL_PATTERN = ((128, 1), (512, 4), (2048, 16))
DIL_HEADS_PER_GROUP = 4
DIL_HD = 128
DIL_BLOCK = 128
DIL_BATCH = 4
N_EXPERTS = 64
TOP_K = 8
N_GROUPS = 8
TOPK_GROUPS = 4
D_EXPERT = 256
ROUTE_SCALE = 2.5
EPS = 1e-6

GDN_W = GDN_HEADS * GDN_DK
DIL_HEADS = len(DIL_PATTERN) * DIL_HEADS_PER_GROUP
DIL_W = DIL_HEADS * DIL_HD
DIL_OUT_W = DIL_HEADS_PER_GROUP * DIL_HD
COL_GQ, COL_GK, COL_GV, COL_GZ = 0, GDN_W, 2 * GDN_W, 3 * GDN_W
COL_GA = 4 * GDN_W
COL_GB = COL_GA + D_MODEL
COL_DQ = COL_GB + D_MODEL
COL_DK = COL_DQ + DIL_W
COL_DV = COL_DK + DIL_W
PROJ_W = COL_DV + DIL_W
LANES = 128

VMEM_LIMIT = 56 * 1024 * 1024


def _params(*sem):
    return pltpu.CompilerParams(dimension_semantics=sem, vmem_limit_bytes=VMEM_LIMIT)


def _sigmoid(x):
    return 1.0 / (1.0 + jnp.exp(-x))


def _silu(x):
    return x * _sigmoid(x)


def _softplus(x):
    return jnp.maximum(x, 0.0) + jnp.log(1.0 + jnp.exp(-jnp.abs(x)))


def _mm(a, b):
    return jnp.dot(a.astype(BF16), b.astype(BF16), preferred_element_type=F32)


def _mm_nt(a, b):
    return lax.dot_general(a.astype(BF16), b.astype(BF16), (((1,), (1,)), ((), ())),
                           preferred_element_type=F32)


def _mm_tn(a, b):
    return lax.dot_general(a.astype(BF16), b.astype(BF16), (((0,), (0,)), ((), ())),
                           preferred_element_type=F32)


def _mm_f32(a, b):
    return jnp.dot(a, b, preferred_element_type=F32, precision=lax.Precision.HIGHEST)


def _ada_kernel(c_ref, w_ref, b_ref, o_ref):
    cs = _silu(c_ref[...])
    o_ref[...] = _mm_f32(cs, w_ref[...]) + b_ref[...]


def _ada(c, w_ada, b_ada):
    bsz, d = c.shape
    n = w_ada.shape[1]
    tn = 1536
    return pl.pallas_call(
        _ada_kernel,
        grid=(n // tn,),
        in_specs=[pl.BlockSpec((bsz, d), lambda j: (0, 0)),
                  pl.BlockSpec((d, tn), lambda j: (0, j)),
                  pl.BlockSpec((1, tn), lambda j: (0, j))],
        out_specs=pl.BlockSpec((bsz, tn), lambda j: (0, j)),
        out_shape=jax.ShapeDtypeStruct((bsz, n), F32),
        compiler_params=_params("parallel"),
        name="ada",
    )(c, w_ada, b_ada.reshape(1, n))


def _inproj_kernel(x_ref, mod_ref, g_ref, w_ref, wba_ref, alog_ref, dtb_ref,
                   proj_ref, gates_ref, h_scr):
    j = pl.program_id(2)

    @pl.when(j == 0)
    def _():
        x = x_ref[0]
        sh1 = mod_ref[0, 0:1, :]
        sc1 = mod_ref[0, 1:2, :]
        ms = jnp.mean(x * x, axis=-1, keepdims=True)
        h = x * lax.rsqrt(ms + EPS) * g_ref[...] * (1.0 + sc1) + sh1
        hb = h.astype(BF16)
        h_scr[...] = hb
        ba = jnp.dot(hb, wba_ref[...], preferred_element_type=F32)
        lane = lax.broadcasted_iota(jnp.int32, ba.shape, 1)
        beta = _sigmoid(ba)
        g = -jnp.exp(alog_ref[...]) * _softplus(ba + dtb_ref[...])
        gates_ref[0] = jnp.where(lane < GDN_HEADS, beta, g)

    proj_ref[0] = jnp.dot(h_scr[...], w_ref[...], preferred_element_type=F32).astype(BF16)


def _inproj(x, mod3, g_mix, w_main, w_ba, alog_vec, dtb_vec):
    bsz, s, d = x.shape
    tm, tn = 1024, 1536
    return pl.pallas_call(
        _inproj_kernel,
        grid=(bsz, s // tm, PROJ_W // tn),
        in_specs=[pl.BlockSpec((1, tm, d), lambda b, i, j: (b, i, 0)),
                  pl.BlockSpec((1, 6, d), lambda b, i, j: (b, 0, 0)),
                  pl.BlockSpec((1, d), lambda b, i, j: (0, 0)),
                  pl.BlockSpec((d, tn), lambda b, i, j: (0, j)),
                  pl.BlockSpec((d, LANES), lambda b, i, j: (0, 0)),
                  pl.BlockSpec((1, LANES), lambda b, i, j: (0, 0)),
                  pl.BlockSpec((1, LANES), lambda b, i, j: (0, 0))],
        out_specs=[pl.BlockSpec((1, tm, tn), lambda b, i, j: (b, i, j)),
                   pl.BlockSpec((1, tm, LANES), lambda b, i, j: (b, i, 0))],
        out_shape=[jax.ShapeDtypeStruct((bsz, s, PROJ_W), BF16),
                   jax.ShapeDtypeStruct((bsz, s, LANES), F32)],
        scratch_shapes=[pltpu.VMEM((tm, d), BF16)],
        compiler_params=_params("parallel", "parallel", "arbitrary"),
        name="inproj",
    )(x, mod3, g_mix.reshape(1, d), w_main, w_ba, alog_vec, dtb_vec)


GDN_C = 128
HALO = 8


def _cumsum_rows(g):
    row = lax.broadcasted_iota(jnp.int32, g.shape, 0)
    sft = 1
    while sft < g.shape[0]:
        g = g + jnp.where(row >= sft, pltpu.roll(g, sft, 0), 0.0)
        sft *= 2
    return g


def _unit_lower_inverses(mats):
    n = mats[0].shape[0]
    row = lax.broadcasted_iota(jnp.int32, (n, n), 0)
    col = lax.broadcasted_iota(jnp.int32, (n, n), 1)
    eye = (row == col).astype(F32)
    ts = None
    b = 1
    while b < n:
        off = ((row // (2 * b)) == (col // (2 * b))) & ((row // b) % 2 == 1) & ((col // b) % 2 == 0)
        if b == 1:
            ts = [eye - jnp.where(off, a, 0.0) for a in mats]
        else:
            tb = [t.astype(BF16) for t in ts]
            inner = [jnp.dot(jnp.where(off, a, 0.0).astype(BF16), t, preferred_element_type=F32)
                     for a, t in zip(mats, tb)]
            ts = [t - jnp.dot(t16, i.astype(BF16), preferred_element_type=F32)
                  for t, t16, i in zip(ts, tb, inner)]
        b *= 2
    return ts


def _gdn_kernel(q_ref, k_ref, v_ref, z_ref, gates_ref, cw_ref, ng_ref, o_ref,
                qbuf, kbuf, vbuf, state):
    sblk = pl.program_id(1)
    c = GDN_C
    nh = GDN_HEADS
    w = nh * GDN_DK

    @pl.when(sblk == 0)
    def _():
        state[...] = jnp.zeros_like(state)
        for buf in (qbuf, kbuf, vbuf):
            buf[0:HALO, :] = jnp.zeros((HALO, w), F32)

    @pl.when(sblk > 0)
    def _():
        for buf in (qbuf, kbuf, vbuf):
            buf[0:HALO, :] = buf[c:c + HALO, :]

    conv = []
    for idx, (ref, buf) in enumerate(((q_ref, qbuf), (k_ref, kbuf), (v_ref, vbuf))):
        buf[HALO:HALO + c, :] = ref[0].astype(F32)
        acc = jnp.zeros((c, w), F32)
        for j in range(GDN_CONV):
            start = HALO - (GDN_CONV - 1) + j
            acc = acc + buf[start:start + c, :] * cw_ref[idx, j:j + 1, :]
        conv.append(_silu(acc))
    qc, kc, vc = conv

    gates = gates_ref[0]
    gcum = _cumsum_rows(gates)
    row = lax.broadcasted_iota(jnp.int32, (c, c), 0)
    col = lax.broadcasted_iota(jnp.int32, (c, c), 1)
    eye = row == col
    incl = col <= row
    strict = col < row

    heads = range(nh)
    hsl = [slice(h * GDN_DK, (h + 1) * GDN_DK) for h in heads]
    q = [qc[:, s] for s in hsl]
    k = [kc[:, s] for s in hsl]
    v = [vc[:, s] for s in hsl]
    q = [x * lax.rsqrt(jnp.sum(x * x, axis=-1, keepdims=True) + EPS) * (GDN_DK ** -0.5) for x in q]
    k = [x * lax.rsqrt(jnp.sum(x * x, axis=-1, keepdims=True) + EPS) for x in k]
    beta = [jnp.broadcast_to(gates[:, h:h + 1], (c, GDN_DK)) for h in heads]
    gc = [jnp.broadcast_to(gcum[:, nh + h:nh + h + 1], (c, GDN_DK)) for h in heads]
    gc_row = [jnp.sum(jnp.where(eye, x, 0.0), axis=0, keepdims=True) for x in gc]
    decay = [jnp.exp(jnp.where(incl, x - y, -jnp.inf)) for x, y in zip(gc, gc_row)]
    egc = [jnp.exp(x) for x in gc]
    kb = [x * b for x, b in zip(k, beta)]
    scores = [_mm_nt(jnp.concatenate([a, b], axis=0), x) for a, b, x in zip(kb, q, k)]
    a_mat = [jnp.where(strict, s[:c] * d, 0.0) for s, d in zip(scores, decay)]
    qk = [s[c:] * d for s, d in zip(scores, decay)]
    t_inv = _unit_lower_inverses(a_mat)
    sol = [_mm(t, jnp.concatenate([x * b, y * e], axis=1))
           for t, x, b, y, e in zip(t_inv, v, beta, kb, egc)]

    st = [state[h] for h in heads]
    ws = [_mm(jnp.concatenate([s[:, GDN_DK:], x * e], axis=0), m)
          for s, x, e, m in zip(sol, q, egc, st)]
    v_new = [s[:, :GDN_DK] - x[:c] for s, x in zip(sol, ws)]
    o = [x[c:] + _mm(a, b) for x, a, b in zip(ws, qk, v_new)]
    gc_last = [x[c - 1:c, :] for x in gc]
    k_dec = [x * jnp.exp(l - g) for x, l, g in zip(k, gc_last, gc)]
    for h in heads:
        state[h] = st[h] * jnp.exp(gc_last[h]) + _mm_tn(k_dec[h], v_new[h])
        y = o[h] * lax.rsqrt(jnp.mean(o[h] * o[h], axis=-1, keepdims=True) + EPS) * ng_ref[...]
        o_ref[0, :, hsl[h]] = (y * _silu(z_ref[0, :, hsl[h]].astype(F32))).astype(o_ref.dtype)


def _gdn(proj, gates, conv_w3, norm_g):
    bsz, s, _ = proj.shape
    w = GDN_W

    def col_spec(col0):
        return pl.BlockSpec((1, GDN_C, w), lambda b, i: (b, i, col0 // w))

    return pl.pallas_call(
        _gdn_kernel,
        grid=(bsz, s // GDN_C),
        in_specs=[col_spec(COL_GQ), col_spec(COL_GK), col_spec(COL_GV), col_spec(COL_GZ),
                  pl.BlockSpec((1, GDN_C, LANES), lambda b, i: (b, i, 0)),
                  pl.BlockSpec((3, GDN_CONV, w), lambda b, i: (0, 0, 0)),
                  pl.BlockSpec((1, GDN_DK), lambda b, i: (0, 0))],
        out_specs=pl.BlockSpec((1, GDN_C, w), lambda b, i: (b, i, 0)),
        out_shape=jax.ShapeDtypeStruct((bsz, s, GDN_W), BF16),
        scratch_shapes=[pltpu.VMEM((GDN_C + HALO, w), F32),
                        pltpu.VMEM((GDN_C + HALO, w), F32),
                        pltpu.VMEM((GDN_C + HALO, w), F32),
                        pltpu.VMEM((GDN_HEADS, GDN_DK, GDN_DK), F32)],
        compiler_params=_params("parallel", "arbitrary"),
        name="gdn",
    )(proj, proj, proj, proj, gates, conv_w3, norm_g.reshape(1, GDN_DK))


def _dil_kernel(q0, q1, q2, k0, k1, k2, v0, v1, v2, qg_ref, kg_ref, o_ref, qs, ks, vs, os_, ls):
    s = qs.shape[1]
    blk = DIL_BLOCK
    ngrp = len(DIL_PATTERN)
    for gi, (q_ref, k_ref, v_ref) in enumerate(((q0, k0, v0), (q1, k1, v1), (q2, k2, v2))):
        q = q_ref[0].astype(F32)
        k = k_ref[0].astype(F32)
        q = q * lax.rsqrt(jnp.mean(q * q, axis=-1, keepdims=True) + EPS) * qg_ref[...]
        k = k * lax.rsqrt(jnp.mean(k * k, axis=-1, keepdims=True) + EPS) * kg_ref[...]
        qs[gi] = q * (DIL_HD ** -0.5)
        ks[gi] = k
        vs[gi] = v_ref[0].astype(F32)

    qi = lax.broadcasted_iota(jnp.int32, (blk, blk), 0)
    kj = lax.broadcasted_iota(jnp.int32, (blk, blk), 1)
    cur_ok = kj <= qi
    prev_ok = kj >= qi

    items = []
    for gi, (win, dil) in enumerate(DIL_PATTERN):
        assert win // dil == blk and (s // dil) % blk == 0
        for r in range(dil):
            for n in range((s // dil) // blk):
                items.append((gi, dil, r, n))

    def rows(dil, r, m):
        if dil == 1:
            return pl.ds(m * blk, blk)
        return pl.ds(m * blk * dil + r, blk, stride=dil)

    for b0 in range(0, len(items), DIL_BATCH):
        batch = items[b0:b0 + DIL_BATCH]
        qb = [qs[gi, rows(dil, r, n), :] for gi, dil, r, n in batch]
        kc = [ks[gi, rows(dil, r, n), :] for gi, dil, r, n in batch]
        kp = [ks[gi, rows(dil, r, n - 1), :] if n > 0 else None for gi, dil, r, n in batch]
        s_cur = [jnp.where(cur_ok, _mm_nt(q, k), -jnp.inf) for q, k in zip(qb, kc)]
        s_prev = [None if k is None else jnp.where(prev_ok, _mm_nt(q, k), -jnp.inf) for q, k in zip(qb, kp)]
        m = [jnp.max(a, axis=-1, keepdims=True) if b is None else
             jnp.maximum(jnp.max(a, axis=-1, keepdims=True), jnp.max(b, axis=-1, keepdims=True))
             for a, b in zip(s_cur, s_prev)]
        p_cur = [jnp.exp(a - mm) for a, mm in zip(s_cur, m)]
        p_prev = [None if b is None else jnp.exp(b - mm) for b, mm in zip(s_prev, m)]
        den = [jnp.sum(a, axis=-1, keepdims=True) if b is None else
               jnp.sum(a, axis=-1, keepdims=True) + jnp.sum(b, axis=-1, keepdims=True)
               for a, b in zip(p_cur, p_prev)]
        o = [_mm(p, vs[gi, rows(dil, r, n), :]) for p, (gi, dil, r, n) in zip(p_cur, batch)]
        o = [a if p is None else a + _mm(p, vs[gi, rows(dil, r, n - 1), :])
             for a, p, (gi, dil, r, n) in zip(o, p_prev, batch)]
        for (gi, dil, r, n), a, d, mm in zip(batch, o, den, m):
            os_[gi, rows(dil, r, n), :] = a / d
            ls[gi, rows(dil, r, n), :] = jnp.broadcast_to(mm + jnp.log(d), (blk, DIL_HD))

    lse = [ls[gi] for gi in range(ngrp)]
    mx = functools.reduce(jnp.maximum, lse)
    ex = [jnp.exp(l - mx) for l in lse]
    tot = functools.reduce(lambda a, b: a + b, ex)
    y = functools.reduce(lambda a, b: a + b, [ex[gi] / tot * os_[gi] for gi in range(ngrp)])
    o_ref[0] = y.astype(o_ref.dtype)


def _dil(proj, q_norm_g, k_norm_g):
    bsz, s, _ = proj.shape
    ngrp = len(DIL_PATTERN)

    def specs(col0):
        blk0 = col0 // DIL_HD
        return [pl.BlockSpec((1, s, DIL_HD), lambda b, h, g=g: (b, 0, blk0 + g * DIL_HEADS_PER_GROUP + h))
                for g in range(ngrp)]

    gain = pl.BlockSpec((1, DIL_HD), lambda b, h: (0, 0))
    return pl.pallas_call(
        _dil_kernel,
        grid=(bsz, DIL_HEADS_PER_GROUP),
        in_specs=specs(COL_DQ) + specs(COL_DK) + specs(COL_DV) + [gain, gain],
        out_specs=pl.BlockSpec((1, s, DIL_HD), lambda b, h: (b, 0, h)),
        out_shape=jax.ShapeDtypeStruct((bsz, s, DIL_OUT_W), BF16),
        scratch_shapes=[pltpu.VMEM((ngrp, s, DIL_HD), F32) for _ in range(5)],
        compiler_params=_params("parallel", "parallel"),
        name="dilattn",
    )(*([proj] * 9), q_norm_g.reshape(1, DIL_HD), k_norm_g.reshape(1, DIL_HD))


def _route(logits, bias):
    e, tm = logits.shape
    per = e // N_GROUPS
    scores = _sigmoid(logits)
    sel = scores + bias
    neg = -jnp.inf
    sub = lax.broadcasted_iota(jnp.int32, (per, tm), 0)
    gs_rows = []
    for g in range(N_GROUPS):
        blk = sel[g * per:(g + 1) * per, :]
        m1 = jnp.max(blk, axis=0, keepdims=True)
        i1 = jnp.min(jnp.where(blk == m1, sub, per), axis=0, keepdims=True)
        m2 = jnp.max(jnp.where(sub == i1, neg, blk), axis=0, keepdims=True)
        gs_rows.append(m1 + m2)
    gs = jnp.concatenate(gs_rows, axis=0)
    gi = lax.broadcasted_iota(jnp.int32, (N_GROUPS, tm), 0)
    gsel = jnp.zeros((N_GROUPS, tm), F32)
    for _ in range(TOPK_GROUPS):
        m = jnp.max(gs, axis=0, keepdims=True)
        idx = jnp.min(jnp.where(gs == m, gi, N_GROUPS), axis=0, keepdims=True)
        hit = gi == idx
        gsel = jnp.where(hit, 1.0, gsel)
        gs = jnp.where(hit, neg, gs)
    cand = jnp.concatenate(
        [jnp.where(gsel[g:g + 1, :] > 0.0, sel[g * per:(g + 1) * per, :], neg) for g in range(N_GROUPS)], axis=0)
    ei = lax.broadcasted_iota(jnp.int32, (e, tm), 0)
    chosen = jnp.zeros((e, tm), F32)
    for _ in range(TOP_K):
        m = jnp.max(cand, axis=0, keepdims=True)
        idx = jnp.min(jnp.where(cand == m, ei, e), axis=0, keepdims=True)
        hit = ei == idx
        chosen = jnp.where(hit, scores, chosen)
        cand = jnp.where(hit, neg, cand)
    return chosen / jnp.sum(chosen, axis=0, keepdims=True) * ROUTE_SCALE


def _merge_kernel(x_ref, mod_ref, ya_ref, yb_ref, ga_ref, gb_ref, wg_ref, wd_ref, wo_ref,
                  gf_ref, wr_ref, rb_ref, x1_ref, h2_ref, comb_ref, rank_ref, cnt_ref):
    gt1 = mod_ref[0, 2:3, :]
    sh2 = mod_ref[0, 3:4, :]
    sc2 = mod_ref[0, 4:5, :]
    ua = jnp.dot(ya_ref[0], wg_ref[...], preferred_element_type=F32)
    ub = jnp.dot(yb_ref[0], wd_ref[...], preferred_element_type=F32)
    merged = _sigmoid(ga_ref[0].astype(F32)) * ua + _sigmoid(gb_ref[0].astype(F32)) * ub
    x1 = x_ref[0] + gt1 * _mm(merged, wo_ref[...])
    x1_ref[0] = x1
    h2 = x1 * lax.rsqrt(jnp.mean(x1 * x1, axis=-1, keepdims=True) + EPS) * gf_ref[...] * (1.0 + sc2) + sh2
    h2_ref[0] = h2.astype(BF16)
    logits = lax.dot_general(wr_ref[...], h2, (((1,), (1,)), ((), ())),
                             preferred_element_type=F32, precision=lax.Precision.HIGHEST)
    comb = _route(logits, rb_ref[...])
    comb_ref[0] = comb
    tb = MOE_TB
    before = (lax.broadcasted_iota(jnp.int32, (tb, tb), 0) < lax.broadcasted_iota(jnp.int32, (tb, tb), 1))
    before = before.astype(BF16)
    for j in range(comb.shape[1] // tb):
        picked = (comb[:, j * tb:(j + 1) * tb] > 0.0).astype(F32)
        rank_ref[0, :, j * tb:(j + 1) * tb] = jnp.dot(picked.astype(BF16), before, preferred_element_type=F32)
        cnt_ref[0, j] = jnp.broadcast_to(jnp.sum(picked, axis=1, keepdims=True), (N_EXPERTS, LANES))


def _merge(x, mod3, y_a, y_b, proj, w_up_gdn, w_up_dil, w_out, g_ffn, w_router_t, router_bias):
    bsz, s, d = x.shape
    tm = 512
    row = lambda w: pl.BlockSpec((1, tm, w), lambda b, i: (b, i, 0))
    full = lambda a: pl.BlockSpec(a.shape, lambda b, i: (0,) * a.ndim)
    ex = pl.BlockSpec((1, N_EXPERTS, tm), lambda b, i: (b, 0, i))
    return pl.pallas_call(
        _merge_kernel,
        grid=(bsz, s // tm),
        in_specs=[row(d),
                  pl.BlockSpec((1, 6, d), lambda b, i: (b, 0, 0)),
                  row(GDN_W), row(DIL_OUT_W),
                  pl.BlockSpec((1, tm, d), lambda b, i: (b, i, COL_GA // d)),
                  pl.BlockSpec((1, tm, d), lambda b, i: (b, i, COL_GB // d)),
                  full(w_up_gdn), full(w_up_dil), full(w_out), full(g_ffn), full(w_router_t),
                  full(router_bias)],
        out_specs=[row(d), row(d), ex, ex,
                   pl.BlockSpec((1, tm // MOE_TB, N_EXPERTS, LANES), lambda b, i: (b, i, 0, 0))],
        out_shape=[jax.ShapeDtypeStruct((bsz, s, d), F32),
                   jax.ShapeDtypeStruct((bsz, s, d), BF16),
                   jax.ShapeDtypeStruct((bsz, N_EXPERTS, s), F32),
                   jax.ShapeDtypeStruct((bsz, N_EXPERTS, s), F32),
                   jax.ShapeDtypeStruct((bsz, s // MOE_TB, N_EXPERTS, LANES), F32)],
        compiler_params=_params("parallel", "parallel"),
        name="merge_router",
    )(x, mod3, y_a, y_b, proj, proj, w_up_gdn, w_up_dil, w_out, g_ffn, w_router_t, router_bias)


MOE_TB = 256
MOE_CH = 16
MOE_RB = MOE_TB * TOP_K + N_EXPERTS * MOE_CH
MOE_CPB = MOE_RB // MOE_CH
MOE_TM = 512
MOE_CPT = MOE_TM // MOE_CH


def _moe_plan(cnt):
    nb, ne = cnt.shape
    i32 = jnp.int32
    nch = (cnt + MOE_CH - 1) // MOE_CH
    seg0 = jnp.cumsum(nch, axis=1) - nch
    pad = jnp.zeros((nb, LANES - ne), i32)
    meta = jnp.stack([jnp.concatenate([seg0 * MOE_CH, pad + MOE_RB], axis=1),
                      jnp.concatenate([nch * MOE_CH, pad], axis=1)], axis=1).astype(F32)
    meta = jnp.pad(meta, ((0, 0), (0, 6), (0, 0)))
    nch_e = nch.T
    cum_e = jnp.cumsum(nch_e, axis=1)
    tot_e = cum_e[:, -1]
    tiles_e = (tot_e + MOE_CPT - 1) // MOE_CPT
    tile_end = jnp.cumsum(tiles_e)
    ntiles = tile_end[-1]
    max_tiles = (nb * MOE_CPB) // MOE_CPT + ne
    i = jnp.arange(max_tiles + 1, dtype=i32)
    te = jnp.sum((tile_end[None, :] <= i[:, None]).astype(i32), axis=1)
    te = jnp.minimum(te, jnp.sum((tile_end <= ntiles - 1).astype(i32)))
    te = jnp.minimum(te, ne - 1)
    oh_te = (te[:, None] == jnp.arange(ne, dtype=i32)[None, :]).astype(i32)
    pick = lambda tab: jnp.sum(oh_te[:, :, None] * tab[None, :, :], axis=1)
    tile_off_t = jnp.sum(oh_te * (tile_end - tiles_e)[None, :], axis=1)
    tot_t = jnp.sum(oh_te * tot_e[None, :], axis=1)
    cum_t, nch_t, seg0_t = pick(cum_e), pick(nch_e), pick(seg0.T)
    q = (i - tile_off_t)[:, None] * MOE_CPT + jnp.arange(MOE_CPT, dtype=i32)[None, :]
    valid = (q < tot_t[:, None]) & (i < ntiles)[:, None]
    blk = jnp.minimum(jnp.sum((cum_t[:, None, :] <= q[:, :, None]).astype(i32), axis=-1), nb - 1)
    oh_b = (blk[:, :, None] == jnp.arange(nb, dtype=i32)[None, None, :]).astype(i32)
    before = jnp.sum(oh_b * (cum_t - nch_t)[:, None, :], axis=-1)
    src = blk * MOE_CPB + jnp.sum(oh_b * seg0_t[:, None, :], axis=-1) + (q - before)
    spare = nb * MOE_CPB + (i % 2)[:, None] * MOE_CPT + jnp.arange(MOE_CPT, dtype=i32)[None, :]
    tbl_in = jnp.where(valid, src, nb * MOE_CPB + 2 * MOE_CPT).astype(i32).reshape(-1)
    tbl_out = jnp.where(valid, src, spare).astype(i32).reshape(-1)
    used = (jnp.sum(nch, axis=1) * MOE_CH).astype(i32)
    return meta, used, te.astype(i32), tbl_in, tbl_out, ntiles.astype(i32).reshape(1)


def _onehot_rows(meta_ref, rhs, r0, nrows, weighted):
    start = meta_ref[0, 0:1, :]
    plen = meta_ref[0, 1:2, :]
    r = (lax.broadcasted_iota(jnp.int32, (nrows, LANES), 0) + r0).astype(F32)
    owner = jnp.where((r >= start) & (r < start + plen), 1.0, 0.0)
    pos = r[:, :1] - jnp.sum(owner * start, axis=1, keepdims=True)
    got = jnp.dot(owner[:, :N_EXPERTS].astype(BF16), rhs, preferred_element_type=F32)
    tb = MOE_TB
    hit = (got[:, :tb] == pos) & (got[:, tb:2 * tb] > 0.5)
    return jnp.where(hit, got[:, 2 * tb:] if weighted else 1.0, 0.0).astype(BF16)


def _route_rhs(rank_ref, comb_ref):
    comb = comb_ref[0]
    picked = jnp.where(comb > 0.0, 1.0, 0.0)
    return jnp.concatenate([rank_ref[0], picked, comb], axis=1).astype(BF16)


def _dispatch_kernel(used_ref, meta_ref, h_ref, rank_ref, comb_ref, x_ref):
    nblk = pl.num_programs(0) - 1
    blk = pl.program_id(0)

    @pl.when(blk < nblk)
    def _():
        rhs = _route_rhs(rank_ref, comb_ref)
        h = h_ref[0]
        for r0 in range(0, MOE_RB, MOE_TM):
            n = min(MOE_TM, MOE_RB - r0)

            @pl.when(r0 < used_ref[blk])
            def _():
                p = _onehot_rows(meta_ref, rhs, r0, n, weighted=False)
                x_ref[0, r0:r0 + n, :] = jnp.dot(p, h, preferred_element_type=F32).astype(BF16)

            @pl.when(r0 >= used_ref[blk])
            def _():
                x_ref[0, r0:r0 + n, :] = jnp.zeros((n, x_ref.shape[2]), BF16)

    @pl.when(blk == nblk)
    def _():
        x_ref[...] = jnp.zeros_like(x_ref)


def _dispatch(h2, rank_t, comb_t, meta, used):
    bsz, s, d = h2.shape
    nsb = s // MOE_TB
    nblk = bsz * nsb
    tok = lambda i: (jnp.minimum(i, nblk - 1) // nsb, jnp.minimum(i, nblk - 1) % nsb)
    ex = pl.BlockSpec((1, N_EXPERTS, MOE_TB), lambda i, u: (tok(i)[0], 0, tok(i)[1]))
    return pl.pallas_call(
        _dispatch_kernel,
        grid_spec=pltpu.PrefetchScalarGridSpec(
            num_scalar_prefetch=1,
            grid=(nblk + 1,),
            in_specs=[pl.BlockSpec((1, 8, LANES), lambda i, u: (jnp.minimum(i, nblk - 1), 0, 0)),
                      pl.BlockSpec((1, MOE_TB, d), lambda i, u: (tok(i)[0], tok(i)[1], 0)),
                      ex, ex],
            out_specs=pl.BlockSpec((1, MOE_RB, d), lambda i, u: (i, 0, 0))),
        out_shape=jax.ShapeDtypeStruct((nblk + 1, MOE_RB, d), BF16),
        compiler_params=_params("parallel"),
        name="moe_dispatch",
    )(used, meta, h2, rank_t, comb_t)


def _ffn_kernel(te_ref, tin_ref, tout_ref, nt_ref, x_hbm, wg_ref, wu_ref, wd_ref, y_hbm,
                xbuf, ybuf, wgb, wub, wdb, sem_in, sem_out):
    i = pl.program_id(0)
    nt = nt_ref[0]
    slot = i % 2

    def copy_in(t, s, j):
        return pltpu.make_async_copy(x_hbm.at[tin_ref[t * MOE_CPT + j]],
                                     xbuf.at[s, pl.ds(j * MOE_CH, MOE_CH)], sem_in.at[s])

    def copy_out(t, s, j):
        return pltpu.make_async_copy(ybuf.at[s, pl.ds(j * MOE_CH, MOE_CH)],
                                     y_hbm.at[tout_ref[t * MOE_CPT + j]], sem_out.at[s])

    @pl.when(i == 0)
    def _():
        for j in range(MOE_CPT):
            copy_in(0, 0, j).start()

    @pl.when(i < nt)
    def _():
        for j in range(MOE_CPT):
            copy_in(i, slot, j).wait()
        for j in range(MOE_CPT):
            copy_in(i + 1, 1 - slot, j).start()

        @pl.when(i >= 2)
        def _():
            for j in range(MOE_CPT):
                copy_out(i - 2, slot, j).wait()

        @pl.when(jnp.logical_or(i == 0, te_ref[i] != te_ref[jnp.maximum(i - 1, 0)]))
        def _():
            wgb[...] = wg_ref[0].astype(BF16)
            wub[...] = wu_ref[0].astype(BF16)
            wdb[...] = wd_ref[0].astype(BF16)

        half = MOE_TM // 2
        for hf in range(2):
            x = xbuf[slot, hf * half:(hf + 1) * half, :]
            a = jnp.dot(x, wgb[...], preferred_element_type=F32)
            u = jnp.dot(x, wub[...], preferred_element_type=F32)
            y = jnp.dot((_silu(a) * u).astype(BF16), wdb[...], preferred_element_type=F32)
            ybuf[slot, hf * half:(hf + 1) * half, :] = y.astype(BF16)
            for j in range(hf * MOE_CPT // 2, (hf + 1) * MOE_CPT // 2):
                copy_out(i, slot, j).start()

        @pl.when(i == nt - 1)
        def _():
            for j in range(MOE_CPT):
                copy_in(i + 1, 1 - slot, j).wait()

            @pl.when(i >= 1)
            def _():
                for j in range(MOE_CPT):
                    copy_out(i - 1, 1 - slot, j).wait()
            for j in range(MOE_CPT):
                copy_out(i, slot, j).wait()


def _ffn(x_rows, w_eg, w_eu, w_ed, te, tbl_in, tbl_out, ntiles):
    nblk, rb, d = x_rows.shape
    ne, _, de = w_eg.shape
    xc = x_rows.reshape(nblk * MOE_CPB, MOE_CH, d)
    max_tiles = te.shape[0] - 1
    y = pl.pallas_call(
        _ffn_kernel,
        grid_spec=pltpu.PrefetchScalarGridSpec(
            num_scalar_prefetch=4,
            grid=(max_tiles,),
            in_specs=[pl.BlockSpec(memory_space=pl.ANY),
                      pl.BlockSpec((1, d, de), lambda i, te, *_: (te[i], 0, 0)),
                      pl.BlockSpec((1, d, de), lambda i, te, *_: (te[i], 0, 0)),
                      pl.BlockSpec((1, de, d), lambda i, te, *_: (te[i], 0, 0))],
            out_specs=pl.BlockSpec(memory_space=pl.ANY),
            scratch_shapes=[pltpu.VMEM((2, MOE_TM, d), BF16), pltpu.VMEM((2, MOE_TM, d), BF16),
                            pltpu.VMEM((d, de), BF16), pltpu.VMEM((d, de), BF16), pltpu.VMEM((de, d), BF16),
                            pltpu.SemaphoreType.DMA((2,)), pltpu.SemaphoreType.DMA((2,))]),
        out_shape=jax.ShapeDtypeStruct(xc.shape, BF16),
        input_output_aliases={4: 0},
        compiler_params=_params("arbitrary"),
        name="moe_ffn",
    )(te, tbl_in, tbl_out, ntiles, xc, w_eg, w_eu, w_ed)
    return y.reshape(nblk, rb, d)


def _combine_kernel(used_ref, meta_ref, y_ref, rank_ref, comb_ref, h_ref, x1_ref, mod_ref,
                    sg_ref, su_ref, sd_ref, o_ref, acc_ref):
    blk = pl.program_id(0) * pl.num_programs(1) + pl.program_id(1)
    rhs = _route_rhs(rank_ref, comb_ref)
    h = h_ref[0]
    a = jnp.dot(h, sg_ref[...], preferred_element_type=F32)
    u = jnp.dot(h, su_ref[...], preferred_element_type=F32)
    acc_ref[...] = _mm(_silu(a) * u, sd_ref[...])
    for r0 in range(0, MOE_RB, MOE_TM):
        n = min(MOE_TM, MOE_RB - r0)

        @pl.when(r0 < used_ref[blk])
        def _():
            p = _onehot_rows(meta_ref, rhs, r0, n, weighted=True)
            acc_ref[...] += lax.dot_general(p, y_ref[0, r0:r0 + n, :], (((0,), (0,)), ((), ())),
                                            preferred_element_type=F32)
    o_ref[0] = x1_ref[0] + mod_ref[0, 5:6, :] * acc_ref[...]


def _combine(y_rows, rank_t, comb_t, h2, x1, mod3, sg, su, sd, meta, used):
    bsz, s, d = x1.shape
    nsb = s // MOE_TB
    tok = lambda w: pl.BlockSpec((1, MOE_TB, w), lambda b, i, u: (b, i, 0))
    ex = pl.BlockSpec((1, N_EXPERTS, MOE_TB), lambda b, i, u: (b, 0, i))
    full = lambda a: pl.BlockSpec(a.shape, lambda b, i, u: (0,) * a.ndim)
    return pl.pallas_call(
        _combine_kernel,
        grid_spec=pltpu.PrefetchScalarGridSpec(
            num_scalar_prefetch=1,
            grid=(bsz, nsb),
            in_specs=[pl.BlockSpec((1, 8, LANES), lambda b, i, u: (b * nsb + i, 0, 0)),
                      pl.BlockSpec((1, MOE_RB, d), lambda b, i, u: (b * nsb + i, 0, 0)),
                      ex, ex, tok(d), tok(d),
                      pl.BlockSpec((1, 6, d), lambda b, i, u: (b, 0, 0)),
                      full(sg), full(su), full(sd)],
            out_specs=tok(d),
            scratch_shapes=[pltpu.VMEM((MOE_TB, d), F32)]),
        out_shape=jax.ShapeDtypeStruct((bsz, s, d), F32),
        compiler_params=_params("parallel", "parallel"),
        name="moe_combine",
    )(used, meta, y_rows, rank_t, comb_t, h2, x1, mod3, sg, su, sd)


def _moe(h2, comb_t, rank_t, cnt, x1, mod3, w_eg, w_eu, w_ed, sg, su, sd):
    meta, used, te, tbl_in, tbl_out, ntiles = _moe_plan(cnt)
    x_rows = _dispatch(h2, rank_t, comb_t, meta, used)
    y_rows = _ffn(x_rows, w_eg, w_eu, w_ed, te, tbl_in, tbl_out, ntiles)
    return _combine(y_rows, rank_t, comb_t, h2, x1, mod3, sg, su, sd, meta, used)


def _layer(x, cmod, g_mix, w_in, conv_w, a_log, dt_bias, norm_g, qn_g, kn_g, w_up_gdn, w_up_dil, w_out,
           g_ffn, w_router, router_bias, w_eg, w_eu, w_ed, w_sg, w_su, w_sd):
    bsz, s, d = x.shape
    mod3 = cmod.reshape(bsz, 6, d)
    o_ba = 4 * GDN_W
    o_dq = o_ba + 2 * GDN_HEADS
    o_ga = o_dq + 3 * DIL_W
    w_main = jnp.concatenate([w_in[:, :o_ba], w_in[:, o_ga:], w_in[:, o_dq:o_ga]], axis=1).astype(BF16)
    pad_hi = LANES - 2 * GDN_HEADS
    w_ba = jnp.pad(w_in[:, o_ba:o_dq], ((0, 0), (0, pad_hi))).astype(BF16)
    alog_vec = jnp.pad(a_log, (GDN_HEADS, pad_hi)).reshape(1, LANES)
    dtb_vec = jnp.pad(dt_bias, (GDN_HEADS, pad_hi)).reshape(1, LANES)
    proj, gates = _inproj(x, mod3, g_mix, w_main, w_ba, alog_vec, dtb_vec)

    conv_w3 = conv_w.reshape(GDN_CONV, 3, GDN_W).transpose(1, 0, 2)
    y_a = _gdn(proj, gates, conv_w3, norm_g)
    y_b = _dil(proj, qn_g, kn_g)

    x1, h2, comb_t, rank_t, cnt = _merge(x, mod3, y_a, y_b, proj, w_up_gdn.astype(BF16),
                                         w_up_dil.astype(BF16), w_out.astype(BF16), g_ffn.reshape(1, d),
                                         w_router.T, router_bias.reshape(N_EXPERTS, 1))
    cnt = cnt[..., 0].astype(jnp.int32).reshape(-1, N_EXPERTS)
    return _moe(h2, comb_t, rank_t, cnt, x1, mod3, w_eg, w_eu, w_ed,
                w_sg.astype(BF16), w_su.astype(BF16), w_sd.astype(BF16))


def kernel(x, c, w_ada, b_ada, g_mix, w_in, gdn_conv_w, gdn_a_log, gdn_dt_bias, gdn_norm_g, dil_q_norm_g, dil_k_norm_g, w_up_gdn, w_up_dil, w_out, g_ffn, w_router, router_bias, w_exp_gate, w_exp_up, w_exp_down, w_sh_gate, w_sh_up, w_sh_down):
    for l in range(w_ada.shape[0]):
        cmod = _ada(c, w_ada[l], b_ada[l])
        x = _layer(x, cmod, g_mix[l], w_in[l], gdn_conv_w[l], gdn_a_log[l], gdn_dt_bias[l], gdn_norm_g[l],
                   dil_q_norm_g[l], dil_k_norm_g[l], w_up_gdn[l], w_up_dil[l], w_out[l], g_ffn[l],
                   w_router[l], router_bias[l], w_exp_gate[l], w_exp_up[l], w_exp_down[l],
                   w_sh_gate[l], w_sh_up[l], w_sh_down[l])
    return x
```

```python
import functools

import jax
import jax.numpy as jnp
from jax import lax
from jax.experimental import pallas as pl
from jax.experimental.pallas import tpu as pltpu

F32 = jnp.float32
BF16 = jnp.bfloat16

D_MODEL = 1024
GDN_HEADS = 8
GDN_DK = 128
GDN_CONV = 4
DIL_PATTERN = ((128, 1), (512, 4), (2048, 16))
DIL_HEADS_PER_GROUP = 4
DIL_HD = 128
DIL_BLOCK = 128
DIL_BATCH = 4
N_EXPERTS = 64
TOP_K = 8
N_GROUPS = 8
TOPK_GROUPS = 4
D_EXPERT = 256
ROUTE_SCALE = 2.5
EPS = 1e-6

GDN_W = GDN_HEADS * GDN_DK
DIL_HEADS = len(DIL_PATTERN) * DIL_HEADS_PER_GROUP
DIL_W = DIL_HEADS * DIL_HD
DIL_OUT_W = DIL_HEADS_PER_GROUP * DIL_HD
COL_GQ, COL_GK, COL_GV, COL_GZ = 0, GDN_W, 2 * GDN_W, 3 * GDN_W
COL_GA = 4 * GDN_W
COL_GB = COL_GA + D_MODEL
COL_DQ = COL_GB + D_MODEL
COL_DK = COL_DQ + DIL_W
COL_DV = COL_DK + DIL_W
PROJ_W = COL_DV + DIL_W
LANES = 128

VMEM_LIMIT = 56 * 1024 * 1024


def _params(*sem):
    return pltpu.CompilerParams(dimension_semantics=sem, vmem_limit_bytes=VMEM_LIMIT)


def _sigmoid(x):
    return 1.0 / (1.0 + jnp.exp(-x))


def _silu(x):
    return x * _sigmoid(x)


def _softplus(x):
    return jnp.maximum(x, 0.0) + jnp.log(1.0 + jnp.exp(-jnp.abs(x)))


def _mm(a, b):
    return jnp.dot(a.astype(BF16), b.astype(BF16), preferred_element_type=F32)


def _mm_nt(a, b):
    return lax.dot_general(a.astype(BF16), b.astype(BF16), (((1,), (1,)), ((), ())),
                           preferred_element_type=F32)


def _mm_tn(a, b):
    return lax.dot_general(a.astype(BF16), b.astype(BF16), (((0,), (0,)), ((), ())),
                           preferred_element_type=F32)


def _mm_f32(a, b):
    return jnp.dot(a, b, preferred_element_type=F32, precision=lax.Precision.HIGHEST)


def _ada_kernel(c_ref, w_ref, b_ref, o_ref):
    cs = _silu(c_ref[...])
    o_ref[...] = _mm_f32(cs, w_ref[...]) + b_ref[...]


def _ada(c, w_ada, b_ada):
    bsz, d = c.shape
    n = w_ada.shape[1]
    tn = 1536
    return pl.pallas_call(
        _ada_kernel,
        grid=(n // tn,),
        in_specs=[pl.BlockSpec((bsz, d), lambda j: (0, 0)),
                  pl.BlockSpec((d, tn), lambda j: (0, j)),
                  pl.BlockSpec((1, tn), lambda j: (0, j))],
        out_specs=pl.BlockSpec((bsz, tn), lambda j: (0, j)),
        out_shape=jax.ShapeDtypeStruct((bsz, n), F32),
        compiler_params=_params("parallel"),
        name="ada",
    )(c, w_ada, b_ada.reshape(1, n))


def _inproj_kernel(x_ref, mod_ref, g_ref, w_ref, wba_ref, alog_ref, dtb_ref,
                   proj_ref, gates_ref, h_scr):
    j = pl.program_id(2)

    @pl.when(j == 0)
    def _():
        x = x_ref[0]
        sh1 = mod_ref[0, 0:1, :]
        sc1 = mod_ref[0, 1:2, :]
        ms = jnp.mean(x * x, axis=-1, keepdims=True)
        h = x * lax.rsqrt(ms + EPS) * g_ref[...] * (1.0 + sc1) + sh1
        hb = h.astype(BF16)
        h_scr[...] = hb
        ba = jnp.dot(hb, wba_ref[...], preferred_element_type=F32)
        lane = lax.broadcasted_iota(jnp.int32, ba.shape, 1)
        beta = _sigmoid(ba)
        g = -jnp.exp(alog_ref[...]) * _softplus(ba + dtb_ref[...])
        gates_ref[0] = jnp.where(lane < GDN_HEADS, beta, g)

    proj_ref[0] = jnp.dot(h_scr[...], w_ref[...], preferred_element_type=F32).astype(BF16)


def _inproj(x, mod3, g_mix, w_main, w_ba, alog_vec, dtb_vec):
    bsz, s, d = x.shape
    tm, tn = 1024, 1536
    return pl.pallas_call(
        _inproj_kernel,
        grid=(bsz, s // tm, PROJ_W // tn),
        in_specs=[pl.BlockSpec((1, tm, d), lambda b, i, j: (b, i, 0)),
                  pl.BlockSpec((1, 6, d), lambda b, i, j: (b, 0, 0)),
                  pl.BlockSpec((1, d), lambda b, i, j: (0, 0)),
                  pl.BlockSpec((d, tn), lambda b, i, j: (0, j)),
                  pl.BlockSpec((d, LANES), lambda b, i, j: (0, 0)),
                  pl.BlockSpec((1, LANES), lambda b, i, j: (0, 0)),
                  pl.BlockSpec((1, LANES), lambda b, i, j: (0, 0))],
        out_specs=[pl.BlockSpec((1, tm, tn), lambda b, i, j: (b, i, j)),
                   pl.BlockSpec((1, tm, LANES), lambda b, i, j: (b, i, 0))],
        out_shape=[jax.ShapeDtypeStruct((bsz, s, PROJ_W), BF16),
                   jax.ShapeDtypeStruct((bsz, s, LANES), F32)],
        scratch_shapes=[pltpu.VMEM((tm, d), BF16)],
        compiler_params=_params("parallel", "parallel", "arbitrary"),
        name="inproj",
    )(x, mod3, g_mix.reshape(1, d), w_main, w_ba, alog_vec, dtb_vec)


GDN_C = 128


def _cumsum_rows(g):
    row = lax.broadcasted_iota(jnp.int32, g.shape, 0)
    sft = 1
    while sft < g.shape[0]:
        g = g + jnp.where(row >= sft, pltpu.roll(g, sft, 0), 0.0)
        sft *= 2
    return g


def _unit_lower_inverses(mats):
    n = mats[0].shape[0]
    row = lax.broadcasted_iota(jnp.int32, (n, n), 0)
    col = lax.broadcasted_iota(jnp.int32, (n, n), 1)
    eye = (row == col).astype(F32)
    ts = None
    b = 1
    while b < n:
        off = ((row // (2 * b)) == (col // (2 * b))) & ((row // b) % 2 == 1) & ((col // b) % 2 == 0)
        if b == 1:
            ts = [eye - jnp.where(off, a, 0.0) for a in mats]
        else:
            tb = [t.astype(BF16) for t in ts]
            inner = [jnp.dot(jnp.where(off, a, 0.0).astype(BF16), t, preferred_element_type=F32)
                     for a, t in zip(mats, tb)]
            ts = [t - jnp.dot(t16, i.astype(BF16), preferred_element_type=F32)
                  for t, t16, i in zip(ts, tb, inner)]
        b *= 2
    return ts


def _gdn_kernel(q_ref, k_ref, v_ref, z_ref, gates_ref, cw_ref, ng_ref, o_ref,
                qprev, kprev, vprev, state):
    sblk = pl.program_id(1)
    c = GDN_C
    nh = GDN_HEADS
    w = nh * GDN_DK

    @pl.when(sblk == 0)
    def _():
        state[...] = jnp.zeros_like(state)
        for buf in (qprev, kprev, vprev):
            buf[...] = jnp.zeros_like(buf)

    taps = GDN_CONV - 1
    si = lax.broadcasted_iota(jnp.int32, (taps * c, 2 * c), 0)
    sm = lax.broadcasted_iota(jnp.int32, (taps * c, 2 * c), 1)
    shift_mat = jnp.where(sm == c + si % c - (taps - si // c), 1.0, 0.0).astype(BF16)
    conv = []
    for idx, (ref, prev) in enumerate(((q_ref, qprev), (k_ref, kprev), (v_ref, vprev))):
        cur = ref[0]
        delayed = jnp.dot(shift_mat, jnp.concatenate([prev[...], cur], axis=0), preferred_element_type=F32)
        acc = cur.astype(F32) * cw_ref[idx, taps:taps + 1, :]
        for j in range(taps):
            acc = acc + delayed[j * c:(j + 1) * c, :] * cw_ref[idx, j:j + 1, :]
        prev[...] = cur
        conv.append(_silu(acc))
    qc, kc, vc = conv

    gates = gates_ref[0]
    gcum = _cumsum_rows(gates)
    row = lax.broadcasted_iota(jnp.int32, (c, c), 0)
    col = lax.broadcasted_iota(jnp.int32, (c, c), 1)
    eye = row == col
    incl = col <= row
    strict = col < row

    heads = range(nh)
    hsl = [slice(h * GDN_DK, (h + 1) * GDN_DK) for h in heads]
    q = [qc[:, s] for s in hsl]
    k = [kc[:, s] for s in hsl]
    v = [vc[:, s] for s in hsl]
    q = [x * lax.rsqrt(jnp.sum(x * x, axis=-1, keepdims=True) + EPS) * (GDN_DK ** -0.5) for x in q]
    k = [x * lax.rsqrt(jnp.sum(x * x, axis=-1, keepdims=True) + EPS) for x in k]
    beta = [jnp.broadcast_to(gates[:, h:h + 1], (c, GDN_DK)) for h in heads]
    gc = [jnp.broadcast_to(gcum[:, nh + h:nh + h + 1], (c, GDN_DK)) for h in heads]
    gc_row = [jnp.sum(jnp.where(eye, x, 0.0), axis=0, keepdims=True) for x in gc]
    decay = [jnp.exp(jnp.where(incl, x - y, -jnp.inf)) for x, y in zip(gc, gc_row)]
    egc = [jnp.exp(x) for x in gc]
    kb = [x * b for x, b in zip(k, beta)]
    scores = [_mm_nt(jnp.concatenate([a, b], axis=0), x) for a, b, x in zip(kb, q, k)]
    a_mat = [jnp.where(strict, s[:c] * d, 0.0) for s, d in zip(scores, decay)]
    qk = [s[c:] * d for s, d in zip(scores, decay)]
    t_inv = _unit_lower_inverses(a_mat)
    sol = [_mm(t, jnp.concatenate([x * b, y * e], axis=1))
           for t, x, b, y, e in zip(t_inv, v, beta, kb, egc)]

    st = [state[h] for h in heads]
    ws = [_mm(jnp.concatenate([s[:, GDN_DK:], x * e], axis=0), m)
          for s, x, e, m in zip(sol, q, egc, st)]
    v_new = [s[:, :GDN_DK] - x[:c] for s, x in zip(sol, ws)]
    o = [x[c:] + _mm(a, b) for x, a, b in zip(ws, qk, v_new)]
    gc_last = [x[c - 1:c, :] for x in gc]
    k_dec = [x * jnp.exp(l - g) for x, l, g in zip(k, gc_last, gc)]
    for h in heads:
        state[h] = st[h] * jnp.exp(gc_last[h]) + _mm_tn(k_dec[h], v_new[h])
        y = o[h] * lax.rsqrt(jnp.mean(o[h] * o[h], axis=-1, keepdims=True) + EPS) * ng_ref[...]
        o_ref[0, :, hsl[h]] = (y * _silu(z_ref[0, :, hsl[h]].astype(F32))).astype(o_ref.dtype)


def _gdn(proj, gates, conv_w3, norm_g):
    bsz, s, _ = proj.shape
    w = GDN_W

    def col_spec(col0):
        return pl.BlockSpec((1, GDN_C, w), lambda b, i: (b, i, col0 // w))

    return pl.pallas_call(
        _gdn_kernel,
        grid=(bsz, s // GDN_C),
        in_specs=[col_spec(COL_GQ), col_spec(COL_GK), col_spec(COL_GV), col_spec(COL_GZ),
                  pl.BlockSpec((1, GDN_C, LANES), lambda b, i: (b, i, 0)),
                  pl.BlockSpec((3, GDN_CONV, w), lambda b, i: (0, 0, 0)),
                  pl.BlockSpec((1, GDN_DK), lambda b, i: (0, 0))],
        out_specs=pl.BlockSpec((1, GDN_C, w), lambda b, i: (b, i, 0)),
        out_shape=jax.ShapeDtypeStruct((bsz, s, GDN_W), BF16),
        scratch_shapes=[pltpu.VMEM((GDN_C, w), BF16),
                        pltpu.VMEM((GDN_C, w), BF16),
                        pltpu.VMEM((GDN_C, w), BF16),
                        pltpu.VMEM((GDN_HEADS, GDN_DK, GDN_DK), F32)],
        compiler_params=_params("parallel", "arbitrary"),
        name="gdn",
    )(proj, proj, proj, proj, gates, conv_w3, norm_g.reshape(1, GDN_DK))


def _dil_kernel(q0, q1, q2, k0, k1, k2, v0, v1, v2, qg_ref, kg_ref, o_ref, qs, ks, vs, os_, ls):
    s = qs.shape[1]
    blk = DIL_BLOCK
    ngrp = len(DIL_PATTERN)
    for gi, (q_ref, k_ref, v_ref) in enumerate(((q0, k0, v0), (q1, k1, v1), (q2, k2, v2))):
        q = q_ref[0].astype(F32)
        k = k_ref[0].astype(F32)
        q = q * lax.rsqrt(jnp.mean(q * q, axis=-1, keepdims=True) + EPS) * qg_ref[...]
        k = k * lax.rsqrt(jnp.mean(k * k, axis=-1, keepdims=True) + EPS) * kg_ref[...]
        qs[gi] = q * (DIL_HD ** -0.5)
        ks[gi] = k
        vs[gi] = v_ref[0].astype(F32)

    qi = lax.broadcasted_iota(jnp.int32, (blk, blk), 0)
    kj = lax.broadcasted_iota(jnp.int32, (blk, blk), 1)
    cur_ok = kj <= qi
    prev_ok = kj >= qi

    items = []
    for gi, (win, dil) in enumerate(DIL_PATTERN):
        assert win // dil == blk and (s // dil) % blk == 0
        for r in range(dil):
            for n in range((s // dil) // blk):
                items.append((gi, dil, r, n))

    def rows(dil, r, m):
        if dil == 1:
            return pl.ds(m * blk, blk)
        return pl.ds(m * blk * dil + r, blk, stride=dil)

    for b0 in range(0, len(items), DIL_BATCH):
        batch = items[b0:b0 + DIL_BATCH]
        qb = [qs[gi, rows(dil, r, n), :] for gi, dil, r, n in batch]
        kc = [ks[gi, rows(dil, r, n), :] for gi, dil, r, n in batch]
        kp = [ks[gi, rows(dil, r, n - 1), :] if n > 0 else None for gi, dil, r, n in batch]
        s_cur = [jnp.where(cur_ok, _mm_nt(q, k), -jnp.inf) for q, k in zip(qb, kc)]
        s_prev = [None if k is None else jnp.where(prev_ok, _mm_nt(q, k), -jnp.inf) for q, k in zip(qb, kp)]
        m = [jnp.max(a, axis=-1, keepdims=True) if b is None else
             jnp.maximum(jnp.max(a, axis=-1, keepdims=True), jnp.max(b, axis=-1, keepdims=True))
             for a, b in zip(s_cur, s_prev)]
        p_cur = [jnp.exp(a - mm) for a, mm in zip(s_cur, m)]
        p_prev = [None if b is None else jnp.exp(b - mm) for b, mm in zip(s_prev, m)]
        den = [jnp.sum(a, axis=-1, keepdims=True) if b is None else
               jnp.sum(a, axis=-1, keepdims=True) + jnp.sum(b, axis=-1, keepdims=True)
               for a, b in zip(p_cur, p_prev)]
        o = [_mm(p, vs[gi, rows(dil, r, n), :]) for p, (gi, dil, r, n) in zip(p_cur, batch)]
        o = [a if p is None else a + _mm(p, vs[gi, rows(dil, r, n - 1), :])
             for a, p, (gi, dil, r, n) in zip(o, p_prev, batch)]
        for (gi, dil, r, n), a, d, mm in zip(batch, o, den, m):
            os_[gi, rows(dil, r, n), :] = a / d
            ls[gi, rows(dil, r, n), :] = jnp.broadcast_to(mm + jnp.log(d), (blk, DIL_HD))

    lse = [ls[gi] for gi in range(ngrp)]
    mx = functools.reduce(jnp.maximum, lse)
    ex = [jnp.exp(l - mx) for l in lse]
    tot = functools.reduce(lambda a, b: a + b, ex)
    y = functools.reduce(lambda a, b: a + b, [ex[gi] / tot * os_[gi] for gi in range(ngrp)])
    o_ref[0] = y.astype(o_ref.dtype)


def _dil(proj, q_norm_g, k_norm_g):
    bsz, s, _ = proj.shape
    ngrp = len(DIL_PATTERN)

    def specs(col0):
        blk0 = col0 // DIL_HD
        return [pl.BlockSpec((1, s, DIL_HD), lambda b, h, g=g: (b, 0, blk0 + g * DIL_HEADS_PER_GROUP + h))
                for g in range(ngrp)]

    gain = pl.BlockSpec((1, DIL_HD), lambda b, h: (0, 0))
    return pl.pallas_call(
        _dil_kernel,
        grid=(bsz, DIL_HEADS_PER_GROUP),
        in_specs=specs(COL_DQ) + specs(COL_DK) + specs(COL_DV) + [gain, gain],
        out_specs=pl.BlockSpec((1, s, DIL_HD), lambda b, h: (b, 0, h)),
        out_shape=jax.ShapeDtypeStruct((bsz, s, DIL_OUT_W), BF16),
        scratch_shapes=[pltpu.VMEM((ngrp, s, DIL_HD), F32) for _ in range(5)],
        compiler_params=_params("parallel", "parallel"),
        name="dilattn",
    )(*([proj] * 9), q_norm_g.reshape(1, DIL_HD), k_norm_g.reshape(1, DIL_HD))


def _route(logits, bias):
    e, tm = logits.shape
    per = e // N_GROUPS
    scores = _sigmoid(logits)
    sel = scores + bias
    neg = -jnp.inf
    sub = lax.broadcasted_iota(jnp.int32, (per, tm), 0)
    gs_rows = []
    for g in range(N_GROUPS):
        blk = sel[g * per:(g + 1) * per, :]
        m1 = jnp.max(blk, axis=0, keepdims=True)
        i1 = jnp.min(jnp.where(blk == m1, sub, per), axis=0, keepdims=True)
        m2 = jnp.max(jnp.where(sub == i1, neg, blk), axis=0, keepdims=True)
        gs_rows.append(m1 + m2)
    gs = jnp.concatenate(gs_rows, axis=0)
    gi = lax.broadcasted_iota(jnp.int32, (N_GROUPS, tm), 0)
    gsel = jnp.zeros((N_GROUPS, tm), F32)
    for _ in range(TOPK_GROUPS):
        m = jnp.max(gs, axis=0, keepdims=True)
        idx = jnp.min(jnp.where(gs == m, gi, N_GROUPS), axis=0, keepdims=True)
        hit = gi == idx
        gsel = jnp.where(hit, 1.0, gsel)
        gs = jnp.where(hit, neg, gs)
    cand = jnp.concatenate(
        [jnp.where(gsel[g:g + 1, :] > 0.0, sel[g * per:(g + 1) * per, :], neg) for g in range(N_GROUPS)], axis=0)
    ei = lax.broadcasted_iota(jnp.int32, (e, tm), 0)
    chosen = jnp.zeros((e, tm), F32)
    for _ in range(TOP_K):
        m = jnp.max(cand, axis=0, keepdims=True)
        idx = jnp.min(jnp.where(cand == m, ei, e), axis=0, keepdims=True)
        hit = ei == idx
        chosen = jnp.where(hit, scores, chosen)
        cand = jnp.where(hit, neg, cand)
    return chosen / jnp.sum(chosen, axis=0, keepdims=True) * ROUTE_SCALE


def _merge_kernel(x_ref, mod_ref, ya_ref, yb_ref, ga_ref, gb_ref, wg_ref, wd_ref, wo_ref,
                  gf_ref, wr_ref, rb_ref, x1_ref, h2_ref, comb_ref, rank_ref, cnt_ref):
    gt1 = mod_ref[0, 2:3, :]
    sh2 = mod_ref[0, 3:4, :]
    sc2 = mod_ref[0, 4:5, :]
    ua = jnp.dot(ya_ref[0], wg_ref[...], preferred_element_type=F32)
    ub = jnp.dot(yb_ref[0], wd_ref[...], preferred_element_type=F32)
    merged = _sigmoid(ga_ref[0].astype(F32)) * ua + _sigmoid(gb_ref[0].astype(F32)) * ub
    x1 = x_ref[0] + gt1 * _mm(merged, wo_ref[...])
    x1_ref[0] = x1
    h2 = x1 * lax.rsqrt(jnp.mean(x1 * x1, axis=-1, keepdims=True) + EPS) * gf_ref[...] * (1.0 + sc2) + sh2
    h2_ref[0] = h2.astype(BF16)
    logits = lax.dot_general(wr_ref[...], h2, (((1,), (1,)), ((), ())),
                             preferred_element_type=F32, precision=lax.Precision.HIGHEST)
    comb = _route(logits, rb_ref[...])
    comb_ref[0] = comb
    tb = MOE_TB
    before = (lax.broadcasted_iota(jnp.int32, (tb, tb), 0) < lax.broadcasted_iota(jnp.int32, (tb, tb), 1))
    before = before.astype(BF16)
    for j in range(comb.shape[1] // tb):
        picked = (comb[:, j * tb:(j + 1) * tb] > 0.0).astype(F32)
        rank_ref[0, :, j * tb:(j + 1) * tb] = jnp.dot(picked.astype(BF16), before, preferred_element_type=F32)
        cnt_ref[0, j] = jnp.broadcast_to(jnp.sum(picked, axis=1, keepdims=True), (N_EXPERTS, LANES))


def _merge(x, mod3, y_a, y_b, proj, w_up_gdn, w_up_dil, w_out, g_ffn, w_router_t, router_bias):
    bsz, s, d = x.shape
    tm = 512
    row = lambda w: pl.BlockSpec((1, tm, w), lambda b, i: (b, i, 0))
    full = lambda a: pl.BlockSpec(a.shape, lambda b, i: (0,) * a.ndim)
    ex = pl.BlockSpec((1, N_EXPERTS, tm), lambda b, i: (b, 0, i))
    return pl.pallas_call(
        _merge_kernel,
        grid=(bsz, s // tm),
        in_specs=[row(d),
                  pl.BlockSpec((1, 6, d), lambda b, i: (b, 0, 0)),
                  row(GDN_W), row(DIL_OUT_W),
                  pl.BlockSpec((1, tm, d), lambda b, i: (b, i, COL_GA // d)),
                  pl.BlockSpec((1, tm, d), lambda b, i: (b, i, COL_GB // d)),
                  full(w_up_gdn), full(w_up_dil), full(w_out), full(g_ffn), full(w_router_t),
                  full(router_bias)],
        out_specs=[row(d), row(d), ex, ex,
                   pl.BlockSpec((1, tm // MOE_TB, N_EXPERTS, LANES), lambda b, i: (b, i, 0, 0))],
        out_shape=[jax.ShapeDtypeStruct((bsz, s, d), F32),
                   jax.ShapeDtypeStruct((bsz, s, d), BF16),
                   jax.ShapeDtypeStruct((bsz, N_EXPERTS, s), F32),
                   jax.ShapeDtypeStruct((bsz, N_EXPERTS, s), F32),
                   jax.ShapeDtypeStruct((bsz, s // MOE_TB, N_EXPERTS, LANES), F32)],
        compiler_params=_params("parallel", "parallel"),
        name="merge_router",
    )(x, mod3, y_a, y_b, proj, proj, w_up_gdn, w_up_dil, w_out, g_ffn, w_router_t, router_bias)


MOE_TB = 256
MOE_CH = 16
MOE_RB = MOE_TB * TOP_K + N_EXPERTS * MOE_CH
MOE_CPB = MOE_RB // MOE_CH
MOE_TM = 512
MOE_FT = 1024
MOE_CPT = MOE_FT // MOE_CH


def _moe_plan(cnt):
    nb, ne = cnt.shape
    i32 = jnp.int32
    nch = (cnt + MOE_CH - 1) // MOE_CH
    seg0 = jnp.cumsum(nch, axis=1) - nch
    pad = jnp.zeros((nb, LANES - ne), i32)
    meta = jnp.stack([jnp.concatenate([seg0 * MOE_CH, pad + MOE_RB], axis=1),
                      jnp.concatenate([nch * MOE_CH, pad], axis=1)], axis=1).astype(F32)
    meta = jnp.pad(meta, ((0, 0), (0, 6), (0, 0)))
    nch_e = nch.T
    cum_e = jnp.cumsum(nch_e, axis=1)
    tot_e = cum_e[:, -1]
    tiles_e = (tot_e + MOE_CPT - 1) // MOE_CPT
    tile_end = jnp.cumsum(tiles_e)
    ntiles = tile_end[-1]
    max_tiles = (nb * MOE_CPB) // MOE_CPT + ne
    i = jnp.arange(max_tiles, dtype=i32)
    te = jnp.sum((tile_end[None, :] <= i[:, None]).astype(i32), axis=1)
    te = jnp.minimum(te, jnp.sum((tile_end <= ntiles - 1).astype(i32)))
    te = jnp.minimum(te, ne - 1)
    oh_te = (te[:, None] == jnp.arange(ne, dtype=i32)[None, :]).astype(i32)
    pick = lambda tab: jnp.sum(oh_te[:, :, None] * tab[None, :, :], axis=1)
    tile_off_t = jnp.sum(oh_te * (tile_end - tiles_e)[None, :], axis=1)
    tot_t = jnp.sum(oh_te * tot_e[None, :], axis=1)
    cum_t, nch_t, seg0_t = pick(cum_e), pick(nch_e), pick(seg0.T)
    q = (i - tile_off_t)[:, None] * MOE_CPT + jnp.arange(MOE_CPT, dtype=i32)[None, :]
    valid = (q < tot_t[:, None]) & (i < ntiles)[:, None]
    blk = jnp.minimum(jnp.sum((cum_t[:, None, :] <= q[:, :, None]).astype(i32), axis=-1), nb - 1)
    oh_b = (blk[:, :, None] == jnp.arange(nb, dtype=i32)[None, None, :]).astype(i32)
    before = jnp.sum(oh_b * (cum_t - nch_t)[:, None, :], axis=-1)
    src = blk * MOE_CPB + jnp.sum(oh_b * seg0_t[:, None, :], axis=-1) + (q - before)
    spare = nb * MOE_CPB + (i % 2)[:, None] * MOE_CPT + jnp.arange(MOE_CPT, dtype=i32)[None, :]
    tbl_in = jnp.where(valid, src, nb * MOE_CPB + 2 * MOE_CPT).astype(i32).reshape(-1)
    tbl_out = jnp.where(valid, src, spare).astype(i32).reshape(-1)
    return meta, te.astype(i32), tbl_in, tbl_out, ntiles.astype(i32).reshape(1)


def _onehot_rows(meta_ref, rhs, r0, nrows, weighted):
    start = meta_ref[0, 0:1, :]
    plen = meta_ref[0, 1:2, :]
    r = (lax.broadcasted_iota(jnp.int32, (nrows, LANES), 0) + r0).astype(F32)
    owner = jnp.where((r >= start) & (r < start + plen), 1.0, 0.0)
    pos = r[:, :1] - jnp.sum(owner * start, axis=1, keepdims=True)
    got = jnp.dot(owner[:, :N_EXPERTS].astype(BF16), rhs, preferred_element_type=F32)
    tb = MOE_TB
    hit = (got[:, :tb] == pos) & (got[:, tb:2 * tb] > 0.5)
    return jnp.where(hit, got[:, 2 * tb:] if weighted else 1.0, 0.0).astype(BF16)


def _route_rhs(rank_ref, comb_ref):
    comb = comb_ref[0]
    picked = jnp.where(comb > 0.0, 1.0, 0.0)
    return jnp.concatenate([rank_ref[0], picked, comb], axis=1).astype(BF16)


def _dispatch_kernel(meta_ref, h_ref, rank_ref, comb_ref, x_ref):
    @pl.when(pl.program_id(0) < pl.num_programs(0) - 1)
    def _():
        rhs = _route_rhs(rank_ref, comb_ref)
        h = h_ref[0]
        for r0 in range(0, MOE_RB, MOE_TM):
            n = min(MOE_TM, MOE_RB - r0)
            p = _onehot_rows(meta_ref, rhs, r0, n, weighted=False)
            x_ref[0, r0:r0 + n, :] = jnp.dot(p, h, preferred_element_type=F32).astype(BF16)

    @pl.when(pl.program_id(0) == pl.num_programs(0) - 1)
    def _():
        x_ref[...] = jnp.zeros_like(x_ref)


def _dispatch(h2, rank_t, comb_t, meta):
    bsz, s, d = h2.shape
    nsb = s // MOE_TB
    nblk = bsz * nsb
    tok = lambda i: (jnp.minimum(i, nblk - 1) // nsb, jnp.minimum(i, nblk - 1) % nsb)
    ex = pl.BlockSpec((1, N_EXPERTS, MOE_TB), lambda i: (tok(i)[0], 0, tok(i)[1]))
    return pl.pallas_call(
        _dispatch_kernel,
        grid=(nblk + 1,),
        in_specs=[pl.BlockSpec((1, 8, LANES), lambda i: (jnp.minimum(i, nblk - 1), 0, 0)),
                  pl.BlockSpec((1, MOE_TB, d), lambda i: (tok(i)[0], tok(i)[1], 0)),
                  ex, ex],
        out_specs=pl.BlockSpec((1, MOE_RB, d), lambda i: (i, 0, 0)),
        out_shape=jax.ShapeDtypeStruct((nblk + 1, MOE_RB, d), BF16),
        compiler_params=_params("parallel"),
        name="moe_dispatch",
    )(meta, h2, rank_t, comb_t)


def _ffn_kernel(te_ref, tin_ref, tout_ref, nt_ref, x_hbm, wg_ref, wu_ref, wd_ref, y_hbm,
                xbuf, ybuf, wgb, wub, wdb, sem_in, sem_out):
    i = pl.program_id(0)
    nt = nt_ref[0]
    slot = i % 2

    def copy_in(t, s, j):
        return pltpu.make_async_copy(x_hbm.at[tin_ref[t * MOE_CPT + j]],
                                     xbuf.at[s, pl.ds(j * MOE_CH, MOE_CH)], sem_in.at[s])

    def copy_out(t, s, j):
        return pltpu.make_async_copy(ybuf.at[s, pl.ds(j * MOE_CH, MOE_CH)],
                                     y_hbm.at[tout_ref[t * MOE_CPT + j]], sem_out.at[s])

    @pl.when(i == 0)
    def _():
        for j in range(MOE_CPT):
            copy_in(0, 0, j).start()

    @pl.when(i + 1 < nt)
    def _():
        for j in range(MOE_CPT):
            copy_in(i + 1, 1 - slot, j).start()

    @pl.when(i < nt)
    def _():
        for j in range(MOE_CPT):
            copy_in(i, slot, j).wait()

        @pl.when(i >= 2)
        def _():
            for j in range(MOE_CPT):
                copy_out(i - 2, slot, j).wait()

        @pl.when(jnp.logical_or(i == 0, te_ref[i] != te_ref[jnp.maximum(i - 1, 0)]))
        def _():
            wgb[...] = wg_ref[0].astype(BF16)
            wub[...] = wu_ref[0].astype(BF16)
            wdb[...] = wd_ref[0].astype(BF16)

        x = xbuf[slot]
        a = jnp.dot(x, wgb[...], preferred_element_type=F32)
        u = jnp.dot(x, wub[...], preferred_element_type=F32)
        ybuf[slot] = jnp.dot((_silu(a) * u).astype(BF16), wdb[...], preferred_element_type=F32).astype(BF16)
        for j in range(MOE_CPT):
            copy_out(i, slot, j).start()

        @pl.when(i == nt - 1)
        def _():
            @pl.when(i >= 1)
            def _():
                for j in range(MOE_CPT):
                    copy_out(i - 1, 1 - slot, j).wait()
            for j in range(MOE_CPT):
                copy_out(i, slot, j).wait()


def _ffn(x_rows, w_eg, w_eu, w_ed, te, tbl_in, tbl_out, ntiles):
    nblk, rb, d = x_rows.shape
    ne, _, de = w_eg.shape
    xc = x_rows.reshape(nblk * MOE_CPB, MOE_CH, d)
    max_tiles = te.shape[0]
    y = pl.pallas_call(
        _ffn_kernel,
        grid_spec=pltpu.PrefetchScalarGridSpec(
            num_scalar_prefetch=4,
            grid=(max_tiles,),
            in_specs=[pl.BlockSpec(memory_space=pl.ANY),
                      pl.BlockSpec((1, d, de), lambda i, te, *_: (te[i], 0, 0)),
                      pl.BlockSpec((1, d, de), lambda i, te, *_: (te[i], 0, 0)),
                      pl.BlockSpec((1, de, d), lambda i, te, *_: (te[i], 0, 0))],
            out_specs=pl.BlockSpec(memory_space=pl.ANY),
            scratch_shapes=[pltpu.VMEM((2, MOE_FT, d), BF16), pltpu.VMEM((2, MOE_FT, d), BF16),
                            pltpu.VMEM((d, de), BF16), pltpu.VMEM((d, de), BF16), pltpu.VMEM((de, d), BF16),
                            pltpu.SemaphoreType.DMA((2,)), pltpu.SemaphoreType.DMA((2,))]),
        out_shape=jax.ShapeDtypeStruct(xc.shape, BF16),
        input_output_aliases={4: 0},
        compiler_params=_params("arbitrary"),
        name="moe_ffn",
    )(te, tbl_in, tbl_out, ntiles, xc, w_eg, w_eu, w_ed)
    return y.reshape(nblk, rb, d)


def _combine_kernel(meta_ref, y_ref, rank_ref, comb_ref, h_ref, x1_ref, mod_ref,
                    sg_ref, su_ref, sd_ref, o_ref):
    rhs = _route_rhs(rank_ref, comb_ref)
    h = h_ref[0]
    a = jnp.dot(h, sg_ref[...], preferred_element_type=F32)
    u = jnp.dot(h, su_ref[...], preferred_element_type=F32)
    acc = _mm(_silu(a) * u, sd_ref[...])
    for r0 in range(0, MOE_RB, MOE_TM):
        n = min(MOE_TM, MOE_RB - r0)
        p = _onehot_rows(meta_ref, rhs, r0, n, weighted=True)
        acc = acc + lax.dot_general(p, y_ref[0, r0:r0 + n, :], (((0,), (0,)), ((), ())),
                                    preferred_element_type=F32)
    o_ref[0] = x1_ref[0] + mod_ref[0, 5:6, :] * acc


def _combine(y_rows, rank_t, comb_t, h2, x1, mod3, sg, su, sd, meta):
    bsz, s, d = x1.shape
    nsb = s // MOE_TB
    tok = lambda w: pl.BlockSpec((1, MOE_TB, w), lambda b, i: (b, i, 0))
    ex = pl.BlockSpec((1, N_EXPERTS, MOE_TB), lambda b, i: (b, 0, i))
    full = lambda a: pl.BlockSpec(a.shape, lambda b, i: (0,) * a.ndim)
    return pl.pallas_call(
        _combine_kernel,
        grid=(bsz, nsb),
        in_specs=[pl.BlockSpec((1, 8, LANES), lambda b, i: (b * nsb + i, 0, 0)),
                  pl.BlockSpec((1, MOE_RB, d), lambda b, i: (b * nsb + i, 0, 0)),
                  ex, ex, tok(d), tok(d),
                  pl.BlockSpec((1, 6, d), lambda b, i: (b, 0, 0)),
                  full(sg), full(su), full(sd)],
        out_specs=tok(d),
        out_shape=jax.ShapeDtypeStruct((bsz, s, d), F32),
        compiler_params=_params("parallel", "parallel"),
        name="moe_combine",
    )(meta, y_rows, rank_t, comb_t, h2, x1, mod3, sg, su, sd)


def _moe(h2, comb_t, rank_t, cnt, x1, mod3, w_eg, w_eu, w_ed, sg, su, sd):
    meta, te, tbl_in, tbl_out, ntiles = _moe_plan(cnt)
    x_rows = _dispatch(h2, rank_t, comb_t, meta)
    y_rows = _ffn(x_rows, w_eg, w_eu, w_ed, te, tbl_in, tbl_out, ntiles)
    return _combine(y_rows, rank_t, comb_t, h2, x1, mod3, sg, su, sd, meta)


def _layer(x, cmod, g_mix, w_in, conv_w, a_log, dt_bias, norm_g, qn_g, kn_g, w_up_gdn, w_up_dil, w_out,
           g_ffn, w_router, router_bias, w_eg, w_eu, w_ed, w_sg, w_su, w_sd):
    bsz, s, d = x.shape
    mod3 = cmod.reshape(bsz, 6, d)
    o_ba = 4 * GDN_W
    o_dq = o_ba + 2 * GDN_HEADS
    o_ga = o_dq + 3 * DIL_W
    w_main = jnp.concatenate([w_in[:, :o_ba], w_in[:, o_ga:], w_in[:, o_dq:o_ga]], axis=1).astype(BF16)
    pad_hi = LANES - 2 * GDN_HEADS
    w_ba = jnp.pad(w_in[:, o_ba:o_dq], ((0, 0), (0, pad_hi))).astype(BF16)
    alog_vec = jnp.pad(a_log, (GDN_HEADS, pad_hi)).reshape(1, LANES)
    dtb_vec = jnp.pad(dt_bias, (GDN_HEADS, pad_hi)).reshape(1, LANES)
    proj, gates = _inproj(x, mod3, g_mix, w_main, w_ba, alog_vec, dtb_vec)

    conv_w3 = conv_w.reshape(GDN_CONV, 3, GDN_W).transpose(1, 0, 2)
    y_a = _gdn(proj, gates, conv_w3, norm_g)
    y_b = _dil(proj, qn_g, kn_g)

    x1, h2, comb_t, rank_t, cnt = _merge(x, mod3, y_a, y_b, proj, w_up_gdn.astype(BF16),
                                         w_up_dil.astype(BF16), w_out.astype(BF16), g_ffn.reshape(1, d),
                                         w_router.T, router_bias.reshape(N_EXPERTS, 1))
    cnt = cnt[..., 0].astype(jnp.int32).reshape(-1, N_EXPERTS)
    return _moe(h2, comb_t, rank_t, cnt, x1, mod3, w_eg, w_eu, w_ed,
                w_sg.astype(BF16), w_su.astype(BF16), w_sd.astype(BF16))


def kernel(x, c, w_ada, b_ada, g_mix, w_in, gdn_conv_w, gdn_a_log, gdn_dt_bias, gdn_norm_g, dil_q_norm_g, dil_k_norm_g, w_up_gdn, w_up_dil, w_out, g_ffn, w_router, router_bias, w_exp_gate, w_exp_up, w_exp_down, w_sh_gate, w_sh_up, w_sh_down):
    for l in range(w_ada.shape[0]):
        cmod = _ada(c, w_ada[l], b_ada[l])
        x = _layer(x, cmod, g_mix[l], w_in[l], gdn_conv_w[l], gdn_a_log[l], gdn_dt_bias[l], gdn_norm_g[l],
                   dil_q_norm_g[l], dil_k_norm_g[l], w_up_gdn[l], w_up_dil[l], w_out[l], g_ffn[l],
                   w_router[l], router_bias[l], w_exp_gate[l], w_exp_up[l], w_exp_down[l],
                   w_sh_gate[l], w_sh_up[l], w_sh_down[l])
    return x
```

```python
import functools

import jax
import jax.numpy as jnp
from jax import lax
from jax.experimental import pallas as pl
from jax.experimental.pallas import tpu as pltpu

F32 = jnp.float32
BF16 = jnp.bfloat16

D_MODEL = 1024
GDN_HEADS = 8
GDN_DK = 128
GDN_CONV = 4
DIL_PATTERN = ((128, 1), (512, 4), (2048, 16))
DIL_HEADS_PER_GROUP = 4
DIL_HD = 128
DIL_BLOCK = 128
DIL_BATCH = 4
N_EXPERTS = 64
TOP_K = 8
N_GROUPS = 8
TOPK_GROUPS = 4
D_EXPERT = 256
ROUTE_SCALE = 2.5
EPS = 1e-6

GDN_W = GDN_HEADS * GDN_DK
DIL_HEADS = len(DIL_PATTERN) * DIL_HEADS_PER_GROUP
DIL_W = DIL_HEADS * DIL_HD
DIL_OUT_W = DIL_HEADS_PER_GROUP * DIL_HD
COL_GQ, COL_GK, COL_GV, COL_GZ = 0, GDN_W, 2 * GDN_W, 3 * GDN_W
COL_GA = 4 * GDN_W
COL_GB = COL_GA + D_MODEL
COL_DQ = COL_GB + D_MODEL
COL_DK = COL_DQ + DIL_W
COL_DV = COL_DK + DIL_W
PROJ_W = COL_DV + DIL_W
LANES = 128

VMEM_LIMIT = 56 * 1024 * 1024


def _params(*sem):
    return pltpu.CompilerParams(dimension_semantics=sem, vmem_limit_bytes=VMEM_LIMIT)


def _sigmoid(x):
    return 1.0 / (1.0 + jnp.exp(-x))


def _silu(x):
    return x * _sigmoid(x)


def _softplus(x):
    return jnp.maximum(x, 0.0) + jnp.log(1.0 + jnp.exp(-jnp.abs(x)))


def _mm(a, b):
    return jnp.dot(a.astype(BF16), b.astype(BF16), preferred_element_type=F32)


def _mm_nt(a, b):
    return lax.dot_general(a.astype(BF16), b.astype(BF16), (((1,), (1,)), ((), ())),
                           preferred_element_type=F32)


def _mm_tn(a, b):
    return lax.dot_general(a.astype(BF16), b.astype(BF16), (((0,), (0,)), ((), ())),
                           preferred_element_type=F32)


def _mm_f32(a, b):
    return jnp.dot(a, b, preferred_element_type=F32, precision=lax.Precision.HIGHEST)


def _ada_kernel(c_ref, w_ref, b_ref, o_ref):
    cs = _silu(c_ref[...])
    o_ref[...] = _mm_f32(cs, w_ref[...]) + b_ref[...]


def _ada(c, w_ada, b_ada):
    bsz, d = c.shape
    n = w_ada.shape[1]
    tn = 1536
    return pl.pallas_call(
        _ada_kernel,
        grid=(n // tn,),
        in_specs=[pl.BlockSpec((bsz, d), lambda j: (0, 0)),
                  pl.BlockSpec((d, tn), lambda j: (0, j)),
                  pl.BlockSpec((1, tn), lambda j: (0, j))],
        out_specs=pl.BlockSpec((bsz, tn), lambda j: (0, j)),
        out_shape=jax.ShapeDtypeStruct((bsz, n), F32),
        compiler_params=_params("parallel"),
        name="ada",
    )(c, w_ada, b_ada.reshape(1, n))


def _inproj_kernel(x_ref, mod_ref, g_ref, w_ref, wba_ref, alog_ref, dtb_ref,
                   proj_ref, gates_ref, h_scr):
    j = pl.program_id(2)

    @pl.when(j == 0)
    def _():
        x = x_ref[0]
        sh1 = mod_ref[0, 0:1, :]
        sc1 = mod_ref[0, 1:2, :]
        ms = jnp.mean(x * x, axis=-1, keepdims=True)
        h = x * lax.rsqrt(ms + EPS) * g_ref[...] * (1.0 + sc1) + sh1
        hb = h.astype(BF16)
        h_scr[...] = hb
        ba = jnp.dot(hb, wba_ref[...], preferred_element_type=F32)
        lane = lax.broadcasted_iota(jnp.int32, ba.shape, 1)
        beta = _sigmoid(ba)
        g = -jnp.exp(alog_ref[...]) * _softplus(ba + dtb_ref[...])
        gates_ref[0] = jnp.where(lane < GDN_HEADS, beta, g)

    proj_ref[0] = jnp.dot(h_scr[...], w_ref[...], preferred_element_type=F32).astype(BF16)


def _inproj(x, mod3, g_mix, w_main, w_ba, alog_vec, dtb_vec):
    bsz, s, d = x.shape
    tm, tn = 1024, 1536
    return pl.pallas_call(
        _inproj_kernel,
        grid=(bsz, s // tm, PROJ_W // tn),
        in_specs=[pl.BlockSpec((1, tm, d), lambda b, i, j: (b, i, 0)),
                  pl.BlockSpec((1, 6, d), lambda b, i, j: (b, 0, 0)),
                  pl.BlockSpec((1, d), lambda b, i, j: (0, 0)),
                  pl.BlockSpec((d, tn), lambda b, i, j: (0, j)),
                  pl.BlockSpec((d, LANES), lambda b, i, j: (0, 0)),
                  pl.BlockSpec((1, LANES), lambda b, i, j: (0, 0)),
                  pl.BlockSpec((1, LANES), lambda b, i, j: (0, 0))],
        out_specs=[pl.BlockSpec((1, tm, tn), lambda b, i, j: (b, i, j)),
                   pl.BlockSpec((1, tm, LANES), lambda b, i, j: (b, i, 0))],
        out_shape=[jax.ShapeDtypeStruct((bsz, s, PROJ_W), BF16),
                   jax.ShapeDtypeStruct((bsz, s, LANES), F32)],
        scratch_shapes=[pltpu.VMEM((tm, d), BF16)],
        compiler_params=_params("parallel", "parallel", "arbitrary"),
        name="inproj",
    )(x, mod3, g_mix.reshape(1, d), w_main, w_ba, alog_vec, dtb_vec)


GDN_C = 128
GDN_TAIL = 8


def _cumsum_rows(g):
    row = lax.broadcasted_iota(jnp.int32, g.shape, 0)
    sft = 1
    while sft < g.shape[0]:
        g = g + jnp.where(row >= sft, pltpu.roll(g, sft, 0), 0.0)
        sft *= 2
    return g


def _unit_lower_inverses(mats):
    n = mats[0].shape[0]
    row = lax.broadcasted_iota(jnp.int32, (n, n), 0)
    col = lax.broadcasted_iota(jnp.int32, (n, n), 1)
    eye = (row == col).astype(F32)
    ts = None
    b = 1
    while b < n:
        off = ((row // (2 * b)) == (col // (2 * b))) & ((row // b) % 2 == 1) & ((col // b) % 2 == 0)
        if b == 1:
            ts = [eye - jnp.where(off, a, 0.0) for a in mats]
        else:
            tb = [t.astype(BF16) for t in ts]
            inner = [jnp.dot(jnp.where(off, a, 0.0).astype(BF16), t, preferred_element_type=F32)
                     for a, t in zip(mats, tb)]
            ts = [t - jnp.dot(t16, i.astype(BF16), preferred_element_type=F32)
                  for t, t16, i in zip(ts, tb, inner)]
        b *= 2
    return ts


def _gdn_kernel(q_ref, k_ref, v_ref, z_ref, gates_ref, cw_ref, ng_ref, o_ref,
                qprev, kprev, vprev, state):
    sblk = pl.program_id(1)
    c = GDN_C
    nh = GDN_HEADS
    w = nh * GDN_DK

    @pl.when(sblk == 0)
    def _():
        state[...] = jnp.zeros_like(state)
        for buf in (qprev, kprev, vprev):
            buf[...] = jnp.zeros_like(buf)

    taps = GDN_CONV - 1
    si = lax.broadcasted_iota(jnp.int32, (taps * c, c), 0)
    sm = lax.broadcasted_iota(jnp.int32, (taps * c, c), 1)
    shift_mat = jnp.where(sm == si % c - (taps - si // c), 1.0, 0.0).astype(BF16)
    trow = lax.broadcasted_iota(jnp.int32, (GDN_TAIL, w), 0)
    conv = []
    for idx, (ref, prev) in enumerate(((q_ref, qprev), (k_ref, kprev), (v_ref, vprev))):
        cur = ref[0]
        cur32 = cur.astype(F32)
        delayed = jnp.dot(shift_mat, cur, preferred_element_type=F32)
        acc = cur32 * cw_ref[idx, taps:taps + 1, :]
        head = jnp.zeros((GDN_TAIL, w), F32)
        tail = prev[...]
        for j in range(taps):
            acc = acc + delayed[j * c:(j + 1) * c, :] * cw_ref[idx, j:j + 1, :]
            head = head + jnp.where(trow < taps - j, pltpu.roll(tail, taps - j, 0), 0.0) * cw_ref[idx, j:j + 1, :]
        acc = jnp.concatenate([acc[:GDN_TAIL] + head, acc[GDN_TAIL:]], axis=0)
        prev[...] = cur32[c - GDN_TAIL:, :]
        conv.append(_silu(acc))
    qc, kc, vc = conv

    gates = gates_ref[0]
    gcum = _cumsum_rows(gates)
    row = lax.broadcasted_iota(jnp.int32, (c, c), 0)
    col = lax.broadcasted_iota(jnp.int32, (c, c), 1)
    eye = row == col
    incl = col <= row
    strict = col < row

    heads = range(nh)
    hsl = [slice(h * GDN_DK, (h + 1) * GDN_DK) for h in heads]
    q = [qc[:, s] for s in hsl]
    k = [kc[:, s] for s in hsl]
    v = [vc[:, s] for s in hsl]
    q = [x * lax.rsqrt(jnp.sum(x * x, axis=-1, keepdims=True) + EPS) * (GDN_DK ** -0.5) for x in q]
    k = [x * lax.rsqrt(jnp.sum(x * x, axis=-1, keepdims=True) + EPS) for x in k]
    beta = [jnp.broadcast_to(gates[:, h:h + 1], (c, GDN_DK)) for h in heads]
    gc = [jnp.broadcast_to(gcum[:, nh + h:nh + h + 1], (c, GDN_DK)) for h in heads]
    gc_row = [jnp.sum(jnp.where(eye, x, 0.0), axis=0, keepdims=True) for x in gc]
    decay = [jnp.exp(jnp.where(incl, x - y, -jnp.inf)) for x, y in zip(gc, gc_row)]
    egc = [jnp.exp(x) for x in gc]
    kb = [x * b for x, b in zip(k, beta)]
    scores = [_mm_nt(jnp.concatenate([a, b], axis=0), x) for a, b, x in zip(kb, q, k)]
    a_mat = [jnp.where(strict, s[:c] * d, 0.0) for s, d in zip(scores, decay)]
    qk = [s[c:] * d for s, d in zip(scores, decay)]
    t_inv = _unit_lower_inverses(a_mat)
    sol = [_mm(t, jnp.concatenate([x * b, y * e], axis=1))
           for t, x, b, y, e in zip(t_inv, v, beta, kb, egc)]

    st = [state[h] for h in heads]
    ws = [_mm(jnp.concatenate([s[:, GDN_DK:], x * e], axis=0), m)
          for s, x, e, m in zip(sol, q, egc, st)]
    v_new = [s[:, :GDN_DK] - x[:c] for s, x in zip(sol, ws)]
    o = [x[c:] + _mm(a, b) for x, a, b in zip(ws, qk, v_new)]
    gc_last = [x[c - 1:c, :] for x in gc]
    k_dec = [x * jnp.exp(l - g) for x, l, g in zip(k, gc_last, gc)]
    for h in heads:
        state[h] = st[h] * jnp.exp(gc_last[h]) + _mm_tn(k_dec[h], v_new[h])
        y = o[h] * lax.rsqrt(jnp.mean(o[h] * o[h], axis=-1, keepdims=True) + EPS) * ng_ref[...]
        o_ref[0, :, hsl[h]] = (y * _silu(z_ref[0, :, hsl[h]].astype(F32))).astype(o_ref.dtype)


def _gdn(proj, gates, conv_w3, norm_g):
    bsz, s, _ = proj.shape
    w = GDN_W

    def col_spec(col0):
        return pl.BlockSpec((1, GDN_C, w), lambda b, i: (b, i, col0 // w))

    return pl.pallas_call(
        _gdn_kernel,
        grid=(bsz, s // GDN_C),
        in_specs=[col_spec(COL_GQ), col_spec(COL_GK), col_spec(COL_GV), col_spec(COL_GZ),
                  pl.BlockSpec((1, GDN_C, LANES), lambda b, i: (b, i, 0)),
                  pl.BlockSpec((3, GDN_CONV, w), lambda b, i: (0, 0, 0)),
                  pl.BlockSpec((1, GDN_DK), lambda b, i: (0, 0))],
        out_specs=pl.BlockSpec((1, GDN_C, w), lambda b, i: (b, i, 0)),
        out_shape=jax.ShapeDtypeStruct((bsz, s, GDN_W), BF16),
        scratch_shapes=[pltpu.VMEM((GDN_TAIL, w), F32),
                        pltpu.VMEM((GDN_TAIL, w), F32),
                        pltpu.VMEM((GDN_TAIL, w), F32),
                        pltpu.VMEM((GDN_HEADS, GDN_DK, GDN_DK), F32)],
        compiler_params=_params("parallel", "arbitrary"),
        name="gdn",
    )(proj, proj, proj, proj, gates, conv_w3, norm_g.reshape(1, GDN_DK))


def _dil_kernel(q0, q1, q2, k0, k1, k2, v0, v1, v2, qg_ref, kg_ref, o_ref, qs, ks, vs, os_, ls):
    s = qs.shape[1]
    blk = DIL_BLOCK
    ngrp = len(DIL_PATTERN)
    for gi, (q_ref, k_ref, v_ref) in enumerate(((q0, k0, v0), (q1, k1, v1), (q2, k2, v2))):
        q = q_ref[0].astype(F32)
        k = k_ref[0].astype(F32)
        q = q * lax.rsqrt(jnp.mean(q * q, axis=-1, keepdims=True) + EPS) * qg_ref[...]
        k = k * lax.rsqrt(jnp.mean(k * k, axis=-1, keepdims=True) + EPS) * kg_ref[...]
        qs[gi] = q * (DIL_HD ** -0.5)
        ks[gi] = k
        vs[gi] = v_ref[0].astype(F32)

    qi = lax.broadcasted_iota(jnp.int32, (blk, blk), 0)
    kj = lax.broadcasted_iota(jnp.int32, (blk, blk), 1)
    cur_ok = kj <= qi
    prev_ok = kj >= qi

    items = []
    for gi, (win, dil) in enumerate(DIL_PATTERN):
        assert win // dil == blk and (s // dil) % blk == 0
        for r in range(dil):
            for n in range((s // dil) // blk):
                items.append((gi, dil, r, n))

    def rows(dil, r, m):
        if dil == 1:
            return pl.ds(m * blk, blk)
        return pl.ds(m * blk * dil + r, blk, stride=dil)

    for b0 in range(0, len(items), DIL_BATCH):
        batch = items[b0:b0 + DIL_BATCH]
        qb = [qs[gi, rows(dil, r, n), :] for gi, dil, r, n in batch]
        kc = [ks[gi, rows(dil, r, n), :] for gi, dil, r, n in batch]
        kp = [ks[gi, rows(dil, r, n - 1), :] if n > 0 else None for gi, dil, r, n in batch]
        s_cur = [jnp.where(cur_ok, _mm_nt(q, k), -jnp.inf) for q, k in zip(qb, kc)]
        s_prev = [None if k is None else jnp.where(prev_ok, _mm_nt(q, k), -jnp.inf) for q, k in zip(qb, kp)]
        m = [jnp.max(a, axis=-1, keepdims=True) if b is None else
             jnp.maximum(jnp.max(a, axis=-1, keepdims=True), jnp.max(b, axis=-1, keepdims=True))
             for a, b in zip(s_cur, s_prev)]
        p_cur = [jnp.exp(a - mm) for a, mm in zip(s_cur, m)]
        p_prev = [None if b is None else jnp.exp(b - mm) for b, mm in zip(s_prev, m)]
        den = [jnp.sum(a, axis=-1, keepdims=True) if b is None else
               jnp.sum(a, axis=-1, keepdims=True) + jnp.sum(b, axis=-1, keepdims=True)
               for a, b in zip(p_cur, p_prev)]
        o = [_mm(p, vs[gi, rows(dil, r, n), :]) for p, (gi, dil, r, n) in zip(p_cur, batch)]
        o = [a if p is None else a + _mm(p, vs[gi, rows(dil, r, n - 1), :])
             for a, p, (gi, dil, r, n) in zip(o, p_prev, batch)]
        for (gi, dil, r, n), a, d, mm in zip(batch, o, den, m):
            os_[gi, rows(dil, r, n), :] = a / d
            ls[gi, rows(dil, r, n), :] = jnp.broadcast_to(mm + jnp.log(d), (blk, DIL_HD))

    lse = [ls[gi] for gi in range(ngrp)]
    mx = functools.reduce(jnp.maximum, lse)
    ex = [jnp.exp(l - mx) for l in lse]
    tot = functools.reduce(lambda a, b: a + b, ex)
    y = functools.reduce(lambda a, b: a + b, [ex[gi] / tot * os_[gi] for gi in range(ngrp)])
    o_ref[0] = y.astype(o_ref.dtype)


def _dil(proj, q_norm_g, k_norm_g):
    bsz, s, _ = proj.shape
    ngrp = len(DIL_PATTERN)

    def specs(col0):
        blk0 = col0 // DIL_HD
        return [pl.BlockSpec((1, s, DIL_HD), lambda b, h, g=g: (b, 0, blk0 + g * DIL_HEADS_PER_GROUP + h))
                for g in range(ngrp)]

    gain = pl.BlockSpec((1, DIL_HD), lambda b, h: (0, 0))
    return pl.pallas_call(
        _dil_kernel,
        grid=(bsz, DIL_HEADS_PER_GROUP),
        in_specs=specs(COL_DQ) + specs(COL_DK) + specs(COL_DV) + [gain, gain],
        out_specs=pl.BlockSpec((1, s, DIL_HD), lambda b, h: (b, 0, h)),
        out_shape=jax.ShapeDtypeStruct((bsz, s, DIL_OUT_W), BF16),
        scratch_shapes=[pltpu.VMEM((ngrp, s, DIL_HD), F32) for _ in range(5)],
        compiler_params=_params("parallel", "parallel"),
        name="dilattn",
    )(*([proj] * 9), q_norm_g.reshape(1, DIL_HD), k_norm_g.reshape(1, DIL_HD))


def _route(logits, bias):
    e, tm = logits.shape
    per = e // N_GROUPS
    scores = _sigmoid(logits)
    sel = scores + bias
    neg = -jnp.inf
    sub = lax.broadcasted_iota(jnp.int32, (per, tm), 0)
    gs_rows = []
    for g in range(N_GROUPS):
        blk = sel[g * per:(g + 1) * per, :]
        m1 = jnp.max(blk, axis=0, keepdims=True)
        i1 = jnp.min(jnp.where(blk == m1, sub, per), axis=0, keepdims=True)
        m2 = jnp.max(jnp.where(sub == i1, neg, blk), axis=0, keepdims=True)
        gs_rows.append(m1 + m2)
    gs = jnp.concatenate(gs_rows, axis=0)
    gi = lax.broadcasted_iota(jnp.int32, (N_GROUPS, tm), 0)
    gsel = jnp.zeros((N_GROUPS, tm), F32)
    for _ in range(TOPK_GROUPS):
        m = jnp.max(gs, axis=0, keepdims=True)
        idx = jnp.min(jnp.where(gs == m, gi, N_GROUPS), axis=0, keepdims=True)
        hit = gi == idx
        gsel = jnp.where(hit, 1.0, gsel)
        gs = jnp.where(hit, neg, gs)
    cand = jnp.concatenate(
        [jnp.where(gsel[g:g + 1, :] > 0.0, sel[g * per:(g + 1) * per, :], neg) for g in range(N_GROUPS)], axis=0)
    ei = lax.broadcasted_iota(jnp.int32, (e, tm), 0)
    chosen = jnp.zeros((e, tm), F32)
    for _ in range(TOP_K):
        m = jnp.max(cand, axis=0, keepdims=True)
        idx = jnp.min(jnp.where(cand == m, ei, e), axis=0, keepdims=True)
        hit = ei == idx
        chosen = jnp.where(hit, scores, chosen)
        cand = jnp.where(hit, neg, cand)
    return chosen / jnp.sum(chosen, axis=0, keepdims=True) * ROUTE_SCALE


def _merge_kernel(x_ref, mod_ref, ya_ref, yb_ref, ga_ref, gb_ref, wg_ref, wd_ref, wo_ref,
                  gf_ref, wr_ref, rb_ref, x1_ref, h2_ref, comb_ref, rank_ref, cnt_ref):
    gt1 = mod_ref[0, 2:3, :]
    sh2 = mod_ref[0, 3:4, :]
    sc2 = mod_ref[0, 4:5, :]
    ua = jnp.dot(ya_ref[0], wg_ref[...], preferred_element_type=F32)
    ub = jnp.dot(yb_ref[0], wd_ref[...], preferred_element_type=F32)
    merged = _sigmoid(ga_ref[0].astype(F32)) * ua + _sigmoid(gb_ref[0].astype(F32)) * ub
    x1 = x_ref[0] + gt1 * _mm(merged, wo_ref[...])
    x1_ref[0] = x1
    h2 = x1 * lax.rsqrt(jnp.mean(x1 * x1, axis=-1, keepdims=True) + EPS) * gf_ref[...] * (1.0 + sc2) + sh2
    h2_hi = h2.astype(BF16)
    h2_ref[0] = h2_hi
    h2_lo = (h2 - h2_hi.astype(F32)).astype(BF16)
    nt_dims = (((1,), (1,)), ((), ()))
    both = lax.dot_general(wr_ref[...], h2_hi, nt_dims, preferred_element_type=F32)
    logits = (both[:N_EXPERTS] + both[N_EXPERTS:]
              + lax.dot_general(wr_ref[:N_EXPERTS, :], h2_lo, nt_dims, preferred_element_type=F32))
    comb = _route(logits, rb_ref[...])
    comb_ref[0] = comb
    tb = MOE_TB
    before = (lax.broadcasted_iota(jnp.int32, (tb, tb), 0) < lax.broadcasted_iota(jnp.int32, (tb, tb), 1))
    before = before.astype(BF16)
    for j in range(comb.shape[1] // tb):
        picked = (comb[:, j * tb:(j + 1) * tb] > 0.0).astype(F32)
        rank_ref[0, :, j * tb:(j + 1) * tb] = jnp.dot(picked.astype(BF16), before, preferred_element_type=F32)
        cnt_ref[0, j] = jnp.broadcast_to(jnp.sum(picked, axis=1, keepdims=True), (N_EXPERTS, LANES))


def _merge(x, mod3, y_a, y_b, proj, w_up_gdn, w_up_dil, w_out, g_ffn, w_router_t, router_bias):
    bsz, s, d = x.shape
    tm = 512
    row = lambda w: pl.BlockSpec((1, tm, w), lambda b, i: (b, i, 0))
    full = lambda a: pl.BlockSpec(a.shape, lambda b, i: (0,) * a.ndim)
    ex = pl.BlockSpec((1, N_EXPERTS, tm), lambda b, i: (b, 0, i))
    return pl.pallas_call(
        _merge_kernel,
        grid=(bsz, s // tm),
        in_specs=[row(d),
                  pl.BlockSpec((1, 6, d), lambda b, i: (b, 0, 0)),
                  row(GDN_W), row(DIL_OUT_W),
                  pl.BlockSpec((1, tm, d), lambda b, i: (b, i, COL_GA // d)),
                  pl.BlockSpec((1, tm, d), lambda b, i: (b, i, COL_GB // d)),
                  full(w_up_gdn), full(w_up_dil), full(w_out), full(g_ffn), full(w_router_t),
                  full(router_bias)],
        out_specs=[row(d), row(d), ex, ex,
                   pl.BlockSpec((1, tm // MOE_TB, N_EXPERTS, LANES), lambda b, i: (b, i, 0, 0))],
        out_shape=[jax.ShapeDtypeStruct((bsz, s, d), F32),
                   jax.ShapeDtypeStruct((bsz, s, d), BF16),
                   jax.ShapeDtypeStruct((bsz, N_EXPERTS, s), F32),
                   jax.ShapeDtypeStruct((bsz, N_EXPERTS, s), F32),
                   jax.ShapeDtypeStruct((bsz, s // MOE_TB, N_EXPERTS, LANES), F32)],
        compiler_params=_params("parallel", "parallel"),
        name="merge_router",
    )(x, mod3, y_a, y_b, proj, proj, w_up_gdn, w_up_dil, w_out, g_ffn, w_router_t, router_bias)


MOE_TB = 256
MOE_CH = 16
MOE_RB = MOE_TB * TOP_K + N_EXPERTS * MOE_CH
MOE_CPB = MOE_RB // MOE_CH
MOE_TM = 512
MOE_FT = 1024
MOE_CPT = MOE_FT // MOE_CH


def _moe_plan(cnt):
    nb, ne = cnt.shape
    i32 = jnp.int32
    nch = (cnt + MOE_CH - 1) // MOE_CH
    seg0 = jnp.cumsum(nch, axis=1) - nch
    pad = jnp.zeros((nb, LANES - ne), i32)
    meta = jnp.stack([jnp.concatenate([seg0 * MOE_CH, pad + MOE_RB], axis=1),
                      jnp.concatenate([nch * MOE_CH, pad], axis=1)], axis=1).astype(F32)
    meta = jnp.pad(meta, ((0, 0), (0, 6), (0, 0)))
    nch_e = nch.T
    cum_e = jnp.cumsum(nch_e, axis=1)
    tot_e = cum_e[:, -1]
    tiles_e = (tot_e + MOE_CPT - 1) // MOE_CPT
    tile_end = jnp.cumsum(tiles_e)
    ntiles = tile_end[-1]
    max_tiles = (nb * MOE_CPB) // MOE_CPT + ne
    i = jnp.arange(max_tiles, dtype=i32)
    te = jnp.sum((tile_end[None, :] <= i[:, None]).astype(i32), axis=1)
    te = jnp.minimum(te, jnp.sum((tile_end <= ntiles - 1).astype(i32)))
    te = jnp.minimum(te, ne - 1)
    oh_te = (te[:, None] == jnp.arange(ne, dtype=i32)[None, :]).astype(i32)
    pick = lambda tab: jnp.sum(oh_te[:, :, None] * tab[None, :, :], axis=1)
    tile_off_t = jnp.sum(oh_te * (tile_end - tiles_e)[None, :], axis=1)
    tot_t = jnp.sum(oh_te * tot_e[None, :], axis=1)
    cum_t, nch_t, seg0_t = pick(cum_e), pick(nch_e), pick(seg0.T)
    q = (i - tile_off_t)[:, None] * MOE_CPT + jnp.arange(MOE_CPT, dtype=i32)[None, :]
    valid = (q < tot_t[:, None]) & (i < ntiles)[:, None]
    blk = jnp.minimum(jnp.sum((cum_t[:, None, :] <= q[:, :, None]).astype(i32), axis=-1), nb - 1)
    oh_b = (blk[:, :, None] == jnp.arange(nb, dtype=i32)[None, None, :]).astype(i32)
    before = jnp.sum(oh_b * (cum_t - nch_t)[:, None, :], axis=-1)
    src = blk * MOE_CPB + jnp.sum(oh_b * seg0_t[:, None, :], axis=-1) + (q - before)
    spare = nb * MOE_CPB + (i % 2)[:, None] * MOE_CPT + jnp.arange(MOE_CPT, dtype=i32)[None, :]
    tbl_in = jnp.where(valid, src, nb * MOE_CPB + 2 * MOE_CPT).astype(i32).reshape(-1)
    tbl_out = jnp.where(valid, src, spare).astype(i32).reshape(-1)
    return meta, te.astype(i32), tbl_in, tbl_out, ntiles.astype(i32).reshape(1)


def _onehot_rows(meta_ref, rhs, r0, nrows, weighted):
    start = meta_ref[0, 0:1, :]
    plen = meta_ref[0, 1:2, :]
    r = (lax.broadcasted_iota(jnp.int32, (nrows, LANES), 0) + r0).astype(F32)
    owner = jnp.where((r >= start) & (r < start + plen), 1.0, 0.0)
    pos = r[:, :1] - jnp.sum(owner * start, axis=1, keepdims=True)
    got = jnp.dot(owner[:, :N_EXPERTS].astype(BF16), rhs, preferred_element_type=F32)
    tb = MOE_TB
    hit = (got[:, :tb] == pos) & (got[:, tb:2 * tb] > 0.5)
    return jnp.where(hit, got[:, 2 * tb:] if weighted else 1.0, 0.0).astype(BF16)


def _route_rhs(rank_ref, comb_ref):
    comb = comb_ref[0]
    picked = jnp.where(comb > 0.0, 1.0, 0.0)
    return jnp.concatenate([rank_ref[0], picked, comb], axis=1).astype(BF16)


def _dispatch_kernel(meta_ref, h_ref, rank_ref, comb_ref, x_ref):
    @pl.when(pl.program_id(0) < pl.num_programs(0) - 1)
    def _():
        rhs = _route_rhs(rank_ref, comb_ref)
        h = h_ref[0]
        for r0 in range(0, MOE_RB, MOE_TM):
            n = min(MOE_TM, MOE_RB - r0)
            p = _onehot_rows(meta_ref, rhs, r0, n, weighted=False)
            x_ref[0, r0:r0 + n, :] = jnp.dot(p, h, preferred_element_type=F32).astype(BF16)

    @pl.when(pl.program_id(0) == pl.num_programs(0) - 1)
    def _():
        x_ref[...] = jnp.zeros_like(x_ref)


def _dispatch(h2, rank_t, comb_t, meta):
    bsz, s, d = h2.shape
    nsb = s // MOE_TB
    nblk = bsz * nsb
    tok = lambda i: (jnp.minimum(i, nblk - 1) // nsb, jnp.minimum(i, nblk - 1) % nsb)
    ex = pl.BlockSpec((1, N_EXPERTS, MOE_TB), lambda i: (tok(i)[0], 0, tok(i)[1]))
    return pl.pallas_call(
        _dispatch_kernel,
        grid=(nblk + 1,),
        in_specs=[pl.BlockSpec((1, 8, LANES), lambda i: (jnp.minimum(i, nblk - 1), 0, 0)),
                  pl.BlockSpec((1, MOE_TB, d), lambda i: (tok(i)[0], tok(i)[1], 0)),
                  ex, ex],
        out_specs=pl.BlockSpec((1, MOE_RB, d), lambda i: (i, 0, 0)),
        out_shape=jax.ShapeDtypeStruct((nblk + 1, MOE_RB, d), BF16),
        compiler_params=_params("parallel"),
        name="moe_dispatch",
    )(meta, h2, rank_t, comb_t)


def _ffn_kernel(te_ref, tin_ref, tout_ref, nt_ref, x_hbm, wg_ref, wu_ref, wd_ref, y_hbm,
                xbuf, ybuf, wgb, wub, wdb, sem_in, sem_out):
    i = pl.program_id(0)
    nt = nt_ref[0]
    slot = i % 2

    def copy_in(t, s, j):
        return pltpu.make_async_copy(x_hbm.at[tin_ref[t * MOE_CPT + j]],
                                     xbuf.at[s, pl.ds(j * MOE_CH, MOE_CH)], sem_in.at[s])

    def copy_out(t, s, j):
        return pltpu.make_async_copy(ybuf.at[s, pl.ds(j * MOE_CH, MOE_CH)],
                                     y_hbm.at[tout_ref[t * MOE_CPT + j]], sem_out.at[s])

    @pl.when(i == 0)
    def _():
        for j in range(MOE_CPT):
            copy_in(0, 0, j).start()

    @pl.when(i + 1 < nt)
    def _():
        for j in range(MOE_CPT):
            copy_in(i + 1, 1 - slot, j).start()

    @pl.when(i < nt)
    def _():
        for j in range(MOE_CPT):
            copy_in(i, slot, j).wait()

        @pl.when(i >= 2)
        def _():
            for j in range(MOE_CPT):
                copy_out(i - 2, slot, j).wait()

        @pl.when(jnp.logical_or(i == 0, te_ref[i] != te_ref[jnp.maximum(i - 1, 0)]))
        def _():
            wgb[...] = wg_ref[0].astype(BF16)
            wub[...] = wu_ref[0].astype(BF16)
            wdb[...] = wd_ref[0].astype(BF16)

        x = xbuf[slot]
        a = jnp.dot(x, wgb[...], preferred_element_type=F32)
        u = jnp.dot(x, wub[...], preferred_element_type=F32)
        ybuf[slot] = jnp.dot((_silu(a) * u).astype(BF16), wdb[...], preferred_element_type=F32).astype(BF16)
        for j in range(MOE_CPT):
            copy_out(i, slot, j).start()

        @pl.when(i == nt - 1)
        def _():
            @pl.when(i >= 1)
            def _():
                for j in range(MOE_CPT):
                    copy_out(i - 1, 1 - slot, j).wait()
            for j in range(MOE_CPT):
                copy_out(i, slot, j).wait()


def _ffn(x_rows, w_eg, w_eu, w_ed, te, tbl_in, tbl_out, ntiles):
    nblk, rb, d = x_rows.shape
    ne, _, de = w_eg.shape
    xc = x_rows.reshape(nblk * MOE_CPB, MOE_CH, d)
    max_tiles = te.shape[0]
    y = pl.pallas_call(
        _ffn_kernel,
        grid_spec=pltpu.PrefetchScalarGridSpec(
            num_scalar_prefetch=4,
            grid=(max_tiles,),
            in_specs=[pl.BlockSpec(memory_space=pl.ANY),
                      pl.BlockSpec((1, d, de), lambda i, te, *_: (te[i], 0, 0)),
                      pl.BlockSpec((1, d, de), lambda i, te, *_: (te[i], 0, 0)),
                      pl.BlockSpec((1, de, d), lambda i, te, *_: (te[i], 0, 0))],
            out_specs=pl.BlockSpec(memory_space=pl.ANY),
            scratch_shapes=[pltpu.VMEM((2, MOE_FT, d), BF16), pltpu.VMEM((2, MOE_FT, d), BF16),
                            pltpu.VMEM((d, de), BF16), pltpu.VMEM((d, de), BF16), pltpu.VMEM((de, d), BF16),
                            pltpu.SemaphoreType.DMA((2,)), pltpu.SemaphoreType.DMA((2,))]),
        out_shape=jax.ShapeDtypeStruct(xc.shape, BF16),
        input_output_aliases={4: 0},
        compiler_params=_params("arbitrary"),
        name="moe_ffn",
    )(te, tbl_in, tbl_out, ntiles, xc, w_eg, w_eu, w_ed)
    return y.reshape(nblk, rb, d)


def _combine_kernel(meta_ref, y_ref, rank_ref, comb_ref, h_ref, x1_ref, mod_ref,
                    sg_ref, su_ref, sd_ref, o_ref):
    rhs = _route_rhs(rank_ref, comb_ref)
    h = h_ref[0]
    a = jnp.dot(h, sg_ref[...], preferred_element_type=F32)
    u = jnp.dot(h, su_ref[...], preferred_element_type=F32)
    acc = _mm(_silu(a) * u, sd_ref[...])
    for r0 in range(0, MOE_RB, MOE_TM):
        n = min(MOE_TM, MOE_RB - r0)
        p = _onehot_rows(meta_ref, rhs, r0, n, weighted=True)
        acc = acc + lax.dot_general(p, y_ref[0, r0:r0 + n, :], (((0,), (0,)), ((), ())),
                                    preferred_element_type=F32)
    o_ref[0] = x1_ref[0] + mod_ref[0, 5:6, :] * acc


def _combine(y_rows, rank_t, comb_t, h2, x1, mod3, sg, su, sd, meta):
    bsz, s, d = x1.shape
    nsb = s // MOE_TB
    tok = lambda w: pl.BlockSpec((1, MOE_TB, w), lambda b, i: (b, i, 0))
    ex = pl.BlockSpec((1, N_EXPERTS, MOE_TB), lambda b, i: (b, 0, i))
    full = lambda a: pl.BlockSpec(a.shape, lambda b, i: (0,) * a.ndim)
    return pl.pallas_call(
        _combine_kernel,
        grid=(bsz, nsb),
        in_specs=[pl.BlockSpec((1, 8, LANES), lambda b, i: (b * nsb + i, 0, 0)),
                  pl.BlockSpec((1, MOE_RB, d), lambda b, i: (b * nsb + i, 0, 0)),
                  ex, ex, tok(d), tok(d),
                  pl.BlockSpec((1, 6, d), lambda b, i: (b, 0, 0)),
                  full(sg), full(su), full(sd)],
        out_specs=tok(d),
        out_shape=jax.ShapeDtypeStruct((bsz, s, d), F32),
        compiler_params=_params("parallel", "parallel"),
        name="moe_combine",
    )(meta, y_rows, rank_t, comb_t, h2, x1, mod3, sg, su, sd)


def _moe(h2, comb_t, rank_t, cnt, x1, mod3, w_eg, w_eu, w_ed, sg, su, sd):
    meta, te, tbl_in, tbl_out, ntiles = _moe_plan(cnt)
    x_rows = _dispatch(h2, rank_t, comb_t, meta)
    y_rows = _ffn(x_rows, w_eg, w_eu, w_ed, te, tbl_in, tbl_out, ntiles)
    return _combine(y_rows, rank_t, comb_t, h2, x1, mod3, sg, su, sd, meta)


def _layer(x, cmod, g_mix, w_in, conv_w, a_log, dt_bias, norm_g, qn_g, kn_g, w_up_gdn, w_up_dil, w_out,
           g_ffn, w_router, router_bias, w_eg, w_eu, w_ed, w_sg, w_su, w_sd):
    bsz, s, d = x.shape
    mod3 = cmod.reshape(bsz, 6, d)
    o_ba = 4 * GDN_W
    o_dq = o_ba + 2 * GDN_HEADS
    o_ga = o_dq + 3 * DIL_W
    w_main = jnp.concatenate([w_in[:, :o_ba], w_in[:, o_ga:], w_in[:, o_dq:o_ga]], axis=1).astype(BF16)
    pad_hi = LANES - 2 * GDN_HEADS
    w_ba = jnp.pad(w_in[:, o_ba:o_dq], ((0, 0), (0, pad_hi))).astype(BF16)
    alog_vec = jnp.pad(a_log, (GDN_HEADS, pad_hi)).reshape(1, LANES)
    dtb_vec = jnp.pad(dt_bias, (GDN_HEADS, pad_hi)).reshape(1, LANES)
    proj, gates = _inproj(x, mod3, g_mix, w_main, w_ba, alog_vec, dtb_vec)

    conv_w3 = conv_w.reshape(GDN_CONV, 3, GDN_W).transpose(1, 0, 2)
    y_a = _gdn(proj, gates, conv_w3, norm_g)
    y_b = _dil(proj, qn_g, kn_g)

    wr_t = w_router.T
    wr_hi = wr_t.astype(BF16)
    w_router_split = jnp.concatenate([wr_hi, (wr_t - wr_hi.astype(F32)).astype(BF16)], axis=0)
    x1, h2, comb_t, rank_t, cnt = _merge(x, mod3, y_a, y_b, proj, w_up_gdn.astype(BF16),
                                         w_up_dil.astype(BF16), w_out.astype(BF16), g_ffn.reshape(1, d),
                                         w_router_split, router_bias.reshape(N_EXPERTS, 1))
    cnt = cnt[..., 0].astype(jnp.int32).reshape(-1, N_EXPERTS)
    return _moe(h2, comb_t, rank_t, cnt, x1, mod3, w_eg, w_eu, w_ed,
                w_sg.astype(BF16), w_su.astype(BF16), w_sd.astype(BF16))


def kernel(x, c, w_ada, b_ada, g_mix, w_in, gdn_conv_w, gdn_a_log, gdn_dt_bias, gdn_norm_g, dil_q_norm_g, dil_k_norm_g, w_up_gdn, w_up_dil, w_out, g_ffn, w_router, router_bias, w_exp_gate, w_exp_up, w_exp_down, w_sh_gate, w_sh_up, w_sh_down):
    for l in range(w_ada.shape[0]):
        cmod = _ada(c, w_ada[l], b_ada[l])
        x = _layer(x, cmod, g_mix[l], w_in[l], gdn_conv_w[l], gdn_a_log[l], gdn_dt_bias[l], gdn_norm_g[l],
                   dil_q_norm_g[l], dil_k_norm_g[l], w_up_gdn[l], w_up_dil[l], w_out[l], g_ffn[l],
                   w_router[l], router_bias[l], w_exp_gate[l], w_exp_up[l], w_exp_down[l],
                   w_sh_gate[l], w_sh_up[l], w_sh_down[l])
    return x
```

```python
import functools

import jax
import jax.numpy as jnp
from jax import lax
from jax.experimental import pallas as pl
from jax.experimental.pallas import tpu as pltpu

F32 = jnp.float32
BF16 = jnp.bfloat16

D_MODEL = 1024
GDN_HEADS = 8
GDN_DK = 128
GDN_CONV = 4
DIL_PATTERN = ((128, 1), (512, 4), (2048, 16))
DIL_HEADS_PER_GROUP = 4
DIL_HD = 128
DIL_BLOCK = 128
DIL_BATCH = 8
N_EXPERTS = 64
TOP_K = 8
N_GROUPS = 8
TOPK_GROUPS = 4
D_EXPERT = 256
ROUTE_SCALE = 2.5
EPS = 1e-6

GDN_W = GDN_HEADS * GDN_DK
DIL_HEADS = len(DIL_PATTERN) * DIL_HEADS_PER_GROUP
DIL_W = DIL_HEADS * DIL_HD
DIL_OUT_W = DIL_HEADS_PER_GROUP * DIL_HD
COL_GQ, COL_GK, COL_GV, COL_GZ = 0, GDN_W, 2 * GDN_W, 3 * GDN_W
COL_GA = 4 * GDN_W
COL_GB = COL_GA + D_MODEL
COL_DQ = COL_GB + D_MODEL
COL_DK = COL_DQ + DIL_W
COL_DV = COL_DK + DIL_W
PROJ_W = COL_DV + DIL_W
LANES = 128

VMEM_LIMIT = 56 * 1024 * 1024


def _params(*sem):
    return pltpu.CompilerParams(dimension_semantics=sem, vmem_limit_bytes=VMEM_LIMIT)


def _sigmoid(x):
    return 1.0 / (1.0 + jnp.exp(-x))


def _silu(x):
    return x * _sigmoid(x)


def _softplus(x):
    return jnp.maximum(x, 0.0) + jnp.log(1.0 + jnp.exp(-jnp.abs(x)))


def _mm(a, b):
    return jnp.dot(a.astype(BF16), b.astype(BF16), preferred_element_type=F32)


def _mm_nt(a, b):
    return lax.dot_general(a.astype(BF16), b.astype(BF16), (((1,), (1,)), ((), ())),
                           preferred_element_type=F32)


def _mm_tn(a, b):
    return lax.dot_general(a.astype(BF16), b.astype(BF16), (((0,), (0,)), ((), ())),
                           preferred_element_type=F32)


def _mm_f32(a, b):
    return jnp.dot(a, b, preferred_element_type=F32, precision=lax.Precision.HIGHEST)


def _ada_kernel(c_ref, w_ref, b_ref, o_ref):
    cs = _silu(c_ref[...])
    o_ref[...] = _mm_f32(cs, w_ref[...]) + b_ref[...]


def _ada(c, w_ada, b_ada):
    bsz, d = c.shape
    n = w_ada.shape[1]
    tn = 1536
    return pl.pallas_call(
        _ada_kernel,
        grid=(n // tn,),
        in_specs=[pl.BlockSpec((bsz, d), lambda j: (0, 0)),
                  pl.BlockSpec((d, tn), lambda j: (0, j)),
                  pl.BlockSpec((1, tn), lambda j: (0, j))],
        out_specs=pl.BlockSpec((bsz, tn), lambda j: (0, j)),
        out_shape=jax.ShapeDtypeStruct((bsz, n), F32),
        compiler_params=_params("parallel"),
        name="ada",
    )(c, w_ada, b_ada.reshape(1, n))


def _inproj_kernel(x_ref, mod_ref, g_ref, w_ref, wba_ref, alog_ref, dtb_ref,
                   proj_ref, gates_ref, h_scr):
    j = pl.program_id(2)

    @pl.when(j == 0)
    def _():
        x = x_ref[0]
        sh1 = mod_ref[0, 0:1, :]
        sc1 = mod_ref[0, 1:2, :]
        ms = jnp.mean(x * x, axis=-1, keepdims=True)
        h = x * lax.rsqrt(ms + EPS) * g_ref[...] * (1.0 + sc1) + sh1
        hb = h.astype(BF16)
        h_scr[...] = hb
        ba = jnp.dot(hb, wba_ref[...], preferred_element_type=F32)
        lane = lax.broadcasted_iota(jnp.int32, ba.shape, 1)
        beta = _sigmoid(ba)
        g = -jnp.exp(alog_ref[...]) * _softplus(ba + dtb_ref[...])
        gates_ref[0] = jnp.where(lane < GDN_HEADS, beta, g)

    proj_ref[0] = jnp.dot(h_scr[...], w_ref[...], preferred_element_type=F32).astype(BF16)


def _inproj(x, mod3, g_mix, w_main, w_ba, alog_vec, dtb_vec):
    bsz, s, d = x.shape
    tm, tn = 1024, 1536
    return pl.pallas_call(
        _inproj_kernel,
        grid=(bsz, s // tm, PROJ_W // tn),
        in_specs=[pl.BlockSpec((1, tm, d), lambda b, i, j: (b, i, 0)),
                  pl.BlockSpec((1, 6, d), lambda b, i, j: (b, 0, 0)),
                  pl.BlockSpec((1, d), lambda b, i, j: (0, 0)),
                  pl.BlockSpec((d, tn), lambda b, i, j: (0, j)),
                  pl.BlockSpec((d, LANES), lambda b, i, j: (0, 0)),
                  pl.BlockSpec((1, LANES), lambda b, i, j: (0, 0)),
                  pl.BlockSpec((1, LANES), lambda b, i, j: (0, 0))],
        out_specs=[pl.BlockSpec((1, tm, tn), lambda b, i, j: (b, i, j)),
                   pl.BlockSpec((1, tm, LANES), lambda b, i, j: (b, i, 0))],
        out_shape=[jax.ShapeDtypeStruct((bsz, s, PROJ_W), BF16),
                   jax.ShapeDtypeStruct((bsz, s, LANES), F32)],
        scratch_shapes=[pltpu.VMEM((tm, d), BF16)],
        compiler_params=_params("parallel", "parallel", "arbitrary"),
        name="inproj",
    )(x, mod3, g_mix.reshape(1, d), w_main, w_ba, alog_vec, dtb_vec)


GDN_C = 128
GDN_HG = 8
GDN_TAIL = 8


def _cumsum_rows(g):
    row = lax.broadcasted_iota(jnp.int32, g.shape, 0)
    sft = 1
    while sft < g.shape[0]:
        g = g + jnp.where(row >= sft, pltpu.roll(g, sft, 0), 0.0)
        sft *= 2
    return g


def _unit_lower_inverses(mats):
    n = mats[0].shape[0]
    row = lax.broadcasted_iota(jnp.int32, (n, n), 0)
    col = lax.broadcasted_iota(jnp.int32, (n, n), 1)
    eye = (row == col).astype(F32)
    ts = None
    b = 1
    while b < n:
        off = ((row // (2 * b)) == (col // (2 * b))) & ((row // b) % 2 == 1) & ((col // b) % 2 == 0)
        if b == 1:
            ts = [eye - jnp.where(off, a, 0.0) for a in mats]
        else:
            tb = [t.astype(BF16) for t in ts]
            inner = [jnp.dot(jnp.where(off, a, 0.0).astype(BF16), t, preferred_element_type=F32)
                     for a, t in zip(mats, tb)]
            ts = [t - jnp.dot(t16, i.astype(BF16), preferred_element_type=F32)
                  for t, t16, i in zip(ts, tb, inner)]
        b *= 2
    return ts


def _gdn_kernel(q_ref, k_ref, v_ref, z_ref, gates_ref, cw_ref, ng_ref, o_ref,
                qprev, kprev, vprev, state):
    sblk = pl.program_id(1)
    c = GDN_C
    nh = GDN_HEADS
    w = nh * GDN_DK

    @pl.when(sblk == 0)
    def _():
        state[...] = jnp.zeros_like(state)
        for buf in (qprev, kprev, vprev):
            buf[...] = jnp.zeros_like(buf)

    taps = GDN_CONV - 1
    si = lax.broadcasted_iota(jnp.int32, (taps * c, c), 0)
    sm = lax.broadcasted_iota(jnp.int32, (taps * c, c), 1)
    shift_mat = jnp.where(sm == si % c - (taps - si // c), 1.0, 0.0).astype(BF16)
    trow = lax.broadcasted_iota(jnp.int32, (GDN_TAIL, w), 0)
    conv = []
    for idx, (ref, prev) in enumerate(((q_ref, qprev), (k_ref, kprev), (v_ref, vprev))):
        cur = ref[0]
        cur32 = cur.astype(F32)
        delayed = jnp.dot(shift_mat, cur, preferred_element_type=F32)
        acc = cur32 * cw_ref[idx, taps:taps + 1, :]
        head = jnp.zeros((GDN_TAIL, w), F32)
        tail = prev[...]
        for j in range(taps):
            acc = acc + delayed[j * c:(j + 1) * c, :] * cw_ref[idx, j:j + 1, :]
            head = head + jnp.where(trow < taps - j, pltpu.roll(tail, taps - j, 0), 0.0) * cw_ref[idx, j:j + 1, :]
        acc = jnp.concatenate([acc[:GDN_TAIL] + head, acc[GDN_TAIL:]], axis=0)
        prev[...] = cur32[c - GDN_TAIL:, :]
        conv.append(_silu(acc))
    qc, kc, vc = conv

    gates = gates_ref[0]
    gcum = _cumsum_rows(gates)
    row = lax.broadcasted_iota(jnp.int32, (c, c), 0)
    col = lax.broadcasted_iota(jnp.int32, (c, c), 1)
    eye = row == col
    incl = col <= row
    strict = col < row

    for hg in range(0, nh, GDN_HG):
        heads = range(hg, hg + GDN_HG)
        hsl = {h: slice(h * GDN_DK, (h + 1) * GDN_DK) for h in heads}
        q = [qc[:, hsl[h]] for h in heads]
        k = [kc[:, hsl[h]] for h in heads]
        v = [vc[:, hsl[h]] for h in heads]
        q = [x * lax.rsqrt(jnp.sum(x * x, axis=-1, keepdims=True) + EPS) * (GDN_DK ** -0.5) for x in q]
        k = [x * lax.rsqrt(jnp.sum(x * x, axis=-1, keepdims=True) + EPS) for x in k]
        beta = [jnp.broadcast_to(gates[:, h:h + 1], (c, GDN_DK)) for h in heads]
        gc = [jnp.broadcast_to(gcum[:, nh + h:nh + h + 1], (c, GDN_DK)) for h in heads]
        gc_row = [jnp.sum(jnp.where(eye, x, 0.0), axis=0, keepdims=True) for x in gc]
        decay = [jnp.exp(jnp.where(incl, x - y, -jnp.inf)) for x, y in zip(gc, gc_row)]
        egc = [jnp.exp(x) for x in gc]
        kb = [x * b for x, b in zip(k, beta)]
        scores = [_mm_nt(jnp.concatenate([a, b], axis=0), x) for a, b, x in zip(kb, q, k)]
        a_mat = [jnp.where(strict, s[:c] * d, 0.0) for s, d in zip(scores, decay)]
        qk = [s[c:] * d for s, d in zip(scores, decay)]
        t_inv = _unit_lower_inverses(a_mat)
        sol = [_mm(t, jnp.concatenate([x * b, y * e], axis=1))
               for t, x, b, y, e in zip(t_inv, v, beta, kb, egc)]

        st = [state[h] for h in heads]
        ws = [_mm(jnp.concatenate([s[:, GDN_DK:], x * e], axis=0), m)
              for s, x, e, m in zip(sol, q, egc, st)]
        v_new = [s[:, :GDN_DK] - x[:c] for s, x in zip(sol, ws)]
        o = [x[c:] + _mm(a, b) for x, a, b in zip(ws, qk, v_new)]
        gc_last = [x[c - 1:c, :] for x in gc]
        k_dec = [x * jnp.exp(l - g) for x, l, g in zip(k, gc_last, gc)]
        for n, h in enumerate(heads):
            state[h] = st[n] * jnp.exp(gc_last[n]) + _mm_tn(k_dec[n], v_new[n])
            y = o[n] * lax.rsqrt(jnp.mean(o[n] * o[n], axis=-1, keepdims=True) + EPS) * ng_ref[...]
            o_ref[0, :, hsl[h]] = (y * _silu(z_ref[0, :, hsl[h]].astype(F32))).astype(o_ref.dtype)


def _gdn(proj, gates, conv_w3, norm_g):
    bsz, s, _ = proj.shape
    w = GDN_W

    def col_spec(col0):
        return pl.BlockSpec((1, GDN_C, w), lambda b, i: (b, i, col0 // w))

    return pl.pallas_call(
        _gdn_kernel,
        grid=(bsz, s // GDN_C),
        in_specs=[col_spec(COL_GQ), col_spec(COL_GK), col_spec(COL_GV), col_spec(COL_GZ),
                  pl.BlockSpec((1, GDN_C, LANES), lambda b, i: (b, i, 0)),
                  pl.BlockSpec((3, GDN_CONV, w), lambda b, i: (0, 0, 0)),
                  pl.BlockSpec((1, GDN_DK), lambda b, i: (0, 0))],
        out_specs=pl.BlockSpec((1, GDN_C, w), lambda b, i: (b, i, 0)),
        out_shape=jax.ShapeDtypeStruct((bsz, s, GDN_W), BF16),
        scratch_shapes=[pltpu.VMEM((GDN_TAIL, w), F32),
                        pltpu.VMEM((GDN_TAIL, w), F32),
                        pltpu.VMEM((GDN_TAIL, w), F32),
                        pltpu.VMEM((GDN_HEADS, GDN_DK, GDN_DK), F32)],
        compiler_params=_params("parallel", "arbitrary"),
        name="gdn",
    )(proj, proj, proj, proj, gates, conv_w3, norm_g.reshape(1, GDN_DK))


def _dil_kernel(q0, q1, q2, k0, k1, k2, v0, v1, v2, qg_ref, kg_ref, o_ref, qs, ks, vs, os_, ls):
    s = qs.shape[1]
    blk = DIL_BLOCK
    ngrp = len(DIL_PATTERN)
    for gi, (q_ref, k_ref, v_ref) in enumerate(((q0, k0, v0), (q1, k1, v1), (q2, k2, v2))):
        q = q_ref[0].astype(F32)
        k = k_ref[0].astype(F32)
        q = q * lax.rsqrt(jnp.mean(q * q, axis=-1, keepdims=True) + EPS) * qg_ref[...]
        k = k * lax.rsqrt(jnp.mean(k * k, axis=-1, keepdims=True) + EPS) * kg_ref[...]
        qs[gi] = q * (DIL_HD ** -0.5)
        ks[gi] = k
        vs[gi] = v_ref[0].astype(F32)

    qi = lax.broadcasted_iota(jnp.int32, (blk, blk), 0)
    kj = lax.broadcasted_iota(jnp.int32, (blk, blk), 1)
    cur_ok = kj <= qi
    prev_ok = kj >= qi

    items = []
    for gi, (win, dil) in enumerate(DIL_PATTERN):
        assert win // dil == blk and (s // dil) % blk == 0
        for r in range(dil):
            for n in range((s // dil) // blk):
                items.append((gi, dil, r, n))

    def rows(dil, r, m):
        if dil == 1:
            return pl.ds(m * blk, blk)
        return pl.ds(m * blk * dil + r, blk, stride=dil)

    for b0 in range(0, len(items), DIL_BATCH):
        batch = items[b0:b0 + DIL_BATCH]
        qb = [qs[gi, rows(dil, r, n), :] for gi, dil, r, n in batch]
        kc = [ks[gi, rows(dil, r, n), :] for gi, dil, r, n in batch]
        kp = [ks[gi, rows(dil, r, n - 1), :] if n > 0 else None for gi, dil, r, n in batch]
        s_cur = [jnp.where(cur_ok, _mm_nt(q, k), -jnp.inf) for q, k in zip(qb, kc)]
        s_prev = [None if k is None else jnp.where(prev_ok, _mm_nt(q, k), -jnp.inf) for q, k in zip(qb, kp)]
        m = [jnp.max(a, axis=-1, keepdims=True) if b is None else
             jnp.maximum(jnp.max(a, axis=-1, keepdims=True), jnp.max(b, axis=-1, keepdims=True))
             for a, b in zip(s_cur, s_prev)]
        p_cur = [jnp.exp(a - mm) for a, mm in zip(s_cur, m)]
        p_prev = [None if b is None else jnp.exp(b - mm) for b, mm in zip(s_prev, m)]
        den = [jnp.sum(a, axis=-1, keepdims=True) if b is None else
               jnp.sum(a, axis=-1, keepdims=True) + jnp.sum(b, axis=-1, keepdims=True)
               for a, b in zip(p_cur, p_prev)]
        o = [_mm(p, vs[gi, rows(dil, r, n), :]) for p, (gi, dil, r, n) in zip(p_cur, batch)]
        o = [a if p is None else a + _mm(p, vs[gi, rows(dil, r, n - 1), :])
             for a, p, (gi, dil, r, n) in zip(o, p_prev, batch)]
        for (gi, dil, r, n), a, d, mm in zip(batch, o, den, m):
            os_[gi, rows(dil, r, n), :] = a / d
            ls[gi, rows(dil, r, n), :] = jnp.broadcast_to(mm + jnp.log(d), (blk, DIL_HD))

    lse = [ls[gi] for gi in range(ngrp)]
    mx = functools.reduce(jnp.maximum, lse)
    ex = [jnp.exp(l - mx) for l in lse]
    tot = functools.reduce(lambda a, b: a + b, ex)
    y = functools.reduce(lambda a, b: a + b, [ex[gi] / tot * os_[gi] for gi in range(ngrp)])
    o_ref[0] = y.astype(o_ref.dtype)


def _dil(proj, q_norm_g, k_norm_g):
    bsz, s, _ = proj.shape
    ngrp = len(DIL_PATTERN)

    def specs(col0):
        blk0 = col0 // DIL_HD
        return [pl.BlockSpec((1, s, DIL_HD), lambda b, h, g=g: (b, 0, blk0 + g * DIL_HEADS_PER_GROUP + h))
                for g in range(ngrp)]

    gain = pl.BlockSpec((1, DIL_HD), lambda b, h: (0, 0))
    return pl.pallas_call(
        _dil_kernel,
        grid=(bsz, DIL_HEADS_PER_GROUP),
        in_specs=specs(COL_DQ) + specs(COL_DK) + specs(COL_DV) + [gain, gain],
        out_specs=pl.BlockSpec((1, s, DIL_HD), lambda b, h: (b, 0, h)),
        out_shape=jax.ShapeDtypeStruct((bsz, s, DIL_OUT_W), BF16),
        scratch_shapes=[pltpu.VMEM((ngrp, s, DIL_HD), F32) for _ in range(5)],
        compiler_params=_params("parallel", "parallel"),
        name="dilattn",
    )(*([proj] * 9), q_norm_g.reshape(1, DIL_HD), k_norm_g.reshape(1, DIL_HD))


def _route(logits, bias):
    e, tm = logits.shape
    per = e // N_GROUPS
    scores = _sigmoid(logits)
    sel = scores + bias
    neg = -jnp.inf
    sub = lax.broadcasted_iota(jnp.int32, (per, tm), 0)
    gs_rows = []
    for g in range(N_GROUPS):
        blk = sel[g * per:(g + 1) * per, :]
        m1 = jnp.max(blk, axis=0, keepdims=True)
        i1 = jnp.min(jnp.where(blk == m1, sub, per), axis=0, keepdims=True)
        m2 = jnp.max(jnp.where(sub == i1, neg, blk), axis=0, keepdims=True)
        gs_rows.append(m1 + m2)
    gs = jnp.concatenate(gs_rows, axis=0)
    gi = lax.broadcasted_iota(jnp.int32, (N_GROUPS, tm), 0)
    gsel = jnp.zeros((N_GROUPS, tm), F32)
    for _ in range(TOPK_GROUPS):
        m = jnp.max(gs, axis=0, keepdims=True)
        idx = jnp.min(jnp.where(gs == m, gi, N_GROUPS), axis=0, keepdims=True)
        hit = gi == idx
        gsel = jnp.where(hit, 1.0, gsel)
        gs = jnp.where(hit, neg, gs)
    cand = jnp.concatenate(
        [jnp.where(gsel[g:g + 1, :] > 0.0, sel[g * per:(g + 1) * per, :], neg) for g in range(N_GROUPS)], axis=0)
    ei = lax.broadcasted_iota(jnp.int32, (e, tm), 0)
    chosen = jnp.zeros((e, tm), F32)
    for _ in range(TOP_K):
        m = jnp.max(cand, axis=0, keepdims=True)
        idx = jnp.min(jnp.where(cand == m, ei, e), axis=0, keepdims=True)
        hit = ei == idx
        chosen = jnp.where(hit, scores, chosen)
        cand = jnp.where(hit, neg, cand)
    return chosen / jnp.sum(chosen, axis=0, keepdims=True) * ROUTE_SCALE


def _merge_kernel(x_ref, mod_ref, ya_ref, yb_ref, ga_ref, gb_ref, wg_ref, wd_ref, wo_ref,
                  gf_ref, wr_ref, rb_ref, x1_ref, h2_ref, comb_ref, rank_ref, cnt_ref):
    gt1 = mod_ref[0, 2:3, :]
    sh2 = mod_ref[0, 3:4, :]
    sc2 = mod_ref[0, 4:5, :]
    ua = jnp.dot(ya_ref[0], wg_ref[...], preferred_element_type=F32)
    ub = jnp.dot(yb_ref[0], wd_ref[...], preferred_element_type=F32)
    merged = _sigmoid(ga_ref[0].astype(F32)) * ua + _sigmoid(gb_ref[0].astype(F32)) * ub
    x1 = x_ref[0] + gt1 * _mm(merged, wo_ref[...])
    x1_ref[0] = x1
    h2 = x1 * lax.rsqrt(jnp.mean(x1 * x1, axis=-1, keepdims=True) + EPS) * gf_ref[...] * (1.0 + sc2) + sh2
    h2_hi = h2.astype(BF16)
    h2_ref[0] = h2_hi
    h2_lo = (h2 - h2_hi.astype(F32)).astype(BF16)
    nt_dims = (((1,), (1,)), ((), ()))
    both = lax.dot_general(wr_ref[...], h2_hi, nt_dims, preferred_element_type=F32)
    logits = (both[:N_EXPERTS] + both[N_EXPERTS:]
              + lax.dot_general(wr_ref[:N_EXPERTS, :], h2_lo, nt_dims, preferred_element_type=F32))
    comb = _route(logits, rb_ref[...])
    comb_ref[0] = comb
    tb = MOE_TB
    before = (lax.broadcasted_iota(jnp.int32, (tb, tb), 0) < lax.broadcasted_iota(jnp.int32, (tb, tb), 1))
    before = before.astype(BF16)
    for j in range(comb.shape[1] // tb):
        picked = (comb[:, j * tb:(j + 1) * tb] > 0.0).astype(F32)
        rank_ref[0, :, j * tb:(j + 1) * tb] = jnp.dot(picked.astype(BF16), before, preferred_element_type=F32)
        cnt_ref[0, j] = jnp.broadcast_to(jnp.sum(picked, axis=1, keepdims=True), (N_EXPERTS, LANES))


def _merge(x, mod3, y_a, y_b, proj, w_up_gdn, w_up_dil, w_out, g_ffn, w_router_t, router_bias):
    bsz, s, d = x.shape
    tm = 512
    row = lambda w: pl.BlockSpec((1, tm, w), lambda b, i: (b, i, 0))
    full = lambda a: pl.BlockSpec(a.shape, lambda b, i: (0,) * a.ndim)
    ex = pl.BlockSpec((1, N_EXPERTS, tm), lambda b, i: (b, 0, i))
    return pl.pallas_call(
        _merge_kernel,
        grid=(bsz, s // tm),
        in_specs=[row(d),
                  pl.BlockSpec((1, 6, d), lambda b, i: (b, 0, 0)),
                  row(GDN_W), row(DIL_OUT_W),
                  pl.BlockSpec((1, tm, d), lambda b, i: (b, i, COL_GA // d)),
                  pl.BlockSpec((1, tm, d), lambda b, i: (b, i, COL_GB // d)),
                  full(w_up_gdn), full(w_up_dil), full(w_out), full(g_ffn), full(w_router_t),
                  full(router_bias)],
        out_specs=[row(d), row(d), ex, ex,
                   pl.BlockSpec((1, tm // MOE_TB, N_EXPERTS, LANES), lambda b, i: (b, i, 0, 0))],
        out_shape=[jax.ShapeDtypeStruct((bsz, s, d), F32),
                   jax.ShapeDtypeStruct((bsz, s, d), BF16),
                   jax.ShapeDtypeStruct((bsz, N_EXPERTS, s), F32),
                   jax.ShapeDtypeStruct((bsz, N_EXPERTS, s), F32),
                   jax.ShapeDtypeStruct((bsz, s // MOE_TB, N_EXPERTS, LANES), F32)],
        compiler_params=_params("parallel", "parallel"),
        name="merge_router",
    )(x, mod3, y_a, y_b, proj, proj, w_up_gdn, w_up_dil, w_out, g_ffn, w_router_t, router_bias)


MOE_TB = 256
MOE_CH = 16
MOE_RB = MOE_TB * TOP_K + N_EXPERTS * MOE_CH
MOE_CPB = MOE_RB // MOE_CH
MOE_TM = 512
MOE_CT = MOE_RB
MOE_FT = 1024
MOE_CPT = MOE_FT // MOE_CH


def _moe_plan(cnt):
    nb, ne = cnt.shape
    i32 = jnp.int32
    nch = (cnt + MOE_CH - 1) // MOE_CH
    seg0 = jnp.cumsum(nch, axis=1) - nch
    pad = jnp.zeros((nb, LANES - ne), i32)
    meta = jnp.stack([jnp.concatenate([seg0 * MOE_CH, pad + MOE_RB], axis=1),
                      jnp.concatenate([nch * MOE_CH, pad], axis=1)], axis=1).astype(F32)
    meta = jnp.pad(meta, ((0, 0), (0, 6), (0, 0)))
    nch_e = nch.T
    cum_e = jnp.cumsum(nch_e, axis=1)
    tot_e = cum_e[:, -1]
    tiles_e = (tot_e + MOE_CPT - 1) // MOE_CPT
    tile_end = jnp.cumsum(tiles_e)
    ntiles = tile_end[-1]
    max_tiles = (nb * MOE_CPB) // MOE_CPT + ne
    i = jnp.arange(max_tiles, dtype=i32)
    te = jnp.sum((tile_end[None, :] <= i[:, None]).astype(i32), axis=1)
    te = jnp.minimum(te, jnp.sum((tile_end <= ntiles - 1).astype(i32)))
    te = jnp.minimum(te, ne - 1)
    oh_te = (te[:, None] == jnp.arange(ne, dtype=i32)[None, :]).astype(i32)
    pick = lambda tab: jnp.sum(oh_te[:, :, None] * tab[None, :, :], axis=1)
    tile_off_t = jnp.sum(oh_te * (tile_end - tiles_e)[None, :], axis=1)
    tot_t = jnp.sum(oh_te * tot_e[None, :], axis=1)
    cum_t, nch_t, seg0_t = pick(cum_e), pick(nch_e), pick(seg0.T)
    q = (i - tile_off_t)[:, None] * MOE_CPT + jnp.arange(MOE_CPT, dtype=i32)[None, :]
    valid = (q < tot_t[:, None]) & (i < ntiles)[:, None]
    blk = jnp.minimum(jnp.sum((cum_t[:, None, :] <= q[:, :, None]).astype(i32), axis=-1), nb - 1)
    oh_b = (blk[:, :, None] == jnp.arange(nb, dtype=i32)[None, None, :]).astype(i32)
    before = jnp.sum(oh_b * (cum_t - nch_t)[:, None, :], axis=-1)
    src = blk * MOE_CPB + jnp.sum(oh_b * seg0_t[:, None, :], axis=-1) + (q - before)
    spare = nb * MOE_CPB + (i % 2)[:, None] * MOE_CPT + jnp.arange(MOE_CPT, dtype=i32)[None, :]
    tbl_in = jnp.where(valid, src, nb * MOE_CPB + 2 * MOE_CPT).astype(i32).reshape(-1)
    tbl_out = jnp.where(valid, src, spare).astype(i32).reshape(-1)
    return meta, te.astype(i32), tbl_in, tbl_out, ntiles.astype(i32).reshape(1)


def _onehot_rows(meta_ref, rhs, r0, nrows, weighted):
    start = meta_ref[0, 0:1, :]
    plen = meta_ref[0, 1:2, :]
    r = (lax.broadcasted_iota(jnp.int32, (nrows, LANES), 0) + r0).astype(F32)
    owner = jnp.where((r >= start) & (r < start + plen), 1.0, 0.0)
    pos = r[:, :1] - jnp.sum(owner * start, axis=1, keepdims=True)
    got = jnp.dot(owner[:, :N_EXPERTS].astype(BF16), rhs, preferred_element_type=F32)
    tb = MOE_TB
    hit = (got[:, :tb] == pos) & (got[:, tb:2 * tb] > 0.5)
    return jnp.where(hit, got[:, 2 * tb:] if weighted else 1.0, 0.0).astype(BF16)


def _route_rhs(rank_ref, comb_ref):
    comb = comb_ref[0]
    picked = jnp.where(comb > 0.0, 1.0, 0.0)
    return jnp.concatenate([rank_ref[0], picked, comb], axis=1).astype(BF16)


def _dispatch_kernel(meta_ref, h_ref, rank_ref, comb_ref, x_ref):
    @pl.when(pl.program_id(0) < pl.num_programs(0) - 1)
    def _():
        rhs = _route_rhs(rank_ref, comb_ref)
        h = h_ref[0]
        for r0 in range(0, MOE_RB, MOE_TM):
            n = min(MOE_TM, MOE_RB - r0)
            p = _onehot_rows(meta_ref, rhs, r0, n, weighted=False)
            x_ref[0, r0:r0 + n, :] = jnp.dot(p, h, preferred_element_type=F32).astype(BF16)

    @pl.when(pl.program_id(0) == pl.num_programs(0) - 1)
    def _():
        x_ref[...] = jnp.zeros_like(x_ref)


def _dispatch(h2, rank_t, comb_t, meta):
    bsz, s, d = h2.shape
    nsb = s // MOE_TB
    nblk = bsz * nsb
    tok = lambda i: (jnp.minimum(i, nblk - 1) // nsb, jnp.minimum(i, nblk - 1) % nsb)
    ex = pl.BlockSpec((1, N_EXPERTS, MOE_TB), lambda i: (tok(i)[0], 0, tok(i)[1]))
    return pl.pallas_call(
        _dispatch_kernel,
        grid=(nblk + 1,),
        in_specs=[pl.BlockSpec((1, 8, LANES), lambda i: (jnp.minimum(i, nblk - 1), 0, 0)),
                  pl.BlockSpec((1, MOE_TB, d), lambda i: (tok(i)[0], tok(i)[1], 0)),
                  ex, ex],
        out_specs=pl.BlockSpec((1, MOE_RB, d), lambda i: (i, 0, 0)),
        out_shape=jax.ShapeDtypeStruct((nblk + 1, MOE_RB, d), BF16),
        compiler_params=_params("parallel"),
        name="moe_dispatch",
    )(meta, h2, rank_t, comb_t)


def _ffn_kernel(te_ref, tin_ref, tout_ref, nt_ref, x_hbm, wg_ref, wu_ref, wd_ref, y_hbm,
                xbuf, ybuf, wgb, wub, wdb, sem_in, sem_out):
    i = pl.program_id(0)
    nt = nt_ref[0]
    slot = i % 2

    def copy_in(t, s, j):
        return pltpu.make_async_copy(x_hbm.at[tin_ref[t * MOE_CPT + j]],
                                     xbuf.at[s, pl.ds(j * MOE_CH, MOE_CH)], sem_in.at[s])

    def copy_out(t, s, j):
        return pltpu.make_async_copy(ybuf.at[s, pl.ds(j * MOE_CH, MOE_CH)],
                                     y_hbm.at[tout_ref[t * MOE_CPT + j]], sem_out.at[s])

    @pl.when(i == 0)
    def _():
        for j in range(MOE_CPT):
            copy_in(0, 0, j).start()

    @pl.when(i + 1 < nt)
    def _():
        for j in range(MOE_CPT):
            copy_in(i + 1, 1 - slot, j).start()

    @pl.when(i < nt)
    def _():
        for j in range(MOE_CPT):
            copy_in(i, slot, j).wait()

        @pl.when(i >= 2)
        def _():
            for j in range(MOE_CPT):
                copy_out(i - 2, slot, j).wait()

        @pl.when(jnp.logical_or(i == 0, te_ref[i] != te_ref[jnp.maximum(i - 1, 0)]))
        def _():
            wgb[...] = wg_ref[0].astype(BF16)
            wub[...] = wu_ref[0].astype(BF16)
            wdb[...] = wd_ref[0].astype(BF16)

        x = xbuf[slot]
        a = jnp.dot(x, wgb[...], preferred_element_type=F32)
        u = jnp.dot(x, wub[...], preferred_element_type=F32)
        ybuf[slot] = jnp.dot((_silu(a) * u).astype(BF16), wdb[...], preferred_element_type=F32).astype(BF16)
        for j in range(MOE_CPT):
            copy_out(i, slot, j).start()

        @pl.when(i == nt - 1)
        def _():
            @pl.when(i >= 1)
            def _():
                for j in range(MOE_CPT):
                    copy_out(i - 1, 1 - slot, j).wait()
            for j in range(MOE_CPT):
                copy_out(i, slot, j).wait()


def _ffn(x_rows, w_eg, w_eu, w_ed, te, tbl_in, tbl_out, ntiles):
    nblk, rb, d = x_rows.shape
    ne, _, de = w_eg.shape
    xc = x_rows.reshape(nblk * MOE_CPB, MOE_CH, d)
    max_tiles = te.shape[0]
    y = pl.pallas_call(
        _ffn_kernel,
        grid_spec=pltpu.PrefetchScalarGridSpec(
            num_scalar_prefetch=4,
            grid=(max_tiles,),
            in_specs=[pl.BlockSpec(memory_space=pl.ANY),
                      pl.BlockSpec((1, d, de), lambda i, te, *_: (te[i], 0, 0)),
                      pl.BlockSpec((1, d, de), lambda i, te, *_: (te[i], 0, 0)),
                      pl.BlockSpec((1, de, d), lambda i, te, *_: (te[i], 0, 0))],
            out_specs=pl.BlockSpec(memory_space=pl.ANY),
            scratch_shapes=[pltpu.VMEM((2, MOE_FT, d), BF16), pltpu.VMEM((2, MOE_FT, d), BF16),
                            pltpu.VMEM((d, de), BF16), pltpu.VMEM((d, de), BF16), pltpu.VMEM((de, d), BF16),
                            pltpu.SemaphoreType.DMA((2,)), pltpu.SemaphoreType.DMA((2,))]),
        out_shape=jax.ShapeDtypeStruct(xc.shape, BF16),
        input_output_aliases={4: 0},
        compiler_params=_params("arbitrary"),
        name="moe_ffn",
    )(te, tbl_in, tbl_out, ntiles, xc, w_eg, w_eu, w_ed)
    return y.reshape(nblk, rb, d)


def _combine_kernel(meta_ref, y_ref, rank_ref, comb_ref, h_ref, x1_ref, mod_ref,
                    sg_ref, su_ref, sd_ref, o_ref):
    rhs = _route_rhs(rank_ref, comb_ref)
    h = h_ref[0]
    a = jnp.dot(h, sg_ref[...], preferred_element_type=F32)
    u = jnp.dot(h, su_ref[...], preferred_element_type=F32)
    acc = _mm(_silu(a) * u, sd_ref[...])
    for r0 in range(0, MOE_RB, MOE_CT):
        n = min(MOE_CT, MOE_RB - r0)
        p = _onehot_rows(meta_ref, rhs, r0, n, weighted=True)
        acc = acc + lax.dot_general(p, y_ref[0, r0:r0 + n, :], (((0,), (0,)), ((), ())),
                                    preferred_element_type=F32)
    o_ref[0] = x1_ref[0] + mod_ref[0, 5:6, :] * acc


def _combine(y_rows, rank_t, comb_t, h2, x1, mod3, sg, su, sd, meta):
    bsz, s, d = x1.shape
    nsb = s // MOE_TB
    tok = lambda w: pl.BlockSpec((1, MOE_TB, w), lambda b, i: (b, i, 0))
    ex = pl.BlockSpec((1, N_EXPERTS, MOE_TB), lambda b, i: (b, 0, i))
    full = lambda a: pl.BlockSpec(a.shape, lambda b, i: (0,) * a.ndim)
    return pl.pallas_call(
        _combine_kernel,
        grid=(bsz, nsb),
        in_specs=[pl.BlockSpec((1, 8, LANES), lambda b, i: (b * nsb + i, 0, 0)),
                  pl.BlockSpec((1, MOE_RB, d), lambda b, i: (b * nsb + i, 0, 0)),
                  ex, ex, tok(d), tok(d),
                  pl.BlockSpec((1, 6, d), lambda b, i: (b, 0, 0)),
                  full(sg), full(su), full(sd)],
        out_specs=tok(d),
        out_shape=jax.ShapeDtypeStruct((bsz, s, d), F32),
        compiler_params=_params("parallel", "parallel"),
        name="moe_combine",
    )(meta, y_rows, rank_t, comb_t, h2, x1, mod3, sg, su, sd)


def _moe(h2, comb_t, rank_t, cnt, x1, mod3, w_eg, w_eu, w_ed, sg, su, sd):
    meta, te, tbl_in, tbl_out, ntiles = _moe_plan(cnt)
    x_rows = _dispatch(h2, rank_t, comb_t, meta)
    y_rows = _ffn(x_rows, w_eg, w_eu, w_ed, te, tbl_in, tbl_out, ntiles)
    return _combine(y_rows, rank_t, comb_t, h2, x1, mod3, sg, su, sd, meta)


def _layer(x, cmod, g_mix, w_in, conv_w, a_log, dt_bias, norm_g, qn_g, kn_g, w_up_gdn, w_up_dil, w_out,
           g_ffn, w_router, router_bias, w_eg, w_eu, w_ed, w_sg, w_su, w_sd):
    bsz, s, d = x.shape
    mod3 = cmod.reshape(bsz, 6, d)
    o_ba = 4 * GDN_W
    o_dq = o_ba + 2 * GDN_HEADS
    o_ga = o_dq + 3 * DIL_W
    w_main = jnp.concatenate([w_in[:, :o_ba], w_in[:, o_ga:], w_in[:, o_dq:o_ga]], axis=1).astype(BF16)
    pad_hi = LANES - 2 * GDN_HEADS
    w_ba = jnp.pad(w_in[:, o_ba:o_dq], ((0, 0), (0, pad_hi))).astype(BF16)
    alog_vec = jnp.pad(a_log, (GDN_HEADS, pad_hi)).reshape(1, LANES)
    dtb_vec = jnp.pad(dt_bias, (GDN_HEADS, pad_hi)).reshape(1, LANES)
    proj, gates = _inproj(x, mod3, g_mix, w_main, w_ba, alog_vec, dtb_vec)

    conv_w3 = conv_w.reshape(GDN_CONV, 3, GDN_W).transpose(1, 0, 2)
    y_a = _gdn(proj, gates, conv_w3, norm_g)
    y_b = _dil(proj, qn_g, kn_g)

    wr_t = w_router.T
    wr_hi = wr_t.astype(BF16)
    w_router_split = jnp.concatenate([wr_hi, (wr_t - wr_hi.astype(F32)).astype(BF16)], axis=0)
    x1, h2, comb_t, rank_t, cnt = _merge(x, mod3, y_a, y_b, proj, w_up_gdn.astype(BF16),
                                         w_up_dil.astype(BF16), w_out.astype(BF16), g_ffn.reshape(1, d),
                                         w_router_split, router_bias.reshape(N_EXPERTS, 1))
    cnt = cnt[..., 0].astype(jnp.int32).reshape(-1, N_EXPERTS)
    return _moe(h2, comb_t, rank_t, cnt, x1, mod3, w_eg, w_eu, w_ed,
                w_sg.astype(BF16), w_su.astype(BF16), w_sd.astype(BF16))


def kernel(x, c, w_ada, b_ada, g_mix, w_in, gdn_conv_w, gdn_a_log, gdn_dt_bias, gdn_norm_g, dil_q_norm_g, dil_k_norm_g, w_up_gdn, w_up_dil, w_out, g_ffn, w_router, router_bias, w_exp_gate, w_exp_up, w_exp_down, w_sh_gate, w_sh_up, w_sh_down):
    for l in range(w_ada.shape[0]):
        cmod = _ada(c, w_ada[l], b_ada[l])
        x = _layer(x, cmod, g_mix[l], w_in[l], gdn_conv_w[l], gdn_a_log[l], gdn_dt_bias[l], gdn_norm_g[l],
                   dil_q_norm_g[l], dil_k_norm_g[l], w_up_gdn[l], w_up_dil[l], w_out[l], g_ffn[l],
                   w_router[l], router_bias[l], w_exp_gate[l], w_exp_up[l], w_exp_down[l],
                   w_sh_gate[l], w_sh_up[l], w_sh_down[l])
    return x
```

```python
import functools

import jax
import jax.numpy as jnp
from jax import lax
from jax.experimental import pallas as pl
from jax.experimental.pallas import tpu as pltpu

F32 = jnp.float32
BF16 = jnp.bfloat16

D_MODEL = 1024
GDN_HEADS = 8
GDN_DK = 128
GDN_CONV = 4
DIL_PATTERN = ((128, 1), (512, 4), (2048, 16))
DIL_HEADS_PER_GROUP = 4
DIL_HD = 128
DIL_BLOCK = 128
DIL_BATCH = 8
N_EXPERTS = 64
TOP_K = 8
N_GROUPS = 8
TOPK_GROUPS = 4
D_EXPERT = 256
ROUTE_SCALE = 2.5
EPS = 1e-6

GDN_W = GDN_HEADS * GDN_DK
DIL_HEADS = len(DIL_PATTERN) * DIL_HEADS_PER_GROUP
DIL_W = DIL_HEADS * DIL_HD
DIL_OUT_W = DIL_HEADS_PER_GROUP * DIL_HD
COL_GQ, COL_GK, COL_GV, COL_GZ = 0, GDN_W, 2 * GDN_W, 3 * GDN_W
COL_GA = 4 * GDN_W
COL_GB = COL_GA + D_MODEL
COL_DQ = COL_GB + D_MODEL
COL_DK = COL_DQ + DIL_W
COL_DV = COL_DK + DIL_W
PROJ_W = COL_DV + DIL_W
LANES = 128

VMEM_LIMIT = 56 * 1024 * 1024


def _params(*sem):
    return pltpu.CompilerParams(dimension_semantics=sem, vmem_limit_bytes=VMEM_LIMIT)


def _sigmoid(x):
    return 1.0 / (1.0 + jnp.exp(-x))


def _silu(x):
    return x * _sigmoid(x)


def _softplus(x):
    return jnp.maximum(x, 0.0) + jnp.log(1.0 + jnp.exp(-jnp.abs(x)))


def _mm(a, b):
    return jnp.dot(a.astype(BF16), b.astype(BF16), preferred_element_type=F32)


def _mm_nt(a, b):
    return lax.dot_general(a.astype(BF16), b.astype(BF16), (((1,), (1,)), ((), ())),
                           preferred_element_type=F32)


def _mm_tn(a, b):
    return lax.dot_general(a.astype(BF16), b.astype(BF16), (((0,), (0,)), ((), ())),
                           preferred_element_type=F32)


def _mm_f32(a, b):
    return jnp.dot(a, b, preferred_element_type=F32, precision=lax.Precision.HIGHEST)


def _ada_kernel(c_ref, w_ref, b_ref, o_ref):
    cs = _silu(c_ref[...])
    o_ref[...] = _mm_f32(cs, w_ref[...]) + b_ref[...]


def _ada(c, w_ada, b_ada):
    bsz, d = c.shape
    n = w_ada.shape[1]
    tn = 1536
    return pl.pallas_call(
        _ada_kernel,
        grid=(n // tn,),
        in_specs=[pl.BlockSpec((bsz, d), lambda j: (0, 0)),
                  pl.BlockSpec((d, tn), lambda j: (0, j)),
                  pl.BlockSpec((1, tn), lambda j: (0, j))],
        out_specs=pl.BlockSpec((bsz, tn), lambda j: (0, j)),
        out_shape=jax.ShapeDtypeStruct((bsz, n), F32),
        compiler_params=_params("parallel"),
        name="ada",
    )(c, w_ada, b_ada.reshape(1, n))


def _inproj_kernel(x_ref, mod_ref, g_ref, w_ref, wba_ref, alog_ref, dtb_ref,
                   proj_ref, gates_ref, h_scr):
    j = pl.program_id(2)

    @pl.when(j == 0)
    def _():
        x = x_ref[0]
        sh1 = mod_ref[0, 0:1, :]
        sc1 = mod_ref[0, 1:2, :]
        ms = jnp.mean(x * x, axis=-1, keepdims=True)
        h = x * lax.rsqrt(ms + EPS) * g_ref[...] * (1.0 + sc1) + sh1
        hb = h.astype(BF16)
        h_scr[...] = hb
        ba = jnp.dot(hb, wba_ref[...], preferred_element_type=F32)
        lane = lax.broadcasted_iota(jnp.int32, ba.shape, 1)
        beta = _sigmoid(ba)
        g = -jnp.exp(alog_ref[...]) * _softplus(ba + dtb_ref[...])
        gates_ref[0] = jnp.where(lane < GDN_HEADS, beta, g)

    proj_ref[0] = jnp.dot(h_scr[...], w_ref[...], preferred_element_type=F32).astype(BF16)


def _inproj(x, mod3, g_mix, w_main, w_ba, alog_vec, dtb_vec):
    bsz, s, d = x.shape
    tm, tn = 1024, 1536
    return pl.pallas_call(
        _inproj_kernel,
        grid=(bsz, s // tm, PROJ_W // tn),
        in_specs=[pl.BlockSpec((1, tm, d), lambda b, i, j: (b, i, 0)),
                  pl.BlockSpec((1, 6, d), lambda b, i, j: (b, 0, 0)),
                  pl.BlockSpec((1, d), lambda b, i, j: (0, 0)),
                  pl.BlockSpec((d, tn), lambda b, i, j: (0, j)),
                  pl.BlockSpec((d, LANES), lambda b, i, j: (0, 0)),
                  pl.BlockSpec((1, LANES), lambda b, i, j: (0, 0)),
                  pl.BlockSpec((1, LANES), lambda b, i, j: (0, 0))],
        out_specs=[pl.BlockSpec((1, tm, tn), lambda b, i, j: (b, i, j)),
                   pl.BlockSpec((1, tm, LANES), lambda b, i, j: (b, i, 0))],
        out_shape=[jax.ShapeDtypeStruct((bsz, s, PROJ_W), BF16),
                   jax.ShapeDtypeStruct((bsz, s, LANES), F32)],
        scratch_shapes=[pltpu.VMEM((tm, d), BF16)],
        compiler_params=_params("parallel", "parallel", "arbitrary"),
        name="inproj",
    )(x, mod3, g_mix.reshape(1, d), w_main, w_ba, alog_vec, dtb_vec)


GDN_C = 128
GDN_HG = 8
GDN_TAIL = 8


def _cumsum_rows(g):
    row = lax.broadcasted_iota(jnp.int32, g.shape, 0)
    sft = 1
    while sft < g.shape[0]:
        g = g + jnp.where(row >= sft, pltpu.roll(g, sft, 0), 0.0)
        sft *= 2
    return g


def _unit_lower_inverses(mats):
    n = mats[0].shape[0]
    row = lax.broadcasted_iota(jnp.int32, (n, n), 0)
    col = lax.broadcasted_iota(jnp.int32, (n, n), 1)
    eye = (row == col).astype(F32)
    ts = None
    b = 1
    while b < n:
        off = ((row // (2 * b)) == (col // (2 * b))) & ((row // b) % 2 == 1) & ((col // b) % 2 == 0)
        if b == 1:
            ts = [eye - jnp.where(off, a, 0.0) for a in mats]
        else:
            tb = [t.astype(BF16) for t in ts]
            inner = [jnp.dot(jnp.where(off, a, 0.0).astype(BF16), t, preferred_element_type=F32)
                     for a, t in zip(mats, tb)]
            ts = [t - jnp.dot(t16, i.astype(BF16), preferred_element_type=F32)
                  for t, t16, i in zip(ts, tb, inner)]
        b *= 2
    return ts


def _gdn_kernel(q_ref, k_ref, v_ref, z_ref, gates_ref, cw_ref, ng_ref, o_ref,
                qprev, kprev, vprev, state):
    sblk = pl.program_id(1)
    c = GDN_C
    nh = GDN_HEADS
    w = nh * GDN_DK

    @pl.when(sblk == 0)
    def _():
        state[...] = jnp.zeros_like(state)
        for buf in (qprev, kprev, vprev):
            buf[...] = jnp.zeros_like(buf)

    taps = GDN_CONV - 1
    si = lax.broadcasted_iota(jnp.int32, (taps * c, c), 0)
    sm = lax.broadcasted_iota(jnp.int32, (taps * c, c), 1)
    shift_mat = jnp.where(sm == si % c - (taps - si // c), 1.0, 0.0).astype(BF16)
    trow = lax.broadcasted_iota(jnp.int32, (GDN_TAIL, w), 0)
    conv = []
    for idx, (ref, prev) in enumerate(((q_ref, qprev), (k_ref, kprev), (v_ref, vprev))):
        cur = ref[0]
        cur32 = cur.astype(F32)
        delayed = jnp.dot(shift_mat, cur, preferred_element_type=F32)
        acc = cur32 * cw_ref[idx, taps:taps + 1, :]
        head = jnp.zeros((GDN_TAIL, w), F32)
        tail = prev[...]
        for j in range(taps):
            acc = acc + delayed[j * c:(j + 1) * c, :] * cw_ref[idx, j:j + 1, :]
            head = head + jnp.where(trow < taps - j, pltpu.roll(tail, taps - j, 0), 0.0) * cw_ref[idx, j:j + 1, :]
        acc = jnp.concatenate([acc[:GDN_TAIL] + head, acc[GDN_TAIL:]], axis=0)
        prev[...] = cur32[c - GDN_TAIL:, :]
        conv.append(_silu(acc))
    qc, kc, vc = conv

    gates = gates_ref[0]
    gcum = _cumsum_rows(gates)
    row = lax.broadcasted_iota(jnp.int32, (c, c), 0)
    col = lax.broadcasted_iota(jnp.int32, (c, c), 1)
    eye = row == col
    incl = col <= row
    strict = col < row

    for hg in range(0, nh, GDN_HG):
        heads = range(hg, hg + GDN_HG)
        hsl = {h: slice(h * GDN_DK, (h + 1) * GDN_DK) for h in heads}
        q = [qc[:, hsl[h]] for h in heads]
        k = [kc[:, hsl[h]] for h in heads]
        v = [vc[:, hsl[h]] for h in heads]
        q = [x * lax.rsqrt(jnp.sum(x * x, axis=-1, keepdims=True) + EPS) * (GDN_DK ** -0.5) for x in q]
        k = [x * lax.rsqrt(jnp.sum(x * x, axis=-1, keepdims=True) + EPS) for x in k]
        beta = [jnp.broadcast_to(gates[:, h:h + 1], (c, GDN_DK)) for h in heads]
        gc = [jnp.broadcast_to(gcum[:, nh + h:nh + h + 1], (c, GDN_DK)) for h in heads]
        gc_row = [jnp.sum(jnp.where(eye, x, 0.0), axis=0, keepdims=True) for x in gc]
        decay = [jnp.exp(jnp.where(incl, x - y, -jnp.inf)) for x, y in zip(gc, gc_row)]
        egc = [jnp.exp(x) for x in gc]
        kb = [x * b for x, b in zip(k, beta)]
        scores = [_mm_nt(jnp.concatenate([a, b], axis=0), x) for a, b, x in zip(kb, q, k)]
        a_mat = [jnp.where(strict, s[:c] * d, 0.0) for s, d in zip(scores, decay)]
        qk = [s[c:] * d for s, d in zip(scores, decay)]
        t_inv = _unit_lower_inverses(a_mat)
        sol = [_mm(t, jnp.concatenate([x * b, y * e], axis=1))
               for t, x, b, y, e in zip(t_inv, v, beta, kb, egc)]

        st = [state[h] for h in heads]
        ws = [_mm(jnp.concatenate([s[:, GDN_DK:], x * e], axis=0), m)
              for s, x, e, m in zip(sol, q, egc, st)]
        v_new = [s[:, :GDN_DK] - x[:c] for s, x in zip(sol, ws)]
        o = [x[c:] + _mm(a, b) for x, a, b in zip(ws, qk, v_new)]
        gc_last = [x[c - 1:c, :] for x in gc]
        k_dec = [x * jnp.exp(l - g) for x, l, g in zip(k, gc_last, gc)]
        for n, h in enumerate(heads):
            state[h] = st[n] * jnp.exp(gc_last[n]) + _mm_tn(k_dec[n], v_new[n])
            y = o[n] * lax.rsqrt(jnp.mean(o[n] * o[n], axis=-1, keepdims=True) + EPS) * ng_ref[...]
            o_ref[0, :, hsl[h]] = (y * _silu(z_ref[0, :, hsl[h]].astype(F32))).astype(o_ref.dtype)


def _gdn(proj, gates, conv_w3, norm_g):
    bsz, s, _ = proj.shape
    w = GDN_W

    def col_spec(col0):
        return pl.BlockSpec((1, GDN_C, w), lambda b, i: (b, i, col0 // w))

    return pl.pallas_call(
        _gdn_kernel,
        grid=(bsz, s // GDN_C),
        in_specs=[col_spec(COL_GQ), col_spec(COL_GK), col_spec(COL_GV), col_spec(COL_GZ),
                  pl.BlockSpec((1, GDN_C, LANES), lambda b, i: (b, i, 0)),
                  pl.BlockSpec((3, GDN_CONV, w), lambda b, i: (0, 0, 0)),
                  pl.BlockSpec((1, GDN_DK), lambda b, i: (0, 0))],
        out_specs=pl.BlockSpec((1, GDN_C, w), lambda b, i: (b, i, 0)),
        out_shape=jax.ShapeDtypeStruct((bsz, s, GDN_W), BF16),
        scratch_shapes=[pltpu.VMEM((GDN_TAIL, w), F32),
                        pltpu.VMEM((GDN_TAIL, w), F32),
                        pltpu.VMEM((GDN_TAIL, w), F32),
                        pltpu.VMEM((GDN_HEADS, GDN_DK, GDN_DK), F32)],
        compiler_params=_params("parallel", "arbitrary"),
        name="gdn",
    )(proj, proj, proj, proj, gates, conv_w3, norm_g.reshape(1, GDN_DK))


def _dil_kernel(q0, q1, q2, k0, k1, k2, v0, v1, v2, qg_ref, kg_ref, o_ref, qs, ks, vs, os_, ls):
    s = qs.shape[1]
    blk = DIL_BLOCK
    ngrp = len(DIL_PATTERN)
    for gi, (q_ref, k_ref, v_ref) in enumerate(((q0, k0, v0), (q1, k1, v1), (q2, k2, v2))):
        q = q_ref[0].astype(F32)
        k = k_ref[0].astype(F32)
        q = q * lax.rsqrt(jnp.mean(q * q, axis=-1, keepdims=True) + EPS) * qg_ref[...]
        k = k * lax.rsqrt(jnp.mean(k * k, axis=-1, keepdims=True) + EPS) * kg_ref[...]
        qs[gi] = q * (DIL_HD ** -0.5)
        ks[gi] = k
        vs[gi] = v_ref[0].astype(F32)

    qi = lax.broadcasted_iota(jnp.int32, (blk, blk), 0)
    kj = lax.broadcasted_iota(jnp.int32, (blk, blk), 1)
    cur_ok = kj <= qi
    prev_ok = kj >= qi

    items = []
    for gi, (win, dil) in enumerate(DIL_PATTERN):
        assert win // dil == blk and (s // dil) % blk == 0
        for r in range(dil):
            for n in range((s // dil) // blk):
                items.append((gi, dil, r, n))

    def rows(dil, r, m):
        if dil == 1:
            return pl.ds(m * blk, blk)
        return pl.ds(m * blk * dil + r, blk, stride=dil)

    for b0 in range(0, len(items), DIL_BATCH):
        batch = items[b0:b0 + DIL_BATCH]
        qb = [qs[gi, rows(dil, r, n), :] for gi, dil, r, n in batch]
        kc = [ks[gi, rows(dil, r, n), :] for gi, dil, r, n in batch]
        kp = [ks[gi, rows(dil, r, n - 1), :] if n > 0 else None for gi, dil, r, n in batch]
        s_cur = [jnp.where(cur_ok, _mm_nt(q, k), -jnp.inf) for q, k in zip(qb, kc)]
        s_prev = [None if k is None else jnp.where(prev_ok, _mm_nt(q, k), -jnp.inf) for q, k in zip(qb, kp)]
        m = [jnp.max(a, axis=-1, keepdims=True) if b is None else
             jnp.maximum(jnp.max(a, axis=-1, keepdims=True), jnp.max(b, axis=-1, keepdims=True))
             for a, b in zip(s_cur, s_prev)]
        p_cur = [jnp.exp(a - mm) for a, mm in zip(s_cur, m)]
        p_prev = [None if b is None else jnp.exp(b - mm) for b, mm in zip(s_prev, m)]
        den = [jnp.sum(a, axis=-1, keepdims=True) if b is None else
               jnp.sum(a, axis=-1, keepdims=True) + jnp.sum(b, axis=-1, keepdims=True)
               for a, b in zip(p_cur, p_prev)]
        o = [_mm(p, vs[gi, rows(dil, r, n), :]) for p, (gi, dil, r, n) in zip(p_cur, batch)]
        o = [a if p is None else a + _mm(p, vs[gi, rows(dil, r, n - 1), :])
             for a, p, (gi, dil, r, n) in zip(o, p_prev, batch)]
        for (gi, dil, r, n), a, d, mm in zip(batch, o, den, m):
            os_[gi, rows(dil, r, n), :] = a / d
            ls[gi, rows(dil, r, n), :] = jnp.broadcast_to(mm + jnp.log(d), (blk, DIL_HD))

    lse = [ls[gi] for gi in range(ngrp)]
    mx = functools.reduce(jnp.maximum, lse)
    ex = [jnp.exp(l - mx) for l in lse]
    tot = functools.reduce(lambda a, b: a + b, ex)
    y = functools.reduce(lambda a, b: a + b, [ex[gi] / tot * os_[gi] for gi in range(ngrp)])
    o_ref[0] = y.astype(o_ref.dtype)


def _dil(proj, q_norm_g, k_norm_g):
    bsz, s, _ = proj.shape
    ngrp = len(DIL_PATTERN)

    def specs(col0):
        blk0 = col0 // DIL_HD
        return [pl.BlockSpec((1, s, DIL_HD), lambda b, h, g=g: (b, 0, blk0 + g * DIL_HEADS_PER_GROUP + h))
                for g in range(ngrp)]

    gain = pl.BlockSpec((1, DIL_HD), lambda b, h: (0, 0))
    return pl.pallas_call(
        _dil_kernel,
        grid=(bsz, DIL_HEADS_PER_GROUP),
        in_specs=specs(COL_DQ) + specs(COL_DK) + specs(COL_DV) + [gain, gain],
        out_specs=pl.BlockSpec((1, s, DIL_HD), lambda b, h: (b, 0, h)),
        out_shape=jax.ShapeDtypeStruct((bsz, s, DIL_OUT_W), BF16),
        scratch_shapes=[pltpu.VMEM((ngrp, s, DIL_HD), F32) for _ in range(5)],
        compiler_params=_params("parallel", "parallel"),
        name="dilattn",
    )(*([proj] * 9), q_norm_g.reshape(1, DIL_HD), k_norm_g.reshape(1, DIL_HD))


def _route(logits, bias):
    e, tm = logits.shape
    per = e // N_GROUPS
    scores = _sigmoid(logits)
    sel = scores + bias
    neg = -jnp.inf
    sub = lax.broadcasted_iota(jnp.int32, (per, tm), 0)
    gs_rows = []
    for g in range(N_GROUPS):
        blk = sel[g * per:(g + 1) * per, :]
        m1 = jnp.max(blk, axis=0, keepdims=True)
        i1 = jnp.min(jnp.where(blk == m1, sub, per), axis=0, keepdims=True)
        m2 = jnp.max(jnp.where(sub == i1, neg, blk), axis=0, keepdims=True)
        gs_rows.append(m1 + m2)
    gs = jnp.concatenate(gs_rows, axis=0)
    gi = lax.broadcasted_iota(jnp.int32, (N_GROUPS, tm), 0)
    gsel = jnp.zeros((N_GROUPS, tm), F32)
    for _ in range(TOPK_GROUPS):
        m = jnp.max(gs, axis=0, keepdims=True)
        idx = jnp.min(jnp.where(gs == m, gi, N_GROUPS), axis=0, keepdims=True)
        hit = gi == idx
        gsel = jnp.where(hit, 1.0, gsel)
        gs = jnp.where(hit, neg, gs)
    cand = jnp.concatenate(
        [jnp.where(gsel[g:g + 1, :] > 0.0, sel[g * per:(g + 1) * per, :], neg) for g in range(N_GROUPS)], axis=0)
    ei = lax.broadcasted_iota(jnp.int32, (e, tm), 0)
    chosen = jnp.zeros((e, tm), F32)
    for _ in range(TOP_K):
        m = jnp.max(cand, axis=0, keepdims=True)
        idx = jnp.min(jnp.where(cand == m, ei, e), axis=0, keepdims=True)
        hit = ei == idx
        chosen = jnp.where(hit, scores, chosen)
        cand = jnp.where(hit, neg, cand)
    return chosen / jnp.sum(chosen, axis=0, keepdims=True) * ROUTE_SCALE


def _merge_kernel(x_ref, mod_ref, ya_ref, yb_ref, ga_ref, gb_ref, wg_ref, wd_ref, wo_ref,
                  gf_ref, wr_ref, rb_ref, x1_ref, h2_ref, comb_ref, rank_ref, cnt_ref):
    gt1 = mod_ref[0, 2:3, :]
    sh2 = mod_ref[0, 3:4, :]
    sc2 = mod_ref[0, 4:5, :]
    ua = jnp.dot(ya_ref[0], wg_ref[...], preferred_element_type=F32)
    ub = jnp.dot(yb_ref[0], wd_ref[...], preferred_element_type=F32)
    merged = _sigmoid(ga_ref[0].astype(F32)) * ua + _sigmoid(gb_ref[0].astype(F32)) * ub
    x1 = x_ref[0] + gt1 * _mm(merged, wo_ref[...])
    x1_ref[0] = x1
    h2 = x1 * lax.rsqrt(jnp.mean(x1 * x1, axis=-1, keepdims=True) + EPS) * gf_ref[...] * (1.0 + sc2) + sh2
    h2_hi = h2.astype(BF16)
    h2_ref[0] = h2_hi
    h2_lo = (h2 - h2_hi.astype(F32)).astype(BF16)
    nt_dims = (((1,), (1,)), ((), ()))
    both = lax.dot_general(wr_ref[...], h2_hi, nt_dims, preferred_element_type=F32)
    logits = (both[:N_EXPERTS] + both[N_EXPERTS:]
              + lax.dot_general(wr_ref[:N_EXPERTS, :], h2_lo, nt_dims, preferred_element_type=F32))
    comb = _route(logits, rb_ref[...])
    comb_ref[0] = comb
    tb = MOE_TB
    before = (lax.broadcasted_iota(jnp.int32, (tb, tb), 0) < lax.broadcasted_iota(jnp.int32, (tb, tb), 1))
    before = before.astype(BF16)
    for j in range(comb.shape[1] // tb):
        picked = (comb[:, j * tb:(j + 1) * tb] > 0.0).astype(F32)
        rank_ref[0, :, j * tb:(j + 1) * tb] = jnp.dot(picked.astype(BF16), before, preferred_element_type=F32)
        cnt_ref[0, j] = jnp.broadcast_to(jnp.sum(picked, axis=1, keepdims=True), (N_EXPERTS, LANES))


def _merge(x, mod3, y_a, y_b, proj, w_up_gdn, w_up_dil, w_out, g_ffn, w_router_t, router_bias):
    bsz, s, d = x.shape
    tm = 512
    row = lambda w: pl.BlockSpec((1, tm, w), lambda b, i: (b, i, 0))
    full = lambda a: pl.BlockSpec(a.shape, lambda b, i: (0,) * a.ndim)
    ex = pl.BlockSpec((1, N_EXPERTS, tm), lambda b, i: (b, 0, i))
    return pl.pallas_call(
        _merge_kernel,
        grid=(bsz, s // tm),
        in_specs=[row(d),
                  pl.BlockSpec((1, 6, d), lambda b, i: (b, 0, 0)),
                  row(GDN_W), row(DIL_OUT_W),
                  pl.BlockSpec((1, tm, d), lambda b, i: (b, i, COL_GA // d)),
                  pl.BlockSpec((1, tm, d), lambda b, i: (b, i, COL_GB // d)),
                  full(w_up_gdn), full(w_up_dil), full(w_out), full(g_ffn), full(w_router_t),
                  full(router_bias)],
        out_specs=[row(d), row(d), ex, ex,
                   pl.BlockSpec((1, tm // MOE_TB, N_EXPERTS, LANES), lambda b, i: (b, i, 0, 0))],
        out_shape=[jax.ShapeDtypeStruct((bsz, s, d), F32),
                   jax.ShapeDtypeStruct((bsz, s, d), BF16),
                   jax.ShapeDtypeStruct((bsz, N_EXPERTS, s), F32),
                   jax.ShapeDtypeStruct((bsz, N_EXPERTS, s), F32),
                   jax.ShapeDtypeStruct((bsz, s // MOE_TB, N_EXPERTS, LANES), F32)],
        compiler_params=_params("parallel", "parallel"),
        name="merge_router",
    )(x, mod3, y_a, y_b, proj, proj, w_up_gdn, w_up_dil, w_out, g_ffn, w_router_t, router_bias)


MOE_TB = 256
MOE_CH = 16
MOE_RB = MOE_TB * TOP_K + N_EXPERTS * MOE_CH
MOE_CPB = MOE_RB // MOE_CH
MOE_TM = 512
MOE_CT = MOE_RB
MOE_FT = 1024
MOE_CPT = MOE_FT // MOE_CH


def _moe_plan(cnt):
    nb, ne = cnt.shape
    i32 = jnp.int32
    nch = (cnt + MOE_CH - 1) // MOE_CH
    seg0 = jnp.cumsum(nch, axis=1) - nch
    pad = jnp.zeros((nb, LANES - ne), i32)
    meta = jnp.stack([jnp.concatenate([seg0 * MOE_CH, pad + MOE_RB], axis=1),
                      jnp.concatenate([nch * MOE_CH, pad], axis=1)], axis=1).astype(F32)
    meta = jnp.pad(meta, ((0, 0), (0, 6), (0, 0)))
    nch_e = nch.T
    cum_e = jnp.cumsum(nch_e, axis=1)
    tot_e = cum_e[:, -1]
    tiles_e = (tot_e + MOE_CPT - 1) // MOE_CPT
    tile_end = jnp.cumsum(tiles_e)
    ntiles = tile_end[-1]
    max_tiles = (nb * MOE_CPB) // MOE_CPT + ne
    i = jnp.arange(max_tiles, dtype=i32)
    te = jnp.sum((tile_end[None, :] <= i[:, None]).astype(i32), axis=1)
    te = jnp.minimum(te, jnp.sum((tile_end <= ntiles - 1).astype(i32)))
    te = jnp.minimum(te, ne - 1)
    oh_te = (te[:, None] == jnp.arange(ne, dtype=i32)[None, :]).astype(i32)
    pick = lambda tab: jnp.sum(oh_te[:, :, None] * tab[None, :, :], axis=1)
    tile_off_t = jnp.sum(oh_te * (tile_end - tiles_e)[None, :], axis=1)
    tot_t = jnp.sum(oh_te * tot_e[None, :], axis=1)
    cum_t, nch_t, seg0_t = pick(cum_e), pick(nch_e), pick(seg0.T)
    q = (i - tile_off_t)[:, None] * MOE_CPT + jnp.arange(MOE_CPT, dtype=i32)[None, :]
    valid = (q < tot_t[:, None]) & (i < ntiles)[:, None]
    blk = jnp.minimum(jnp.sum((cum_t[:, None, :] <= q[:, :, None]).astype(i32), axis=-1), nb - 1)
    oh_b = (blk[:, :, None] == jnp.arange(nb, dtype=i32)[None, None, :]).astype(i32)
    before = jnp.sum(oh_b * (cum_t - nch_t)[:, None, :], axis=-1)
    src = blk * MOE_CPB + jnp.sum(oh_b * seg0_t[:, None, :], axis=-1) + (q - before)
    spare = nb * MOE_CPB + (i % 2)[:, None] * MOE_CPT + jnp.arange(MOE_CPT, dtype=i32)[None, :]
    tbl_in = jnp.where(valid, src, nb * MOE_CPB + 2 * MOE_CPT).astype(i32).reshape(-1)
    tbl_out = jnp.where(valid, src, spare).astype(i32).reshape(-1)
    return meta, te.astype(i32), tbl_in, tbl_out, ntiles.astype(i32).reshape(1)


def _onehot_rows(meta_ref, rhs, r0, nrows, weighted):
    start = meta_ref[0, 0:1, :]
    plen = meta_ref[0, 1:2, :]
    r = (lax.broadcasted_iota(jnp.int32, (nrows, LANES), 0) + r0).astype(F32)
    owner = jnp.where((r >= start) & (r < start + plen), 1.0, 0.0)
    pos = r[:, :1] - jnp.sum(owner * start, axis=1, keepdims=True)
    got = jnp.dot(owner[:, :N_EXPERTS].astype(BF16), rhs, preferred_element_type=F32)
    tb = MOE_TB
    hit = (got[:, :tb] == pos) & (got[:, tb:2 * tb] > 0.5)
    return jnp.where(hit, got[:, 2 * tb:] if weighted else 1.0, 0.0).astype(BF16)


def _route_rhs(rank_ref, comb_ref):
    comb = comb_ref[0]
    picked = jnp.where(comb > 0.0, 1.0, 0.0)
    return jnp.concatenate([rank_ref[0], picked, comb], axis=1).astype(BF16)


def _dispatch_kernel(meta_ref, h_ref, rank_ref, comb_ref, x_ref):
    @pl.when(pl.program_id(0) < pl.num_programs(0) - 1)
    def _():
        rhs = _route_rhs(rank_ref, comb_ref)
        h = h_ref[0]
        for r0 in range(0, MOE_RB, MOE_TM):
            n = min(MOE_TM, MOE_RB - r0)
            p = _onehot_rows(meta_ref, rhs, r0, n, weighted=False)
            x_ref[0, r0:r0 + n, :] = jnp.dot(p, h, preferred_element_type=F32).astype(BF16)

    @pl.when(pl.program_id(0) == pl.num_programs(0) - 1)
    def _():
        x_ref[...] = jnp.zeros_like(x_ref)


def _dispatch(h2, rank_t, comb_t, meta):
    bsz, s, d = h2.shape
    nsb = s // MOE_TB
    nblk = bsz * nsb
    tok = lambda i: (jnp.minimum(i, nblk - 1) // nsb, jnp.minimum(i, nblk - 1) % nsb)
    ex = pl.BlockSpec((1, N_EXPERTS, MOE_TB), lambda i: (tok(i)[0], 0, tok(i)[1]))
    return pl.pallas_call(
        _dispatch_kernel,
        grid=(nblk + 1,),
        in_specs=[pl.BlockSpec((1, 8, LANES), lambda i: (jnp.minimum(i, nblk - 1), 0, 0)),
                  pl.BlockSpec((1, MOE_TB, d), lambda i: (tok(i)[0], tok(i)[1], 0)),
                  ex, ex],
        out_specs=pl.BlockSpec((1, MOE_RB, d), lambda i: (i, 0, 0)),
        out_shape=jax.ShapeDtypeStruct((nblk + 1, MOE_RB, d), BF16),
        compiler_params=_params("parallel"),
        name="moe_dispatch",
    )(meta, h2, rank_t, comb_t)


def _ffn_kernel(te_ref, tin_ref, tout_ref, nt_ref, x_hbm, wg_ref, wu_ref, wd_ref, y_hbm,
                xbuf, ybuf, wgb, wub, wdb, sem_in, sem_out):
    i = pl.program_id(0)
    nt = nt_ref[0]
    slot = i % 2

    def copy_in(t, s, j):
        return pltpu.make_async_copy(x_hbm.at[tin_ref[t * MOE_CPT + j]],
                                     xbuf.at[s, pl.ds(j * MOE_CH, MOE_CH)], sem_in.at[s])

    def copy_out(t, s, j):
        return pltpu.make_async_copy(ybuf.at[s, pl.ds(j * MOE_CH, MOE_CH)],
                                     y_hbm.at[tout_ref[t * MOE_CPT + j]], sem_out.at[s])

    @pl.when(i == 0)
    def _():
        for j in range(MOE_CPT):
            copy_in(0, 0, j).start()

    @pl.when(i + 1 < nt)
    def _():
        for j in range(MOE_CPT):
            copy_in(i + 1, 1 - slot, j).start()

    @pl.when(i < nt)
    def _():
        for j in range(MOE_CPT):
            copy_in(i, slot, j).wait()

        @pl.when(i >= 2)
        def _():
            for j in range(MOE_CPT):
                copy_out(i - 2, slot, j).wait()

        @pl.when(jnp.logical_or(i == 0, te_ref[i] != te_ref[jnp.maximum(i - 1, 0)]))
        def _():
            wgb[...] = wg_ref[0].astype(BF16)
            wub[...] = wu_ref[0].astype(BF16)
            wdb[...] = wd_ref[0].astype(BF16)

        x = xbuf[slot]
        a = jnp.dot(x, wgb[...], preferred_element_type=F32)
        u = jnp.dot(x, wub[...], preferred_element_type=F32)
        ybuf[slot] = jnp.dot((_silu(a) * u).astype(BF16), wdb[...], preferred_element_type=F32).astype(BF16)
        for j in range(MOE_CPT):
            copy_out(i, slot, j).start()

        @pl.when(i == nt - 1)
        def _():
            @pl.when(i >= 1)
            def _():
                for j in range(MOE_CPT):
                    copy_out(i - 1, 1 - slot, j).wait()
            for j in range(MOE_CPT):
                copy_out(i, slot, j).wait()


def _ffn(x_rows, w_eg, w_eu, w_ed, te, tbl_in, tbl_out, ntiles):
    nblk, rb, d = x_rows.shape
    ne, _, de = w_eg.shape
    xc = x_rows.reshape(nblk * MOE_CPB, MOE_CH, d)
    max_tiles = te.shape[0]
    y = pl.pallas_call(
        _ffn_kernel,
        grid_spec=pltpu.PrefetchScalarGridSpec(
            num_scalar_prefetch=4,
            grid=(max_tiles,),
            in_specs=[pl.BlockSpec(memory_space=pl.ANY),
                      pl.BlockSpec((1, d, de), lambda i, te, *_: (te[i], 0, 0)),
                      pl.BlockSpec((1, d, de), lambda i, te, *_: (te[i], 0, 0)),
                      pl.BlockSpec((1, de, d), lambda i, te, *_: (te[i], 0, 0))],
            out_specs=pl.BlockSpec(memory_space=pl.ANY),
            scratch_shapes=[pltpu.VMEM((2, MOE_FT, d), BF16), pltpu.VMEM((2, MOE_FT, d), BF16),
                            pltpu.VMEM((d, de), BF16), pltpu.VMEM((d, de), BF16), pltpu.VMEM((de, d), BF16),
                            pltpu.SemaphoreType.DMA((2,)), pltpu.SemaphoreType.DMA((2,))]),
        out_shape=jax.ShapeDtypeStruct(xc.shape, BF16),
        input_output_aliases={4: 0},
        compiler_params=_params("arbitrary"),
        name="moe_ffn",
    )(te, tbl_in, tbl_out, ntiles, xc, w_eg, w_eu, w_ed)
    return y.reshape(nblk, rb, d)


def _combine_kernel(meta_ref, y_ref, rank_ref, comb_ref, h_ref, x1_ref, mod_ref,
                    sg_ref, su_ref, sd_ref, o_ref):
    rhs = _route_rhs(rank_ref, comb_ref)
    h = h_ref[0]
    a = jnp.dot(h, sg_ref[...], preferred_element_type=F32)
    u = jnp.dot(h, su_ref[...], preferred_element_type=F32)
    acc = _mm(_silu(a) * u, sd_ref[...])
    for r0 in range(0, MOE_RB, MOE_CT):
        n = min(MOE_CT, MOE_RB - r0)
        p = _onehot_rows(meta_ref, rhs, r0, n, weighted=True)
        acc = acc + lax.dot_general(p, y_ref[0, r0:r0 + n, :], (((0,), (0,)), ((), ())),
                                    preferred_element_type=F32)
    o_ref[0] = x1_ref[0] + mod_ref[0, 5:6, :] * acc


def _combine(y_rows, rank_t, comb_t, h2, x1, mod3, sg, su, sd, meta):
    bsz, s, d = x1.shape
    nsb = s // MOE_TB
    tok = lambda w: pl.BlockSpec((1, MOE_TB, w), lambda b, i: (b, i, 0))
    ex = pl.BlockSpec((1, N_EXPERTS, MOE_TB), lambda b, i: (b, 0, i))
    full = lambda a: pl.BlockSpec(a.shape, lambda b, i: (0,) * a.ndim)
    return pl.pallas_call(
        _combine_kernel,
        grid=(bsz, nsb),
        in_specs=[pl.BlockSpec((1, 8, LANES), lambda b, i: (b * nsb + i, 0, 0)),
                  pl.BlockSpec((1, MOE_RB, d), lambda b, i: (b * nsb + i, 0, 0)),
                  ex, ex, tok(d), tok(d),
                  pl.BlockSpec((1, 6, d), lambda b, i: (b, 0, 0)),
                  full(sg), full(su), full(sd)],
        out_specs=tok(d),
        out_shape=jax.ShapeDtypeStruct((bsz, s, d), F32),
        compiler_params=_params("parallel", "parallel"),
        name="moe_combine",
    )(meta, y_rows, rank_t, comb_t, h2, x1, mod3, sg, su, sd)


def _moe(h2, comb_t, rank_t, cnt, x1, mod3, w_eg, w_eu, w_ed, sg, su, sd):
    meta, te, tbl_in, tbl_out, ntiles = _moe_plan(cnt)
    x_rows = _dispatch(h2, rank_t, comb_t, meta)
    y_rows = _ffn(x_rows, w_eg, w_eu, w_ed, te, tbl_in, tbl_out, ntiles)
    return _combine(y_rows, rank_t, comb_t, h2, x1, mod3, sg, su, sd, meta)


def _layer(x, cmod, g_mix, w_in, conv_w, a_log, dt_bias, norm_g, qn_g, kn_g, w_up_gdn, w_up_dil, w_out,
           g_ffn, w_router, router_bias, w_eg, w_eu, w_ed, w_sg, w_su, w_sd):
    bsz, s, d = x.shape
    mod3 = cmod.reshape(bsz, 6, d)
    o_ba = 4 * GDN_W
    o_dq = o_ba + 2 * GDN_HEADS
    o_ga = o_dq + 3 * DIL_W
    w_in16 = w_in.astype(BF16)
    w_main = jnp.concatenate([w_in16[:, :o_ba], w_in16[:, o_ga:], w_in16[:, o_dq:o_ga]], axis=1)
    pad_hi = LANES - 2 * GDN_HEADS
    w_ba = jnp.pad(w_in16[:, o_ba:o_dq], ((0, 0), (0, pad_hi)))
    alog_vec = jnp.pad(a_log, (GDN_HEADS, pad_hi)).reshape(1, LANES)
    dtb_vec = jnp.pad(dt_bias, (GDN_HEADS, pad_hi)).reshape(1, LANES)
    proj, gates = _inproj(x, mod3, g_mix, w_main, w_ba, alog_vec, dtb_vec)

    conv_w3 = conv_w.reshape(GDN_CONV, 3, GDN_W).transpose(1, 0, 2)
    y_a = _gdn(proj, gates, conv_w3, norm_g)
    y_b = _dil(proj, qn_g, kn_g)

    wr_t = w_router.T
    wr_hi = wr_t.astype(BF16)
    w_router_split = jnp.concatenate([wr_hi, (wr_t - wr_hi.astype(F32)).astype(BF16)], axis=0)
    x1, h2, comb_t, rank_t, cnt = _merge(x, mod3, y_a, y_b, proj, w_up_gdn.astype(BF16),
                                         w_up_dil.astype(BF16), w_out.astype(BF16), g_ffn.reshape(1, d),
                                         w_router_split, router_bias.reshape(N_EXPERTS, 1))
    cnt = cnt[..., 0].astype(jnp.int32).reshape(-1, N_EXPERTS)
    return _moe(h2, comb_t, rank_t, cnt, x1, mod3, w_eg, w_eu, w_ed,
                w_sg.astype(BF16), w_su.astype(BF16), w_sd.astype(BF16))


def kernel(x, c, w_ada, b_ada, g_mix, w_in, gdn_conv_w, gdn_a_log, gdn_dt_bias, gdn_norm_g, dil_q_norm_g, dil_k_norm_g, w_up_gdn, w_up_dil, w_out, g_ffn, w_router, router_bias, w_exp_gate, w_exp_up, w_exp_down, w_sh_gate, w_sh_up, w_sh_down):
    for l in range(w_ada.shape[0]):
        cmod = _ada(c, w_ada[l], b_ada[l])
        x = _layer(x, cmod, g_mix[l], w_in[l], gdn_conv_w[l], gdn_a_log[l], gdn_dt_bias[l], gdn_norm_g[l],
                   dil_q_norm_g[l], dil_k_norm_g[l], w_up_gdn[l], w_up_dil[l], w_out[l], g_ffn[l],
                   w_router[l], router_bias[l], w_exp_gate[l], w_exp_up[l], w_exp_down[l],
                   w_sh_gate[l], w_sh_up[l], w_sh_down[l])
    return x
```

```python
import functools

import jax
import jax.numpy as jnp
from jax import lax
from jax.experimental import pallas as pl
from jax.experimental.pallas import tpu as pltpu

F32 = jnp.float32
BF16 = jnp.bfloat16

D_MODEL = 1024
GDN_HEADS = 8
GDN_DK = 128
GDN_CONV = 4
DIL_PATTERN = ((128, 1), (512, 4), (2048, 16))
DIL_HEADS_PER_GROUP = 4
DIL_HD = 128
DIL_BLOCK = 128
DIL_BATCH = 8
N_EXPERTS = 64
TOP_K = 8
N_GROUPS = 8
TOPK_GROUPS = 4
D_EXPERT = 256
ROUTE_SCALE = 2.5
EPS = 1e-6

GDN_W = GDN_HEADS * GDN_DK
DIL_HEADS = len(DIL_PATTERN) * DIL_HEADS_PER_GROUP
DIL_W = DIL_HEADS * DIL_HD
DIL_OUT_W = DIL_HEADS_PER_GROUP * DIL_HD
COL_GQ, COL_GK, COL_GV, COL_GZ = 0, GDN_W, 2 * GDN_W, 3 * GDN_W
COL_GA = 4 * GDN_W
COL_GB = COL_GA + D_MODEL
COL_DQ = COL_GB + D_MODEL
COL_DK = COL_DQ + DIL_W
COL_DV = COL_DK + DIL_W
PROJ_W = COL_DV + DIL_W
LANES = 128

VMEM_LIMIT = 56 * 1024 * 1024


def _params(*sem):
    return pltpu.CompilerParams(dimension_semantics=sem, vmem_limit_bytes=VMEM_LIMIT)


def _sigmoid(x):
    return 1.0 / (1.0 + jnp.exp(-x))


def _silu(x):
    return x * _sigmoid(x)


def _softplus(x):
    return jnp.maximum(x, 0.0) + jnp.log(1.0 + jnp.exp(-jnp.abs(x)))


def _mm(a, b):
    return jnp.dot(a.astype(BF16), b.astype(BF16), preferred_element_type=F32)


def _mm_nt(a, b):
    return lax.dot_general(a.astype(BF16), b.astype(BF16), (((1,), (1,)), ((), ())),
                           preferred_element_type=F32)


def _mm_tn(a, b):
    return lax.dot_general(a.astype(BF16), b.astype(BF16), (((0,), (0,)), ((), ())),
                           preferred_element_type=F32)


def _mm_f32(a, b):
    return jnp.dot(a, b, preferred_element_type=F32, precision=lax.Precision.HIGHEST)


def _ada_kernel(c_ref, w_ref, b_ref, o_ref):
    cs = _silu(c_ref[...])
    o_ref[...] = _mm_f32(cs, w_ref[...]) + b_ref[...]


def _ada(c, w_ada, b_ada):
    bsz, d = c.shape
    n = w_ada.shape[1]
    tn = 1536
    return pl.pallas_call(
        _ada_kernel,
        grid=(n // tn,),
        in_specs=[pl.BlockSpec((bsz, d), lambda j: (0, 0)),
                  pl.BlockSpec((d, tn), lambda j: (0, j)),
                  pl.BlockSpec((1, tn), lambda j: (0, j))],
        out_specs=pl.BlockSpec((bsz, tn), lambda j: (0, j)),
        out_shape=jax.ShapeDtypeStruct((bsz, n), F32),
        compiler_params=_params("parallel"),
        name="ada",
    )(c, w_ada, b_ada.reshape(1, n))


def _inproj_kernel(x_ref, mod_ref, g_ref, w_ref, wba_ref, alog_ref, dtb_ref,
                   proj_ref, gates_ref, h_scr):
    j = pl.program_id(2)

    @pl.when(j == 0)
    def _():
        x = x_ref[0]
        sh1 = mod_ref[0, 0:1, :]
        sc1 = mod_ref[0, 1:2, :]
        ms = jnp.mean(x * x, axis=-1, keepdims=True)
        h = x * lax.rsqrt(ms + EPS) * g_ref[...] * (1.0 + sc1) + sh1
        hb = h.astype(BF16)
        h_scr[...] = hb
        ba = jnp.dot(hb, wba_ref[...], preferred_element_type=F32)
        lane = lax.broadcasted_iota(jnp.int32, ba.shape, 1)
        beta = _sigmoid(ba)
        g = -jnp.exp(alog_ref[...]) * _softplus(ba + dtb_ref[...])
        gates_ref[0] = jnp.where(lane < GDN_HEADS, beta, g)

    proj_ref[0] = jnp.dot(h_scr[...], w_ref[...], preferred_element_type=F32).astype(BF16)


def _inproj(x, mod3, g_mix, w_main, w_ba, alog_vec, dtb_vec):
    bsz, s, d = x.shape
    tm, tn = 1024, 2688
    return pl.pallas_call(
        _inproj_kernel,
        grid=(bsz, s // tm, PROJ_W // tn),
        in_specs=[pl.BlockSpec((1, tm, d), lambda b, i, j: (b, i, 0)),
                  pl.BlockSpec((1, 6, d), lambda b, i, j: (b, 0, 0)),
                  pl.BlockSpec((1, d), lambda b, i, j: (0, 0)),
                  pl.BlockSpec((d, tn), lambda b, i, j: (0, j)),
                  pl.BlockSpec((d, LANES), lambda b, i, j: (0, 0)),
                  pl.BlockSpec((1, LANES), lambda b, i, j: (0, 0)),
                  pl.BlockSpec((1, LANES), lambda b, i, j: (0, 0))],
        out_specs=[pl.BlockSpec((1, tm, tn), lambda b, i, j: (b, i, j)),
                   pl.BlockSpec((1, tm, LANES), lambda b, i, j: (b, i, 0))],
        out_shape=[jax.ShapeDtypeStruct((bsz, s, PROJ_W), BF16),
                   jax.ShapeDtypeStruct((bsz, s, LANES), F32)],
        scratch_shapes=[pltpu.VMEM((tm, d), BF16)],
        compiler_params=_params("parallel", "parallel", "arbitrary"),
        name="inproj",
    )(x, mod3, g_mix.reshape(1, d), w_main, w_ba, alog_vec, dtb_vec)


GDN_C = 128
GDN_HG = 8
GDN_TAIL = 8


def _cumsum_rows(g):
    row = lax.broadcasted_iota(jnp.int32, g.shape, 0)
    sft = 1
    while sft < g.shape[0]:
        g = g + jnp.where(row >= sft, pltpu.roll(g, sft, 0), 0.0)
        sft *= 2
    return g


def _unit_lower_inverses(mats):
    n = mats[0].shape[0]
    row = lax.broadcasted_iota(jnp.int32, (n, n), 0)
    col = lax.broadcasted_iota(jnp.int32, (n, n), 1)
    eye = (row == col).astype(F32)
    ts = None
    b = 1
    while b < n:
        off = ((row // (2 * b)) == (col // (2 * b))) & ((row // b) % 2 == 1) & ((col // b) % 2 == 0)
        if b == 1:
            ts = [eye - jnp.where(off, a, 0.0) for a in mats]
        else:
            tb = [t.astype(BF16) for t in ts]
            inner = [jnp.dot(jnp.where(off, a, 0.0).astype(BF16), t, preferred_element_type=F32)
                     for a, t in zip(mats, tb)]
            ts = [t - jnp.dot(t16, i.astype(BF16), preferred_element_type=F32)
                  for t, t16, i in zip(ts, tb, inner)]
        b *= 2
    return ts


def _gdn_kernel(q_ref, k_ref, v_ref, z_ref, gates_ref, cw_ref, ng_ref, o_ref,
                qprev, kprev, vprev, state):
    sblk = pl.program_id(1)
    c = GDN_C
    nh = GDN_HEADS
    w = nh * GDN_DK

    @pl.when(sblk == 0)
    def _():
        state[...] = jnp.zeros_like(state)
        for buf in (qprev, kprev, vprev):
            buf[...] = jnp.zeros_like(buf)

    taps = GDN_CONV - 1
    si = lax.broadcasted_iota(jnp.int32, (taps * c, c), 0)
    sm = lax.broadcasted_iota(jnp.int32, (taps * c, c), 1)
    shift_mat = jnp.where(sm == si % c - (taps - si // c), 1.0, 0.0).astype(BF16)
    trow = lax.broadcasted_iota(jnp.int32, (GDN_TAIL, w), 0)
    conv = []
    for idx, (ref, prev) in enumerate(((q_ref, qprev), (k_ref, kprev), (v_ref, vprev))):
        cur = ref[0]
        cur32 = cur.astype(F32)
        delayed = jnp.dot(shift_mat, cur, preferred_element_type=F32)
        acc = cur32 * cw_ref[idx, taps:taps + 1, :]
        head = jnp.zeros((GDN_TAIL, w), F32)
        tail = prev[...]
        for j in range(taps):
            acc = acc + delayed[j * c:(j + 1) * c, :] * cw_ref[idx, j:j + 1, :]
            head = head + jnp.where(trow < taps - j, pltpu.roll(tail, taps - j, 0), 0.0) * cw_ref[idx, j:j + 1, :]
        acc = jnp.concatenate([acc[:GDN_TAIL] + head, acc[GDN_TAIL:]], axis=0)
        prev[...] = cur32[c - GDN_TAIL:, :]
        conv.append(_silu(acc))
    qc, kc, vc = conv

    gates = gates_ref[0]
    gcum = _cumsum_rows(gates)
    row = lax.broadcasted_iota(jnp.int32, (c, c), 0)
    col = lax.broadcasted_iota(jnp.int32, (c, c), 1)
    eye = row == col
    incl = col <= row
    strict = col < row

    for hg in range(0, nh, GDN_HG):
        heads = range(hg, hg + GDN_HG)
        hsl = {h: slice(h * GDN_DK, (h + 1) * GDN_DK) for h in heads}
        q = [qc[:, hsl[h]] for h in heads]
        k = [kc[:, hsl[h]] for h in heads]
        v = [vc[:, hsl[h]] for h in heads]
        q = [x * lax.rsqrt(jnp.sum(x * x, axis=-1, keepdims=True) + EPS) * (GDN_DK ** -0.5) for x in q]
        k = [x * lax.rsqrt(jnp.sum(x * x, axis=-1, keepdims=True) + EPS) for x in k]
        beta = [jnp.broadcast_to(gates[:, h:h + 1], (c, GDN_DK)) for h in heads]
        gc = [jnp.broadcast_to(gcum[:, nh + h:nh + h + 1], (c, GDN_DK)) for h in heads]
        gc_row = [jnp.sum(jnp.where(eye, x, 0.0), axis=0, keepdims=True) for x in gc]
        decay = [jnp.exp(jnp.where(incl, x - y, -jnp.inf)) for x, y in zip(gc, gc_row)]
        egc = [jnp.exp(x) for x in gc]
        kb = [x * b for x, b in zip(k, beta)]
        scores = [_mm_nt(jnp.concatenate([a, b], axis=0), x) for a, b, x in zip(kb, q, k)]
        a_mat = [jnp.where(strict, s[:c] * d, 0.0) for s, d in zip(scores, decay)]
        qk = [s[c:] * d for s, d in zip(scores, decay)]
        t_inv = _unit_lower_inverses(a_mat)
        sol = [_mm(t, jnp.concatenate([x * b, y * e], axis=1))
               for t, x, b, y, e in zip(t_inv, v, beta, kb, egc)]

        st = [state[h] for h in heads]
        ws = [_mm(jnp.concatenate([s[:, GDN_DK:], x * e], axis=0), m)
              for s, x, e, m in zip(sol, q, egc, st)]
        v_new = [s[:, :GDN_DK] - x[:c] for s, x in zip(sol, ws)]
        o = [x[c:] + _mm(a, b) for x, a, b in zip(ws, qk, v_new)]
        gc_last = [x[c - 1:c, :] for x in gc]
        k_dec = [x * jnp.exp(l - g) for x, l, g in zip(k, gc_last, gc)]
        for n, h in enumerate(heads):
            state[h] = st[n] * jnp.exp(gc_last[n]) + _mm_tn(k_dec[n], v_new[n])
            y = o[n] * lax.rsqrt(jnp.mean(o[n] * o[n], axis=-1, keepdims=True) + EPS) * ng_ref[...]
            o_ref[0, :, hsl[h]] = (y * _silu(z_ref[0, :, hsl[h]].astype(F32))).astype(o_ref.dtype)


def _gdn(proj, gates, conv_w3, norm_g):
    bsz, s, _ = proj.shape
    w = GDN_W

    def col_spec(col0):
        return pl.BlockSpec((1, GDN_C, w), lambda b, i: (b, i, col0 // w))

    return pl.pallas_call(
        _gdn_kernel,
        grid=(bsz, s // GDN_C),
        in_specs=[col_spec(COL_GQ), col_spec(COL_GK), col_spec(COL_GV), col_spec(COL_GZ),
                  pl.BlockSpec((1, GDN_C, LANES), lambda b, i: (b, i, 0)),
                  pl.BlockSpec((3, GDN_CONV, w), lambda b, i: (0, 0, 0)),
                  pl.BlockSpec((1, GDN_DK), lambda b, i: (0, 0))],
        out_specs=pl.BlockSpec((1, GDN_C, w), lambda b, i: (b, i, 0)),
        out_shape=jax.ShapeDtypeStruct((bsz, s, GDN_W), BF16),
        scratch_shapes=[pltpu.VMEM((GDN_TAIL, w), F32),
                        pltpu.VMEM((GDN_TAIL, w), F32),
                        pltpu.VMEM((GDN_TAIL, w), F32),
                        pltpu.VMEM((GDN_HEADS, GDN_DK, GDN_DK), F32)],
        compiler_params=_params("parallel", "arbitrary"),
        name="gdn",
    )(proj, proj, proj, proj, gates, conv_w3, norm_g.reshape(1, GDN_DK))


def _dil_kernel(q0, q1, q2, k0, k1, k2, v0, v1, v2, qg_ref, kg_ref, o_ref, qs, ks, vs, os_, ls):
    s = qs.shape[1]
    blk = DIL_BLOCK
    ngrp = len(DIL_PATTERN)
    for gi, (q_ref, k_ref, v_ref) in enumerate(((q0, k0, v0), (q1, k1, v1), (q2, k2, v2))):
        q = q_ref[0].astype(F32)
        k = k_ref[0].astype(F32)
        q = q * lax.rsqrt(jnp.mean(q * q, axis=-1, keepdims=True) + EPS) * qg_ref[...]
        k = k * lax.rsqrt(jnp.mean(k * k, axis=-1, keepdims=True) + EPS) * kg_ref[...]
        qs[gi] = q * (DIL_HD ** -0.5)
        ks[gi] = k
        vs[gi] = v_ref[0].astype(F32)

    qi = lax.broadcasted_iota(jnp.int32, (blk, blk), 0)
    kj = lax.broadcasted_iota(jnp.int32, (blk, blk), 1)
    cur_ok = kj <= qi
    prev_ok = kj >= qi

    items = []
    for gi, (win, dil) in enumerate(DIL_PATTERN):
        assert win // dil == blk and (s // dil) % blk == 0
        for r in range(dil):
            for n in range((s // dil) // blk):
                items.append((gi, dil, r, n))

    def rows(dil, r, m):
        if dil == 1:
            return pl.ds(m * blk, blk)
        return pl.ds(m * blk * dil + r, blk, stride=dil)

    for b0 in range(0, len(items), DIL_BATCH):
        batch = items[b0:b0 + DIL_BATCH]
        qb = [qs[gi, rows(dil, r, n), :] for gi, dil, r, n in batch]
        kc = [ks[gi, rows(dil, r, n), :] for gi, dil, r, n in batch]
        kp = [ks[gi, rows(dil, r, n - 1), :] if n > 0 else None for gi, dil, r, n in batch]
        s_cur = [jnp.where(cur_ok, _mm_nt(q, k), -jnp.inf) for q, k in zip(qb, kc)]
        s_prev = [None if k is None else jnp.where(prev_ok, _mm_nt(q, k), -jnp.inf) for q, k in zip(qb, kp)]
        m = [jnp.max(a, axis=-1, keepdims=True) if b is None else
             jnp.maximum(jnp.max(a, axis=-1, keepdims=True), jnp.max(b, axis=-1, keepdims=True))
             for a, b in zip(s_cur, s_prev)]
        p_cur = [jnp.exp(a - mm) for a, mm in zip(s_cur, m)]
        p_prev = [None if b is None else jnp.exp(b - mm) for b, mm in zip(s_prev, m)]
        den = [jnp.sum(a, axis=-1, keepdims=True) if b is None else
               jnp.sum(a, axis=-1, keepdims=True) + jnp.sum(b, axis=-1, keepdims=True)
               for a, b in zip(p_cur, p_prev)]
        o = [_mm(p, vs[gi, rows(dil, r, n), :]) for p, (gi, dil, r, n) in zip(p_cur, batch)]
        o = [a if p is None else a + _mm(p, vs[gi, rows(dil, r, n - 1), :])
             for a, p, (gi, dil, r, n) in zip(o, p_prev, batch)]
        for (gi, dil, r, n), a, d, mm in zip(batch, o, den, m):
            os_[gi, rows(dil, r, n), :] = a / d
            ls[gi, rows(dil, r, n), :] = jnp.broadcast_to(mm + jnp.log(d), (blk, DIL_HD))

    lse = [ls[gi] for gi in range(ngrp)]
    mx = functools.reduce(jnp.maximum, lse)
    ex = [jnp.exp(l - mx) for l in lse]
    tot = functools.reduce(lambda a, b: a + b, ex)
    y = functools.reduce(lambda a, b: a + b, [ex[gi] / tot * os_[gi] for gi in range(ngrp)])
    o_ref[0] = y.astype(o_ref.dtype)


def _dil(proj, q_norm_g, k_norm_g):
    bsz, s, _ = proj.shape
    ngrp = len(DIL_PATTERN)

    def specs(col0):
        blk0 = col0 // DIL_HD
        return [pl.BlockSpec((1, s, DIL_HD), lambda b, h, g=g: (b, 0, blk0 + g * DIL_HEADS_PER_GROUP + h))
                for g in range(ngrp)]

    gain = pl.BlockSpec((1, DIL_HD), lambda b, h: (0, 0))
    return pl.pallas_call(
        _dil_kernel,
        grid=(bsz, DIL_HEADS_PER_GROUP),
        in_specs=specs(COL_DQ) + specs(COL_DK) + specs(COL_DV) + [gain, gain],
        out_specs=pl.BlockSpec((1, s, DIL_HD), lambda b, h: (b, 0, h)),
        out_shape=jax.ShapeDtypeStruct((bsz, s, DIL_OUT_W), BF16),
        scratch_shapes=[pltpu.VMEM((ngrp, s, DIL_HD), F32) for _ in range(5)],
        compiler_params=_params("parallel", "parallel"),
        name="dilattn",
    )(*([proj] * 9), q_norm_g.reshape(1, DIL_HD), k_norm_g.reshape(1, DIL_HD))


def _route(logits, bias):
    e, tm = logits.shape
    per = e // N_GROUPS
    scores = _sigmoid(logits)
    sel = scores + bias
    neg = -jnp.inf
    sub = lax.broadcasted_iota(jnp.int32, (per, tm), 0)
    gs_rows = []
    for g in range(N_GROUPS):
        blk = sel[g * per:(g + 1) * per, :]
        m1 = jnp.max(blk, axis=0, keepdims=True)
        i1 = jnp.min(jnp.where(blk == m1, sub, per), axis=0, keepdims=True)
        m2 = jnp.max(jnp.where(sub == i1, neg, blk), axis=0, keepdims=True)
        gs_rows.append(m1 + m2)
    gs = jnp.concatenate(gs_rows, axis=0)
    gi = lax.broadcasted_iota(jnp.int32, (N_GROUPS, tm), 0)
    gsel = jnp.zeros((N_GROUPS, tm), F32)
    for _ in range(TOPK_GROUPS):
        m = jnp.max(gs, axis=0, keepdims=True)
        idx = jnp.min(jnp.where(gs == m, gi, N_GROUPS), axis=0, keepdims=True)
        hit = gi == idx
        gsel = jnp.where(hit, 1.0, gsel)
        gs = jnp.where(hit, neg, gs)
    cand = jnp.concatenate(
        [jnp.where(gsel[g:g + 1, :] > 0.0, sel[g * per:(g + 1) * per, :], neg) for g in range(N_GROUPS)], axis=0)
    ei = lax.broadcasted_iota(jnp.int32, (e, tm), 0)
    chosen = jnp.zeros((e, tm), F32)
    for _ in range(TOP_K):
        m = jnp.max(cand, axis=0, keepdims=True)
        idx = jnp.min(jnp.where(cand == m, ei, e), axis=0, keepdims=True)
        hit = ei == idx
        chosen = jnp.where(hit, scores, chosen)
        cand = jnp.where(hit, neg, cand)
    return chosen / jnp.sum(chosen, axis=0, keepdims=True) * ROUTE_SCALE


def _merge_kernel(x_ref, mod_ref, ya_ref, yb_ref, ga_ref, gb_ref, wg_ref, wd_ref, wo_ref,
                  gf_ref, wr_ref, rb_ref, x1_ref, h2_ref, comb_ref, rank_ref, cnt_ref):
    gt1 = mod_ref[0, 2:3, :]
    sh2 = mod_ref[0, 3:4, :]
    sc2 = mod_ref[0, 4:5, :]
    ua = jnp.dot(ya_ref[0], wg_ref[...], preferred_element_type=F32)
    ub = jnp.dot(yb_ref[0], wd_ref[...], preferred_element_type=F32)
    merged = _sigmoid(ga_ref[0].astype(F32)) * ua + _sigmoid(gb_ref[0].astype(F32)) * ub
    x1 = x_ref[0] + gt1 * _mm(merged, wo_ref[...])
    x1_ref[0] = x1
    h2 = x1 * lax.rsqrt(jnp.mean(x1 * x1, axis=-1, keepdims=True) + EPS) * gf_ref[...] * (1.0 + sc2) + sh2
    h2_hi = h2.astype(BF16)
    h2_ref[0] = h2_hi
    h2_lo = (h2 - h2_hi.astype(F32)).astype(BF16)
    nt_dims = (((1,), (1,)), ((), ()))
    both = lax.dot_general(wr_ref[...], h2_hi, nt_dims, preferred_element_type=F32)
    logits = (both[:N_EXPERTS] + both[N_EXPERTS:]
              + lax.dot_general(wr_ref[:N_EXPERTS, :], h2_lo, nt_dims, preferred_element_type=F32))
    comb = _route(logits, rb_ref[...])
    comb_ref[0] = comb
    tb = MOE_TB
    before = (lax.broadcasted_iota(jnp.int32, (tb, tb), 0) < lax.broadcasted_iota(jnp.int32, (tb, tb), 1))
    before = before.astype(BF16)
    for j in range(comb.shape[1] // tb):
        picked = (comb[:, j * tb:(j + 1) * tb] > 0.0).astype(F32)
        rank_ref[0, :, j * tb:(j + 1) * tb] = jnp.dot(picked.astype(BF16), before, preferred_element_type=F32)
        cnt_ref[0, j] = jnp.broadcast_to(jnp.sum(picked, axis=1, keepdims=True), (N_EXPERTS, LANES))


def _merge(x, mod3, y_a, y_b, proj, w_up_gdn, w_up_dil, w_out, g_ffn, w_router_t, router_bias):
    bsz, s, d = x.shape
    tm = 512
    row = lambda w: pl.BlockSpec((1, tm, w), lambda b, i: (b, i, 0))
    full = lambda a: pl.BlockSpec(a.shape, lambda b, i: (0,) * a.ndim)
    ex = pl.BlockSpec((1, N_EXPERTS, tm), lambda b, i: (b, 0, i))
    return pl.pallas_call(
        _merge_kernel,
        grid=(bsz, s // tm),
        in_specs=[row(d),
                  pl.BlockSpec((1, 6, d), lambda b, i: (b, 0, 0)),
                  row(GDN_W), row(DIL_OUT_W),
                  pl.BlockSpec((1, tm, d), lambda b, i: (b, i, COL_GA // d)),
                  pl.BlockSpec((1, tm, d), lambda b, i: (b, i, COL_GB // d)),
                  full(w_up_gdn), full(w_up_dil), full(w_out), full(g_ffn), full(w_router_t),
                  full(router_bias)],
        out_specs=[row(d), row(d), ex, ex,
                   pl.BlockSpec((1, tm // MOE_TB, N_EXPERTS, LANES), lambda b, i: (b, i, 0, 0))],
        out_shape=[jax.ShapeDtypeStruct((bsz, s, d), F32),
                   jax.ShapeDtypeStruct((bsz, s, d), BF16),
                   jax.ShapeDtypeStruct((bsz, N_EXPERTS, s), F32),
                   jax.ShapeDtypeStruct((bsz, N_EXPERTS, s), F32),
                   jax.ShapeDtypeStruct((bsz, s // MOE_TB, N_EXPERTS, LANES), F32)],
        compiler_params=_params("parallel", "parallel"),
        name="merge_router",
    )(x, mod3, y_a, y_b, proj, proj, w_up_gdn, w_up_dil, w_out, g_ffn, w_router_t, router_bias)


MOE_TB = 256
MOE_CH = 16
MOE_RB = MOE_TB * TOP_K + N_EXPERTS * MOE_CH
MOE_CPB = MOE_RB // MOE_CH
MOE_TM = 512
MOE_CT = MOE_RB
MOE_FT = 1024
MOE_CPT = MOE_FT // MOE_CH


def _moe_plan(cnt):
    nb, ne = cnt.shape
    i32 = jnp.int32
    nch = (cnt + MOE_CH - 1) // MOE_CH
    seg0 = jnp.cumsum(nch, axis=1) - nch
    pad = jnp.zeros((nb, LANES - ne), i32)
    meta = jnp.stack([jnp.concatenate([seg0 * MOE_CH, pad + MOE_RB], axis=1),
                      jnp.concatenate([nch * MOE_CH, pad], axis=1)], axis=1).astype(F32)
    meta = jnp.pad(meta, ((0, 0), (0, 6), (0, 0)))
    nch_e = nch.T
    cum_e = jnp.cumsum(nch_e, axis=1)
    tot_e = cum_e[:, -1]
    tiles_e = (tot_e + MOE_CPT - 1) // MOE_CPT
    tile_end = jnp.cumsum(tiles_e)
    ntiles = tile_end[-1]
    max_tiles = (nb * MOE_CPB) // MOE_CPT + ne
    i = jnp.arange(max_tiles, dtype=i32)
    te = jnp.sum((tile_end[None, :] <= i[:, None]).astype(i32), axis=1)
    te = jnp.minimum(te, jnp.sum((tile_end <= ntiles - 1).astype(i32)))
    te = jnp.minimum(te, ne - 1)
    oh_te = (te[:, None] == jnp.arange(ne, dtype=i32)[None, :]).astype(i32)
    pick = lambda tab: jnp.sum(oh_te[:, :, None] * tab[None, :, :], axis=1)
    tile_off_t = jnp.sum(oh_te * (tile_end - tiles_e)[None, :], axis=1)
    tot_t = jnp.sum(oh_te * tot_e[None, :], axis=1)
    cum_t, nch_t, seg0_t = pick(cum_e), pick(nch_e), pick(seg0.T)
    q = (i - tile_off_t)[:, None] * MOE_CPT + jnp.arange(MOE_CPT, dtype=i32)[None, :]
    valid = (q < tot_t[:, None]) & (i < ntiles)[:, None]
    blk = jnp.minimum(jnp.sum((cum_t[:, None, :] <= q[:, :, None]).astype(i32), axis=-1), nb - 1)
    oh_b = (blk[:, :, None] == jnp.arange(nb, dtype=i32)[None, None, :]).astype(i32)
    before = jnp.sum(oh_b * (cum_t - nch_t)[:, None, :], axis=-1)
    src = blk * MOE_CPB + jnp.sum(oh_b * seg0_t[:, None, :], axis=-1) + (q - before)
    spare = nb * MOE_CPB + (i % 2)[:, None] * MOE_CPT + jnp.arange(MOE_CPT, dtype=i32)[None, :]
    tbl_in = jnp.where(valid, src, nb * MOE_CPB + 2 * MOE_CPT).astype(i32).reshape(-1)
    tbl_out = jnp.where(valid, src, spare).astype(i32).reshape(-1)
    return meta, te.astype(i32), tbl_in, tbl_out, ntiles.astype(i32).reshape(1)


def _onehot_rows(meta_ref, rhs, r0, nrows, weighted):
    start = meta_ref[0, 0:1, :]
    plen = meta_ref[0, 1:2, :]
    r = (lax.broadcasted_iota(jnp.int32, (nrows, LANES), 0) + r0).astype(F32)
    owner = jnp.where((r >= start) & (r < start + plen), 1.0, 0.0)
    pos = r[:, :1] - jnp.sum(owner * start, axis=1, keepdims=True)
    got = jnp.dot(owner[:, :N_EXPERTS].astype(BF16), rhs, preferred_element_type=F32)
    tb = MOE_TB
    hit = (got[:, :tb] == pos) & (got[:, tb:2 * tb] > 0.5)
    return jnp.where(hit, got[:, 2 * tb:] if weighted else 1.0, 0.0).astype(BF16)


def _route_rhs(rank_ref, comb_ref, weighted):
    comb = comb_ref[0]
    picked = jnp.where(comb > 0.0, 1.0, 0.0)
    parts = [rank_ref[0], picked] + ([comb] if weighted else [])
    return jnp.concatenate(parts, axis=1).astype(BF16)


def _dispatch_kernel(meta_ref, h_ref, rank_ref, comb_ref, x_ref):
    @pl.when(pl.program_id(0) < pl.num_programs(0) - 1)
    def _():
        rhs = _route_rhs(rank_ref, comb_ref, weighted=False)
        h = h_ref[0]
        for r0 in range(0, MOE_RB, MOE_TM):
            n = min(MOE_TM, MOE_RB - r0)
            p = _onehot_rows(meta_ref, rhs, r0, n, weighted=False)
            x_ref[0, r0:r0 + n, :] = jnp.dot(p, h, preferred_element_type=F32).astype(BF16)

    @pl.when(pl.program_id(0) == pl.num_programs(0) - 1)
    def _():
        x_ref[...] = jnp.zeros_like(x_ref)


def _dispatch(h2, rank_t, comb_t, meta):
    bsz, s, d = h2.shape
    nsb = s // MOE_TB
    nblk = bsz * nsb
    tok = lambda i: (jnp.minimum(i, nblk - 1) // nsb, jnp.minimum(i, nblk - 1) % nsb)
    ex = pl.BlockSpec((1, N_EXPERTS, MOE_TB), lambda i: (tok(i)[0], 0, tok(i)[1]))
    return pl.pallas_call(
        _dispatch_kernel,
        grid=(nblk + 1,),
        in_specs=[pl.BlockSpec((1, 8, LANES), lambda i: (jnp.minimum(i, nblk - 1), 0, 0)),
                  pl.BlockSpec((1, MOE_TB, d), lambda i: (tok(i)[0], tok(i)[1], 0)),
                  ex, ex],
        out_specs=pl.BlockSpec((1, MOE_RB, d), lambda i: (i, 0, 0)),
        out_shape=jax.ShapeDtypeStruct((nblk + 1, MOE_RB, d), BF16),
        compiler_params=_params("parallel"),
        name="moe_dispatch",
    )(meta, h2, rank_t, comb_t)


def _ffn_kernel(te_ref, tin_ref, tout_ref, nt_ref, x_hbm, wg_ref, wu_ref, wd_ref, y_hbm,
                xbuf, ybuf, wgb, wub, wdb, sem_in, sem_out):
    i = pl.program_id(0)
    nt = nt_ref[0]
    slot = i % 2

    def copy_in(t, s, j):
        return pltpu.make_async_copy(x_hbm.at[tin_ref[t * MOE_CPT + j]],
                                     xbuf.at[s, pl.ds(j * MOE_CH, MOE_CH)], sem_in.at[s])

    def copy_out(t, s, j):
        return pltpu.make_async_copy(ybuf.at[s, pl.ds(j * MOE_CH, MOE_CH)],
                                     y_hbm.at[tout_ref[t * MOE_CPT + j]], sem_out.at[s])

    @pl.when(i == 0)
    def _():
        for j in range(MOE_CPT):
            copy_in(0, 0, j).start()

    @pl.when(i + 1 < nt)
    def _():
        for j in range(MOE_CPT):
            copy_in(i + 1, 1 - slot, j).start()

    @pl.when(i < nt)
    def _():
        for j in range(MOE_CPT):
            copy_in(i, slot, j).wait()

        @pl.when(i >= 2)
        def _():
            for j in range(MOE_CPT):
                copy_out(i - 2, slot, j).wait()

        @pl.when(jnp.logical_or(i == 0, te_ref[i] != te_ref[jnp.maximum(i - 1, 0)]))
        def _():
            wgb[...] = wg_ref[0].astype(BF16)
            wub[...] = wu_ref[0].astype(BF16)
            wdb[...] = wd_ref[0].astype(BF16)

        x = xbuf[slot]
        a = jnp.dot(x, wgb[...], preferred_element_type=F32)
        u = jnp.dot(x, wub[...], preferred_element_type=F32)
        ybuf[slot] = jnp.dot((_silu(a) * u).astype(BF16), wdb[...], preferred_element_type=F32).astype(BF16)
        for j in range(MOE_CPT):
            copy_out(i, slot, j).start()

        @pl.when(i == nt - 1)
        def _():
            @pl.when(i >= 1)
            def _():
                for j in range(MOE_CPT):
                    copy_out(i - 1, 1 - slot, j).wait()
            for j in range(MOE_CPT):
                copy_out(i, slot, j).wait()


def _ffn(x_rows, w_eg, w_eu, w_ed, te, tbl_in, tbl_out, ntiles):
    nblk, rb, d = x_rows.shape
    ne, _, de = w_eg.shape
    xc = x_rows.reshape(nblk * MOE_CPB, MOE_CH, d)
    max_tiles = te.shape[0]
    y = pl.pallas_call(
        _ffn_kernel,
        grid_spec=pltpu.PrefetchScalarGridSpec(
            num_scalar_prefetch=4,
            grid=(max_tiles,),
            in_specs=[pl.BlockSpec(memory_space=pl.ANY),
                      pl.BlockSpec((1, d, de), lambda i, te, *_: (te[i], 0, 0)),
                      pl.BlockSpec((1, d, de), lambda i, te, *_: (te[i], 0, 0)),
                      pl.BlockSpec((1, de, d), lambda i, te, *_: (te[i], 0, 0))],
            out_specs=pl.BlockSpec(memory_space=pl.ANY),
            scratch_shapes=[pltpu.VMEM((2, MOE_FT, d), BF16), pltpu.VMEM((2, MOE_FT, d), BF16),
                            pltpu.VMEM((d, de), BF16), pltpu.VMEM((d, de), BF16), pltpu.VMEM((de, d), BF16),
                            pltpu.SemaphoreType.DMA((2,)), pltpu.SemaphoreType.DMA((2,))]),
        out_shape=jax.ShapeDtypeStruct(xc.shape, BF16),
        input_output_aliases={4: 0},
        compiler_params=_params("arbitrary"),
        name="moe_ffn",
    )(te, tbl_in, tbl_out, ntiles, xc, w_eg, w_eu, w_ed)
    return y.reshape(nblk, rb, d)


def _combine_kernel(meta_ref, y_ref, rank_ref, comb_ref, h_ref, x1_ref, mod_ref,
                    sg_ref, su_ref, sd_ref, o_ref):
    rhs = _route_rhs(rank_ref, comb_ref, weighted=True)
    h = h_ref[0]
    a = jnp.dot(h, sg_ref[...], preferred_element_type=F32)
    u = jnp.dot(h, su_ref[...], preferred_element_type=F32)
    acc = _mm(_silu(a) * u, sd_ref[...])
    for r0 in range(0, MOE_RB, MOE_CT):
        n = min(MOE_CT, MOE_RB - r0)
        p = _onehot_rows(meta_ref, rhs, r0, n, weighted=True)
        acc = acc + lax.dot_general(p, y_ref[0, r0:r0 + n, :], (((0,), (0,)), ((), ())),
                                    preferred_element_type=F32)
    o_ref[0] = x1_ref[0] + mod_ref[0, 5:6, :] * acc


def _combine(y_rows, rank_t, comb_t, h2, x1, mod3, sg, su, sd, meta):
    bsz, s, d = x1.shape
    nsb = s // MOE_TB
    tok = lambda w: pl.BlockSpec((1, MOE_TB, w), lambda b, i: (b, i, 0))
    ex = pl.BlockSpec((1, N_EXPERTS, MOE_TB), lambda b, i: (b, 0, i))
    full = lambda a: pl.BlockSpec(a.shape, lambda b, i: (0,) * a.ndim)
    return pl.pallas_call(
        _combine_kernel,
        grid=(bsz, nsb),
        in_specs=[pl.BlockSpec((1, 8, LANES), lambda b, i: (b * nsb + i, 0, 0)),
                  pl.BlockSpec((1, MOE_RB, d), lambda b, i: (b * nsb + i, 0, 0)),
                  ex, ex, tok(d), tok(d),
                  pl.BlockSpec((1, 6, d), lambda b, i: (b, 0, 0)),
                  full(sg), full(su), full(sd)],
        out_specs=tok(d),
        out_shape=jax.ShapeDtypeStruct((bsz, s, d), F32),
        compiler_params=_params("parallel", "parallel"),
        name="moe_combine",
    )(meta, y_rows, rank_t, comb_t, h2, x1, mod3, sg, su, sd)


def _moe(h2, comb_t, rank_t, cnt, x1, mod3, w_eg, w_eu, w_ed, sg, su, sd):
    meta, te, tbl_in, tbl_out, ntiles = _moe_plan(cnt)
    x_rows = _dispatch(h2, rank_t, comb_t, meta)
    y_rows = _ffn(x_rows, w_eg, w_eu, w_ed, te, tbl_in, tbl_out, ntiles)
    return _combine(y_rows, rank_t, comb_t, h2, x1, mod3, sg, su, sd, meta)


def _layer(x, cmod, g_mix, w_in, conv_w, a_log, dt_bias, norm_g, qn_g, kn_g, w_up_gdn, w_up_dil, w_out,
           g_ffn, w_router, router_bias, w_eg, w_eu, w_ed, w_sg, w_su, w_sd):
    bsz, s, d = x.shape
    mod3 = cmod.reshape(bsz, 6, d)
    o_ba = 4 * GDN_W
    o_dq = o_ba + 2 * GDN_HEADS
    o_ga = o_dq + 3 * DIL_W
    w_in16 = w_in.astype(BF16)
    w_main = jnp.concatenate([w_in16[:, :o_ba], w_in16[:, o_ga:], w_in16[:, o_dq:o_ga]], axis=1)
    pad_hi = LANES - 2 * GDN_HEADS
    w_ba = jnp.pad(w_in16[:, o_ba:o_dq], ((0, 0), (0, pad_hi)))
    alog_vec = jnp.pad(a_log, (GDN_HEADS, pad_hi)).reshape(1, LANES)
    dtb_vec = jnp.pad(dt_bias, (GDN_HEADS, pad_hi)).reshape(1, LANES)
    proj, gates = _inproj(x, mod3, g_mix, w_main, w_ba, alog_vec, dtb_vec)

    conv_w3 = conv_w.reshape(GDN_CONV, 3, GDN_W).transpose(1, 0, 2)
    y_a = _gdn(proj, gates, conv_w3, norm_g)
    y_b = _dil(proj, qn_g, kn_g)

    wr_t = w_router.T
    wr_hi = wr_t.astype(BF16)
    w_router_split = jnp.concatenate([wr_hi, (wr_t - wr_hi.astype(F32)).astype(BF16)], axis=0)
    x1, h2, comb_t, rank_t, cnt = _merge(x, mod3, y_a, y_b, proj, w_up_gdn.astype(BF16),
                                         w_up_dil.astype(BF16), w_out.astype(BF16), g_ffn.reshape(1, d),
                                         w_router_split, router_bias.reshape(N_EXPERTS, 1))
    cnt = cnt[..., 0].astype(jnp.int32).reshape(-1, N_EXPERTS)
    return _moe(h2, comb_t, rank_t, cnt, x1, mod3, w_eg, w_eu, w_ed,
                w_sg.astype(BF16), w_su.astype(BF16), w_sd.astype(BF16))


def kernel(x, c, w_ada, b_ada, g_mix, w_in, gdn_conv_w, gdn_a_log, gdn_dt_bias, gdn_norm_g, dil_q_norm_g, dil_k_norm_g, w_up_gdn, w_up_dil, w_out, g_ffn, w_router, router_bias, w_exp_gate, w_exp_up, w_exp_down, w_sh_gate, w_sh_up, w_sh_down):
    for l in range(w_ada.shape[0]):
        cmod = _ada(c, w_ada[l], b_ada[l])
        x = _layer(x, cmod, g_mix[l], w_in[l], gdn_conv_w[l], gdn_a_log[l], gdn_dt_bias[l], gdn_norm_g[l],
                   dil_q_norm_g[l], dil_k_norm_g[l], w_up_gdn[l], w_up_dil[l], w_out[l], g_ffn[l],
                   w_router[l], router_bias[l], w_exp_gate[l], w_exp_up[l], w_exp_down[l],
                   w_sh_gate[l], w_sh_up[l], w_sh_down[l])
    return x
```

```python
import functools

import jax
import jax.numpy as jnp
from jax import lax
from jax.experimental import pallas as pl
from jax.experimental.pallas import tpu as pltpu

F32 = jnp.float32
BF16 = jnp.bfloat16

D_MODEL = 1024
GDN_HEADS = 8
GDN_DK = 128
GDN_CONV = 4
DIL_PATTERN = ((128, 1), (512, 4), (2048, 16))
DIL_HEADS_PER_GROUP = 4
DIL_HD = 128
DIL_BLOCK = 128
DIL_BATCH = 8
N_EXPERTS = 64
TOP_K = 8
N_GROUPS = 8
TOPK_GROUPS = 4
D_EXPERT = 256
ROUTE_SCALE = 2.5
EPS = 1e-6

GDN_W = GDN_HEADS * GDN_DK
DIL_HEADS = len(DIL_PATTERN) * DIL_HEADS_PER_GROUP
DIL_W = DIL_HEADS * DIL_HD
DIL_OUT_W = DIL_HEADS_PER_GROUP * DIL_HD
COL_GQ, COL_GK, COL_GV, COL_GZ = 0, GDN_W, 2 * GDN_W, 3 * GDN_W
COL_GA = 4 * GDN_W
COL_GB = COL_GA + D_MODEL
COL_DQ = COL_GB + D_MODEL
COL_DK = COL_DQ + DIL_W
COL_DV = COL_DK + DIL_W
PROJ_W = COL_DV + DIL_W
LANES = 128

VMEM_LIMIT = 56 * 1024 * 1024


def _params(*sem):
    return pltpu.CompilerParams(dimension_semantics=sem, vmem_limit_bytes=VMEM_LIMIT)


def _sigmoid(x):
    return 1.0 / (1.0 + jnp.exp(-x))


def _silu(x):
    return x * _sigmoid(x)


def _softplus(x):
    return jnp.maximum(x, 0.0) + jnp.log(1.0 + jnp.exp(-jnp.abs(x)))


def _mm(a, b):
    return jnp.dot(a.astype(BF16), b.astype(BF16), preferred_element_type=F32)


def _mm_nt(a, b):
    return lax.dot_general(a.astype(BF16), b.astype(BF16), (((1,), (1,)), ((), ())),
                           preferred_element_type=F32)


def _mm_tn(a, b):
    return lax.dot_general(a.astype(BF16), b.astype(BF16), (((0,), (0,)), ((), ())),
                           preferred_element_type=F32)


def _mm_f32(a, b):
    return jnp.dot(a, b, preferred_element_type=F32, precision=lax.Precision.HIGHEST)


def _ada_kernel(c_ref, w_ref, b_ref, o_ref):
    cs = _silu(c_ref[...])
    o_ref[...] = _mm_f32(cs, w_ref[...]) + b_ref[...]


def _ada(c, w_ada, b_ada):
    bsz, d = c.shape
    n = w_ada.shape[1]
    tn = 1536
    return pl.pallas_call(
        _ada_kernel,
        grid=(n // tn,),
        in_specs=[pl.BlockSpec((bsz, d), lambda j: (0, 0)),
                  pl.BlockSpec((d, tn), lambda j: (0, j)),
                  pl.BlockSpec((1, tn), lambda j: (0, j))],
        out_specs=pl.BlockSpec((bsz, tn), lambda j: (0, j)),
        out_shape=jax.ShapeDtypeStruct((bsz, n), F32),
        compiler_params=_params("parallel"),
        name="ada",
    )(c, w_ada, b_ada.reshape(1, n))


def _inproj_kernel(x_ref, mod_ref, g_ref, w_ref, wba_ref, alog_ref, dtb_ref,
                   proj_ref, gates_ref, h_scr):
    j = pl.program_id(2)

    @pl.when(j == 0)
    def _():
        x = x_ref[0]
        sh1 = mod_ref[0, 0:1, :]
        sc1 = mod_ref[0, 1:2, :]
        ms = jnp.mean(x * x, axis=-1, keepdims=True)
        h = x * lax.rsqrt(ms + EPS) * g_ref[...] * (1.0 + sc1) + sh1
        hb = h.astype(BF16)
        h_scr[...] = hb
        ba = jnp.dot(hb, wba_ref[...], preferred_element_type=F32)
        lane = lax.broadcasted_iota(jnp.int32, ba.shape, 1)
        beta = _sigmoid(ba)
        g = -jnp.exp(alog_ref[...]) * _softplus(ba + dtb_ref[...])
        gates_ref[0] = jnp.where(lane < GDN_HEADS, beta, g)

    proj_ref[0] = jnp.dot(h_scr[...], w_ref[...], preferred_element_type=F32).astype(BF16)


def _inproj(x, mod3, g_mix, w_main, w_ba, alog_vec, dtb_vec):
    bsz, s, d = x.shape
    tm, tn = 1024, 2688
    return pl.pallas_call(
        _inproj_kernel,
        grid=(bsz, s // tm, PROJ_W // tn),
        in_specs=[pl.BlockSpec((1, tm, d), lambda b, i, j: (b, i, 0)),
                  pl.BlockSpec((1, 6, d), lambda b, i, j: (b, 0, 0)),
                  pl.BlockSpec((1, d), lambda b, i, j: (0, 0)),
                  pl.BlockSpec((d, tn), lambda b, i, j: (0, j)),
                  pl.BlockSpec((d, LANES), lambda b, i, j: (0, 0)),
                  pl.BlockSpec((1, LANES), lambda b, i, j: (0, 0)),
                  pl.BlockSpec((1, LANES), lambda b, i, j: (0, 0))],
        out_specs=[pl.BlockSpec((1, tm, tn), lambda b, i, j: (b, i, j)),
                   pl.BlockSpec((1, tm, LANES), lambda b, i, j: (b, i, 0))],
        out_shape=[jax.ShapeDtypeStruct((bsz, s, PROJ_W), BF16),
                   jax.ShapeDtypeStruct((bsz, s, LANES), F32)],
        scratch_shapes=[pltpu.VMEM((tm, d), BF16)],
        compiler_params=_params("parallel", "parallel", "arbitrary"),
        name="inproj",
    )(x, mod3, g_mix.reshape(1, d), w_main, w_ba, alog_vec, dtb_vec)


GDN_C = 128
GDN_HG = 8
GDN_TAIL = 8


def _cumsum_rows(g):
    row = lax.broadcasted_iota(jnp.int32, g.shape, 0)
    sft = 1
    while sft < g.shape[0]:
        g = g + jnp.where(row >= sft, pltpu.roll(g, sft, 0), 0.0)
        sft *= 2
    return g


def _unit_lower_inverses(mats):
    n = mats[0].shape[0]
    row = lax.broadcasted_iota(jnp.int32, (n, n), 0)
    col = lax.broadcasted_iota(jnp.int32, (n, n), 1)
    eye = (row == col).astype(F32)
    ts = None
    b = 1
    while b < n:
        off = ((row // (2 * b)) == (col // (2 * b))) & ((row // b) % 2 == 1) & ((col // b) % 2 == 0)
        if b == 1:
            ts = [eye - jnp.where(off, a, 0.0) for a in mats]
        else:
            tb = [t.astype(BF16) for t in ts]
            inner = [jnp.dot(jnp.where(off, a, 0.0).astype(BF16), t, preferred_element_type=F32)
                     for a, t in zip(mats, tb)]
            ts = [t - jnp.dot(t16, i.astype(BF16), preferred_element_type=F32)
                  for t, t16, i in zip(ts, tb, inner)]
        b *= 2
    return ts


def _gdn_kernel(q_ref, k_ref, v_ref, z_ref, gates_ref, cw_ref, ng_ref, o_ref,
                qprev, kprev, vprev, state):
    sblk = pl.program_id(1)
    c = GDN_C
    nh = GDN_HEADS
    w = nh * GDN_DK

    @pl.when(sblk == 0)
    def _():
        state[...] = jnp.zeros_like(state)
        for buf in (qprev, kprev, vprev):
            buf[...] = jnp.zeros_like(buf)

    taps = GDN_CONV - 1
    si = lax.broadcasted_iota(jnp.int32, (taps * c, c), 0)
    sm = lax.broadcasted_iota(jnp.int32, (taps * c, c), 1)
    shift_mat = jnp.where(sm == si % c - (taps - si // c), 1.0, 0.0).astype(BF16)
    trow = lax.broadcasted_iota(jnp.int32, (GDN_TAIL, w), 0)
    conv = []
    for idx, (ref, prev) in enumerate(((q_ref, qprev), (k_ref, kprev), (v_ref, vprev))):
        cur = ref[0]
        cur32 = cur.astype(F32)
        delayed = jnp.dot(shift_mat, cur, preferred_element_type=F32)
        acc = cur32 * cw_ref[idx, taps:taps + 1, :]
        head = jnp.zeros((GDN_TAIL, w), F32)
        tail = prev[...]
        for j in range(taps):
            acc = acc + delayed[j * c:(j + 1) * c, :] * cw_ref[idx, j:j + 1, :]
            head = head + jnp.where(trow < taps - j, pltpu.roll(tail, taps - j, 0), 0.0) * cw_ref[idx, j:j + 1, :]
        acc = jnp.concatenate([acc[:GDN_TAIL] + head, acc[GDN_TAIL:]], axis=0)
        prev[...] = cur32[c - GDN_TAIL:, :]
        conv.append(_silu(acc))
    qc, kc, vc = conv

    gates = gates_ref[0]
    gcum = _cumsum_rows(gates)
    row = lax.broadcasted_iota(jnp.int32, (c, c), 0)
    col = lax.broadcasted_iota(jnp.int32, (c, c), 1)
    eye = row == col
    incl = col <= row
    strict = col < row

    for hg in range(0, nh, GDN_HG):
        heads = range(hg, hg + GDN_HG)
        hsl = {h: slice(h * GDN_DK, (h + 1) * GDN_DK) for h in heads}
        q = [qc[:, hsl[h]] for h in heads]
        k = [kc[:, hsl[h]] for h in heads]
        v = [vc[:, hsl[h]] for h in heads]
        q = [x * lax.rsqrt(jnp.sum(x * x, axis=-1, keepdims=True) + EPS) * (GDN_DK ** -0.5) for x in q]
        k = [x * lax.rsqrt(jnp.sum(x * x, axis=-1, keepdims=True) + EPS) for x in k]
        beta = [jnp.broadcast_to(gates[:, h:h + 1], (c, GDN_DK)) for h in heads]
        gc = [jnp.broadcast_to(gcum[:, nh + h:nh + h + 1], (c, GDN_DK)) for h in heads]
        gc_row = [jnp.sum(jnp.where(eye, x, 0.0), axis=0, keepdims=True) for x in gc]
        decay = [jnp.exp(jnp.where(incl, x - y, -jnp.inf)) for x, y in zip(gc, gc_row)]
        egc = [jnp.exp(x) for x in gc]
        kb = [x * b for x, b in zip(k, beta)]
        scores = [_mm_nt(jnp.concatenate([a, b], axis=0), x) for a, b, x in zip(kb, q, k)]
        a_mat = [jnp.where(strict, s[:c] * d, 0.0) for s, d in zip(scores, decay)]
        qk = [s[c:] * d for s, d in zip(scores, decay)]
        t_inv = _unit_lower_inverses(a_mat)
        sol = [_mm(t, jnp.concatenate([x * b, y * e], axis=1))
               for t, x, b, y, e in zip(t_inv, v, beta, kb, egc)]

        st = [state[h] for h in heads]
        ws = [_mm(jnp.concatenate([s[:, GDN_DK:], x * e], axis=0), m)
              for s, x, e, m in zip(sol, q, egc, st)]
        v_new = [s[:, :GDN_DK] - x[:c] for s, x in zip(sol, ws)]
        o = [x[c:] + _mm(a, b) for x, a, b in zip(ws, qk, v_new)]
        gc_last = [x[c - 1:c, :] for x in gc]
        k_dec = [x * jnp.exp(l - g) for x, l, g in zip(k, gc_last, gc)]
        for n, h in enumerate(heads):
            state[h] = st[n] * jnp.exp(gc_last[n]) + _mm_tn(k_dec[n], v_new[n])
            y = o[n] * lax.rsqrt(jnp.mean(o[n] * o[n], axis=-1, keepdims=True) + EPS) * ng_ref[...]
            o_ref[0, :, hsl[h]] = (y * _silu(z_ref[0, :, hsl[h]].astype(F32))).astype(o_ref.dtype)


def _gdn(proj, gates, conv_w3, norm_g):
    bsz, s, _ = proj.shape
    w = GDN_W

    def col_spec(col0):
        return pl.BlockSpec((1, GDN_C, w), lambda b, i: (b, i, col0 // w))

    return pl.pallas_call(
        _gdn_kernel,
        grid=(bsz, s // GDN_C),
        in_specs=[col_spec(COL_GQ), col_spec(COL_GK), col_spec(COL_GV), col_spec(COL_GZ),
                  pl.BlockSpec((1, GDN_C, LANES), lambda b, i: (b, i, 0)),
                  pl.BlockSpec((3, GDN_CONV, w), lambda b, i: (0, 0, 0)),
                  pl.BlockSpec((1, GDN_DK), lambda b, i: (0, 0))],
        out_specs=pl.BlockSpec((1, GDN_C, w), lambda b, i: (b, i, 0)),
        out_shape=jax.ShapeDtypeStruct((bsz, s, GDN_W), BF16),
        scratch_shapes=[pltpu.VMEM((GDN_TAIL, w), F32),
                        pltpu.VMEM((GDN_TAIL, w), F32),
                        pltpu.VMEM((GDN_TAIL, w), F32),
                        pltpu.VMEM((GDN_HEADS, GDN_DK, GDN_DK), F32)],
        compiler_params=_params("parallel", "arbitrary"),
        name="gdn",
    )(proj, proj, proj, proj, gates, conv_w3, norm_g.reshape(1, GDN_DK))


def _dil_kernel(q0, q1, q2, k0, k1, k2, v0, v1, v2, qg_ref, kg_ref, o_ref, qs, ks, vs, os_, ls):
    s = qs.shape[1]
    blk = DIL_BLOCK
    ngrp = len(DIL_PATTERN)
    for gi, (q_ref, k_ref, v_ref) in enumerate(((q0, k0, v0), (q1, k1, v1), (q2, k2, v2))):
        q = q_ref[0].astype(F32)
        k = k_ref[0].astype(F32)
        q = q * lax.rsqrt(jnp.mean(q * q, axis=-1, keepdims=True) + EPS) * qg_ref[...]
        k = k * lax.rsqrt(jnp.mean(k * k, axis=-1, keepdims=True) + EPS) * kg_ref[...]
        qs[gi] = q * (DIL_HD ** -0.5)
        ks[gi] = k
        vs[gi] = v_ref[0].astype(F32)

    qi = lax.broadcasted_iota(jnp.int32, (blk, blk), 0)
    kj = lax.broadcasted_iota(jnp.int32, (blk, blk), 1)
    cur_ok = kj <= qi
    prev_ok = kj >= qi

    items = []
    for gi, (win, dil) in enumerate(DIL_PATTERN):
        assert win // dil == blk and (s // dil) % blk == 0
        for r in range(dil):
            for n in range((s // dil) // blk):
                items.append((gi, dil, r, n))

    def rows(dil, r, m):
        if dil == 1:
            return pl.ds(m * blk, blk)
        return pl.ds(m * blk * dil + r, blk, stride=dil)

    for b0 in range(0, len(items), DIL_BATCH):
        batch = items[b0:b0 + DIL_BATCH]
        qb = [qs[gi, rows(dil, r, n), :] for gi, dil, r, n in batch]
        kc = [ks[gi, rows(dil, r, n), :] for gi, dil, r, n in batch]
        kp = [ks[gi, rows(dil, r, n - 1), :] if n > 0 else None for gi, dil, r, n in batch]
        s_cur = [jnp.where(cur_ok, _mm_nt(q, k), -jnp.inf) for q, k in zip(qb, kc)]
        s_prev = [None if k is None else jnp.where(prev_ok, _mm_nt(q, k), -jnp.inf) for q, k in zip(qb, kp)]
        m = [jnp.max(a, axis=-1, keepdims=True) if b is None else
             jnp.maximum(jnp.max(a, axis=-1, keepdims=True), jnp.max(b, axis=-1, keepdims=True))
             for a, b in zip(s_cur, s_prev)]
        p_cur = [jnp.exp(a - mm) for a, mm in zip(s_cur, m)]
        p_prev = [None if b is None else jnp.exp(b - mm) for b, mm in zip(s_prev, m)]
        den = [jnp.sum(a, axis=-1, keepdims=True) if b is None else
               jnp.sum(a, axis=-1, keepdims=True) + jnp.sum(b, axis=-1, keepdims=True)
               for a, b in zip(p_cur, p_prev)]
        o = [_mm(p, vs[gi, rows(dil, r, n), :]) for p, (gi, dil, r, n) in zip(p_cur, batch)]
        o = [a if p is None else a + _mm(p, vs[gi, rows(dil, r, n - 1), :])
             for a, p, (gi, dil, r, n) in zip(o, p_prev, batch)]
        for (gi, dil, r, n), a, d, mm in zip(batch, o, den, m):
            os_[gi, rows(dil, r, n), :] = a / d
            ls[gi, rows(dil, r, n), :] = jnp.broadcast_to(mm + jnp.log(d), (blk, DIL_HD))

    lse = [ls[gi] for gi in range(ngrp)]
    mx = functools.reduce(jnp.maximum, lse)
    ex = [jnp.exp(l - mx) for l in lse]
    tot = functools.reduce(lambda a, b: a + b, ex)
    y = functools.reduce(lambda a, b: a + b, [ex[gi] / tot * os_[gi] for gi in range(ngrp)])
    o_ref[0] = y.astype(o_ref.dtype)


def _dil(proj, q_norm_g, k_norm_g):
    bsz, s, _ = proj.shape
    ngrp = len(DIL_PATTERN)

    def specs(col0):
        blk0 = col0 // DIL_HD
        return [pl.BlockSpec((1, s, DIL_HD), lambda b, h, g=g: (b, 0, blk0 + g * DIL_HEADS_PER_GROUP + h))
                for g in range(ngrp)]

    gain = pl.BlockSpec((1, DIL_HD), lambda b, h: (0, 0))
    return pl.pallas_call(
        _dil_kernel,
        grid=(bsz, DIL_HEADS_PER_GROUP),
        in_specs=specs(COL_DQ) + specs(COL_DK) + specs(COL_DV) + [gain, gain],
        out_specs=pl.BlockSpec((1, s, DIL_HD), lambda b, h: (b, 0, h)),
        out_shape=jax.ShapeDtypeStruct((bsz, s, DIL_OUT_W), BF16),
        scratch_shapes=[pltpu.VMEM((ngrp, s, DIL_HD), F32) for _ in range(5)],
        compiler_params=_params("parallel", "parallel"),
        name="dilattn",
    )(*([proj] * 9), q_norm_g.reshape(1, DIL_HD), k_norm_g.reshape(1, DIL_HD))


def _route(logits, bias):
    e, tm = logits.shape
    per = e // N_GROUPS
    scores = _sigmoid(logits)
    sel = scores + bias
    neg = -jnp.inf
    sub = lax.broadcasted_iota(jnp.int32, (per, tm), 0)
    gs_rows = []
    for g in range(N_GROUPS):
        blk = sel[g * per:(g + 1) * per, :]
        m1 = jnp.max(blk, axis=0, keepdims=True)
        i1 = jnp.min(jnp.where(blk == m1, sub, per), axis=0, keepdims=True)
        m2 = jnp.max(jnp.where(sub == i1, neg, blk), axis=0, keepdims=True)
        gs_rows.append(m1 + m2)
    gs = jnp.concatenate(gs_rows, axis=0)
    gi = lax.broadcasted_iota(jnp.int32, (N_GROUPS, tm), 0)
    gsel = jnp.zeros((N_GROUPS, tm), F32)
    for _ in range(TOPK_GROUPS):
        m = jnp.max(gs, axis=0, keepdims=True)
        idx = jnp.min(jnp.where(gs == m, gi, N_GROUPS), axis=0, keepdims=True)
        hit = gi == idx
        gsel = jnp.where(hit, 1.0, gsel)
        gs = jnp.where(hit, neg, gs)
    cand = jnp.concatenate(
        [jnp.where(gsel[g:g + 1, :] > 0.0, sel[g * per:(g + 1) * per, :], neg) for g in range(N_GROUPS)], axis=0)
    ei = lax.broadcasted_iota(jnp.int32, (e, tm), 0)
    chosen = jnp.zeros((e, tm), F32)
    for _ in range(TOP_K):
        m = jnp.max(cand, axis=0, keepdims=True)
        idx = jnp.min(jnp.where(cand == m, ei, e), axis=0, keepdims=True)
        hit = ei == idx
        chosen = jnp.where(hit, scores, chosen)
        cand = jnp.where(hit, neg, cand)
    return chosen / jnp.sum(chosen, axis=0, keepdims=True) * ROUTE_SCALE


def _merge_kernel(x_ref, mod_ref, ya_ref, yb_ref, ga_ref, gb_ref, wg_ref, wd_ref, wo_ref,
                  gf_ref, wr_ref, rb_ref, x1_ref, h2_ref, comb_ref, rank_ref, cnt_ref):
    gt1 = mod_ref[0, 2:3, :]
    sh2 = mod_ref[0, 3:4, :]
    sc2 = mod_ref[0, 4:5, :]
    ua = jnp.dot(ya_ref[0], wg_ref[...], preferred_element_type=F32)
    ub = jnp.dot(yb_ref[0], wd_ref[...], preferred_element_type=F32)
    merged = _sigmoid(ga_ref[0].astype(F32)) * ua + _sigmoid(gb_ref[0].astype(F32)) * ub
    x1 = x_ref[0] + gt1 * _mm(merged, wo_ref[...])
    x1_ref[0] = x1
    h2 = x1 * lax.rsqrt(jnp.mean(x1 * x1, axis=-1, keepdims=True) + EPS) * gf_ref[...] * (1.0 + sc2) + sh2
    h2_hi = h2.astype(BF16)
    h2_ref[0] = h2_hi
    h2_lo = (h2 - h2_hi.astype(F32)).astype(BF16)
    nt_dims = (((1,), (1,)), ((), ()))
    both = lax.dot_general(wr_ref[...], h2_hi, nt_dims, preferred_element_type=F32)
    logits = (both[:N_EXPERTS] + both[N_EXPERTS:]
              + lax.dot_general(wr_ref[:N_EXPERTS, :], h2_lo, nt_dims, preferred_element_type=F32))
    comb = _route(logits, rb_ref[...])
    comb_ref[0] = comb
    tb = MOE_TB
    before = (lax.broadcasted_iota(jnp.int32, (tb, tb), 0) < lax.broadcasted_iota(jnp.int32, (tb, tb), 1))
    before = before.astype(BF16)
    for j in range(comb.shape[1] // tb):
        picked = (comb[:, j * tb:(j + 1) * tb] > 0.0).astype(F32)
        rank_ref[0, :, j * tb:(j + 1) * tb] = jnp.dot(picked.astype(BF16), before, preferred_element_type=F32)
        cnt_ref[0, j] = jnp.broadcast_to(jnp.sum(picked, axis=1, keepdims=True), (N_EXPERTS, LANES))


def _merge(x, mod3, y_a, y_b, proj, w_up_gdn, w_up_dil, w_out, g_ffn, w_router_t, router_bias):
    bsz, s, d = x.shape
    tm = 512
    row = lambda w: pl.BlockSpec((1, tm, w), lambda b, i: (b, i, 0))
    full = lambda a: pl.BlockSpec(a.shape, lambda b, i: (0,) * a.ndim)
    ex = pl.BlockSpec((1, N_EXPERTS, tm), lambda b, i: (b, 0, i))
    return pl.pallas_call(
        _merge_kernel,
        grid=(bsz, s // tm),
        in_specs=[row(d),
                  pl.BlockSpec((1, 6, d), lambda b, i: (b, 0, 0)),
                  row(GDN_W), row(DIL_OUT_W),
                  pl.BlockSpec((1, tm, d), lambda b, i: (b, i, COL_GA // d)),
                  pl.BlockSpec((1, tm, d), lambda b, i: (b, i, COL_GB // d)),
                  full(w_up_gdn), full(w_up_dil), full(w_out), full(g_ffn), full(w_router_t),
                  full(router_bias)],
        out_specs=[row(d), row(d), ex, ex,
                   pl.BlockSpec((1, tm // MOE_TB, N_EXPERTS, LANES), lambda b, i: (b, i, 0, 0))],
        out_shape=[jax.ShapeDtypeStruct((bsz, s, d), F32),
                   jax.ShapeDtypeStruct((bsz, s, d), BF16),
                   jax.ShapeDtypeStruct((bsz, N_EXPERTS, s), F32),
                   jax.ShapeDtypeStruct((bsz, N_EXPERTS, s), F32),
                   jax.ShapeDtypeStruct((bsz, s // MOE_TB, N_EXPERTS, LANES), F32)],
        compiler_params=_params("parallel", "parallel"),
        name="merge_router",
    )(x, mod3, y_a, y_b, proj, proj, w_up_gdn, w_up_dil, w_out, g_ffn, w_router_t, router_bias)


MOE_TB = 256
MOE_CH = 16
MOE_RB = MOE_TB * TOP_K + N_EXPERTS * MOE_CH
MOE_CPB = MOE_RB // MOE_CH
MOE_TM = 512
MOE_CT = MOE_RB
MOE_FT = 512
MOE_CPT = MOE_FT // MOE_CH


def _moe_plan(cnt):
    nb, ne = cnt.shape
    i32 = jnp.int32
    nch = (cnt + MOE_CH - 1) // MOE_CH
    seg0 = jnp.cumsum(nch, axis=1) - nch
    pad = jnp.zeros((nb, LANES - ne), i32)
    meta = jnp.stack([jnp.concatenate([seg0 * MOE_CH, pad + MOE_RB], axis=1),
                      jnp.concatenate([nch * MOE_CH, pad], axis=1)], axis=1).astype(F32)
    meta = jnp.pad(meta, ((0, 0), (0, 6), (0, 0)))
    nch_e = nch.T
    cum_e = jnp.cumsum(nch_e, axis=1)
    tot_e = cum_e[:, -1]
    tiles_e = (tot_e + MOE_CPT - 1) // MOE_CPT
    tile_end = jnp.cumsum(tiles_e)
    ntiles = tile_end[-1]
    max_tiles = (nb * MOE_CPB) // MOE_CPT + ne
    i = jnp.arange(max_tiles, dtype=i32)
    te = jnp.sum((tile_end[None, :] <= i[:, None]).astype(i32), axis=1)
    te = jnp.minimum(te, jnp.sum((tile_end <= ntiles - 1).astype(i32)))
    te = jnp.minimum(te, ne - 1)
    oh_te = (te[:, None] == jnp.arange(ne, dtype=i32)[None, :]).astype(i32)
    pick = lambda tab: jnp.sum(oh_te[:, :, None] * tab[None, :, :], axis=1)
    tile_off_t = jnp.sum(oh_te * (tile_end - tiles_e)[None, :], axis=1)
    tot_t = jnp.sum(oh_te * tot_e[None, :], axis=1)
    cum_t, nch_t, seg0_t = pick(cum_e), pick(nch_e), pick(seg0.T)
    q = (i - tile_off_t)[:, None] * MOE_CPT + jnp.arange(MOE_CPT, dtype=i32)[None, :]
    valid = (q < tot_t[:, None]) & (i < ntiles)[:, None]
    blk = jnp.minimum(jnp.sum((cum_t[:, None, :] <= q[:, :, None]).astype(i32), axis=-1), nb - 1)
    oh_b = (blk[:, :, None] == jnp.arange(nb, dtype=i32)[None, None, :]).astype(i32)
    before = jnp.sum(oh_b * (cum_t - nch_t)[:, None, :], axis=-1)
    src = blk * MOE_CPB + jnp.sum(oh_b * seg0_t[:, None, :], axis=-1) + (q - before)
    spare = nb * MOE_CPB + (i % 2)[:, None] * MOE_CPT + jnp.arange(MOE_CPT, dtype=i32)[None, :]
    tbl_in = jnp.where(valid, src, nb * MOE_CPB + 2 * MOE_CPT).astype(i32).reshape(-1)
    tbl_out = jnp.where(valid, src, spare).astype(i32).reshape(-1)
    return meta, te.astype(i32), tbl_in, tbl_out, ntiles.astype(i32).reshape(1)


def _onehot_rows(meta_ref, rhs, r0, nrows, weighted):
    start = meta_ref[0, 0:1, :]
    plen = meta_ref[0, 1:2, :]
    r = (lax.broadcasted_iota(jnp.int32, (nrows, LANES), 0) + r0).astype(F32)
    owner = jnp.where((r >= start) & (r < start + plen), 1.0, 0.0)
    pos = r[:, :1] - jnp.sum(owner * start, axis=1, keepdims=True)
    got = jnp.dot(owner[:, :N_EXPERTS].astype(BF16), rhs, preferred_element_type=F32)
    tb = MOE_TB
    hit = (got[:, :tb] == pos) & (got[:, tb:2 * tb] > 0.5)
    return jnp.where(hit, got[:, 2 * tb:] if weighted else 1.0, 0.0).astype(BF16)


def _route_rhs(rank_ref, comb_ref, weighted):
    comb = comb_ref[0]
    picked = jnp.where(comb > 0.0, 1.0, 0.0)
    parts = [rank_ref[0], picked] + ([comb] if weighted else [])
    return jnp.concatenate(parts, axis=1).astype(BF16)


def _dispatch_kernel(meta_ref, h_ref, rank_ref, comb_ref, x_ref):
    @pl.when(pl.program_id(0) < pl.num_programs(0) - 1)
    def _():
        rhs = _route_rhs(rank_ref, comb_ref, weighted=False)
        h = h_ref[0]
        for r0 in range(0, MOE_RB, MOE_TM):
            n = min(MOE_TM, MOE_RB - r0)
            p = _onehot_rows(meta_ref, rhs, r0, n, weighted=False)
            x_ref[0, r0:r0 + n, :] = jnp.dot(p, h, preferred_element_type=F32).astype(BF16)

    @pl.when(pl.program_id(0) == pl.num_programs(0) - 1)
    def _():
        x_ref[...] = jnp.zeros_like(x_ref)


def _dispatch(h2, rank_t, comb_t, meta):
    bsz, s, d = h2.shape
    nsb = s // MOE_TB
    nblk = bsz * nsb
    tok = lambda i: (jnp.minimum(i, nblk - 1) // nsb, jnp.minimum(i, nblk - 1) % nsb)
    ex = pl.BlockSpec((1, N_EXPERTS, MOE_TB), lambda i: (tok(i)[0], 0, tok(i)[1]))
    return pl.pallas_call(
        _dispatch_kernel,
        grid=(nblk + 1,),
        in_specs=[pl.BlockSpec((1, 8, LANES), lambda i: (jnp.minimum(i, nblk - 1), 0, 0)),
                  pl.BlockSpec((1, MOE_TB, d), lambda i: (tok(i)[0], tok(i)[1], 0)),
                  ex, ex],
        out_specs=pl.BlockSpec((1, MOE_RB, d), lambda i: (i, 0, 0)),
        out_shape=jax.ShapeDtypeStruct((nblk + 1, MOE_RB, d), BF16),
        compiler_params=_params("parallel"),
        name="moe_dispatch",
    )(meta, h2, rank_t, comb_t)


def _ffn_kernel(te_ref, tin_ref, tout_ref, nt_ref, x_hbm, wg_ref, wu_ref, wd_ref, y_hbm,
                xbuf, ybuf, wgb, wub, wdb, sem_in, sem_out):
    i = pl.program_id(0)
    nt = nt_ref[0]
    slot = i % 2

    def copy_in(t, s, j):
        return pltpu.make_async_copy(x_hbm.at[tin_ref[t * MOE_CPT + j]],
                                     xbuf.at[s, pl.ds(j * MOE_CH, MOE_CH)], sem_in.at[s])

    def copy_out(t, s, j):
        return pltpu.make_async_copy(ybuf.at[s, pl.ds(j * MOE_CH, MOE_CH)],
                                     y_hbm.at[tout_ref[t * MOE_CPT + j]], sem_out.at[s])

    @pl.when(i == 0)
    def _():
        for j in range(MOE_CPT):
            copy_in(0, 0, j).start()

    @pl.when(i + 1 < nt)
    def _():
        for j in range(MOE_CPT):
            copy_in(i + 1, 1 - slot, j).start()

    @pl.when(i < nt)
    def _():
        for j in range(MOE_CPT):
            copy_in(i, slot, j).wait()

        @pl.when(i >= 2)
        def _():
            for j in range(MOE_CPT):
                copy_out(i - 2, slot, j).wait()

        @pl.when(jnp.logical_or(i == 0, te_ref[i] != te_ref[jnp.maximum(i - 1, 0)]))
        def _():
            wgb[...] = wg_ref[0].astype(BF16)
            wub[...] = wu_ref[0].astype(BF16)
            wdb[...] = wd_ref[0].astype(BF16)

        x = xbuf[slot]
        a = jnp.dot(x, wgb[...], preferred_element_type=F32)
        u = jnp.dot(x, wub[...], preferred_element_type=F32)
        ybuf[slot] = jnp.dot((_silu(a) * u).astype(BF16), wdb[...], preferred_element_type=F32).astype(BF16)
        for j in range(MOE_CPT):
            copy_out(i, slot, j).start()

        @pl.when(i == nt - 1)
        def _():
            @pl.when(i >= 1)
            def _():
                for j in range(MOE_CPT):
                    copy_out(i - 1, 1 - slot, j).wait()
            for j in range(MOE_CPT):
                copy_out(i, slot, j).wait()


def _ffn(x_rows, w_eg, w_eu, w_ed, te, tbl_in, tbl_out, ntiles):
    nblk, rb, d = x_rows.shape
    ne, _, de = w_eg.shape
    xc = x_rows.reshape(nblk * MOE_CPB, MOE_CH, d)
    max_tiles = te.shape[0]
    y = pl.pallas_call(
        _ffn_kernel,
        grid_spec=pltpu.PrefetchScalarGridSpec(
            num_scalar_prefetch=4,
            grid=(max_tiles,),
            in_specs=[pl.BlockSpec(memory_space=pl.ANY),
                      pl.BlockSpec((1, d, de), lambda i, te, *_: (te[i], 0, 0)),
                      pl.BlockSpec((1, d, de), lambda i, te, *_: (te[i], 0, 0)),
                      pl.BlockSpec((1, de, d), lambda i, te, *_: (te[i], 0, 0))],
            out_specs=pl.BlockSpec(memory_space=pl.ANY),
            scratch_shapes=[pltpu.VMEM((2, MOE_FT, d), BF16), pltpu.VMEM((2, MOE_FT, d), BF16),
                            pltpu.VMEM((d, de), BF16), pltpu.VMEM((d, de), BF16), pltpu.VMEM((de, d), BF16),
                            pltpu.SemaphoreType.DMA((2,)), pltpu.SemaphoreType.DMA((2,))]),
        out_shape=jax.ShapeDtypeStruct(xc.shape, BF16),
        input_output_aliases={4: 0},
        compiler_params=_params("arbitrary"),
        name="moe_ffn",
    )(te, tbl_in, tbl_out, ntiles, xc, w_eg, w_eu, w_ed)
    return y.reshape(nblk, rb, d)


def _combine_kernel(meta_ref, y_ref, rank_ref, comb_ref, h_ref, x1_ref, mod_ref,
                    sg_ref, su_ref, sd_ref, o_ref):
    rhs = _route_rhs(rank_ref, comb_ref, weighted=True)
    h = h_ref[0]
    a = jnp.dot(h, sg_ref[...], preferred_element_type=F32)
    u = jnp.dot(h, su_ref[...], preferred_element_type=F32)
    acc = _mm(_silu(a) * u, sd_ref[...])
    for r0 in range(0, MOE_RB, MOE_CT):
        n = min(MOE_CT, MOE_RB - r0)
        p = _onehot_rows(meta_ref, rhs, r0, n, weighted=True)
        acc = acc + lax.dot_general(p, y_ref[0, r0:r0 + n, :], (((0,), (0,)), ((), ())),
                                    preferred_element_type=F32)
    o_ref[0] = x1_ref[0] + mod_ref[0, 5:6, :] * acc


def _combine(y_rows, rank_t, comb_t, h2, x1, mod3, sg, su, sd, meta):
    bsz, s, d = x1.shape
    nsb = s // MOE_TB
    tok = lambda w: pl.BlockSpec((1, MOE_TB, w), lambda b, i: (b, i, 0))
    ex = pl.BlockSpec((1, N_EXPERTS, MOE_TB), lambda b, i: (b, 0, i))
    full = lambda a: pl.BlockSpec(a.shape, lambda b, i: (0,) * a.ndim)
    return pl.pallas_call(
        _combine_kernel,
        grid=(bsz, nsb),
        in_specs=[pl.BlockSpec((1, 8, LANES), lambda b, i: (b * nsb + i, 0, 0)),
                  pl.BlockSpec((1, MOE_RB, d), lambda b, i: (b * nsb + i, 0, 0)),
                  ex, ex, tok(d), tok(d),
                  pl.BlockSpec((1, 6, d), lambda b, i: (b, 0, 0)),
                  full(sg), full(su), full(sd)],
        out_specs=tok(d),
        out_shape=jax.ShapeDtypeStruct((bsz, s, d), F32),
        compiler_params=_params("parallel", "parallel"),
        name="moe_combine",
    )(meta, y_rows, rank_t, comb_t, h2, x1, mod3, sg, su, sd)


def _moe(h2, comb_t, rank_t, cnt, x1, mod3, w_eg, w_eu, w_ed, sg, su, sd):
    meta, te, tbl_in, tbl_out, ntiles = _moe_plan(cnt)
    x_rows = _dispatch(h2, rank_t, comb_t, meta)
    y_rows = _ffn(x_rows, w_eg, w_eu, w_ed, te, tbl_in, tbl_out, ntiles)
    return _combine(y_rows, rank_t, comb_t, h2, x1, mod3, sg, su, sd, meta)


def _layer(x, cmod, g_mix, w_in, conv_w, a_log, dt_bias, norm_g, qn_g, kn_g, w_up_gdn, w_up_dil, w_out,
           g_ffn, w_router, router_bias, w_eg, w_eu, w_ed, w_sg, w_su, w_sd):
    bsz, s, d = x.shape
    mod3 = cmod.reshape(bsz, 6, d)
    o_ba = 4 * GDN_W
    o_dq = o_ba + 2 * GDN_HEADS
    o_ga = o_dq + 3 * DIL_W
    w_in16 = w_in.astype(BF16)
    w_main = jnp.concatenate([w_in16[:, :o_ba], w_in16[:, o_ga:], w_in16[:, o_dq:o_ga]], axis=1)
    pad_hi = LANES - 2 * GDN_HEADS
    w_ba = jnp.pad(w_in16[:, o_ba:o_dq], ((0, 0), (0, pad_hi)))
    alog_vec = jnp.pad(a_log, (GDN_HEADS, pad_hi)).reshape(1, LANES)
    dtb_vec = jnp.pad(dt_bias, (GDN_HEADS, pad_hi)).reshape(1, LANES)
    proj, gates = _inproj(x, mod3, g_mix, w_main, w_ba, alog_vec, dtb_vec)

    conv_w3 = conv_w.reshape(GDN_CONV, 3, GDN_W).transpose(1, 0, 2)
    y_a = _gdn(proj, gates, conv_w3, norm_g)
    y_b = _dil(proj, qn_g, kn_g)

    wr_t = w_router.T
    wr_hi = wr_t.astype(BF16)
    w_router_split = jnp.concatenate([wr_hi, (wr_t - wr_hi.astype(F32)).astype(BF16)], axis=0)
    x1, h2, comb_t, rank_t, cnt = _merge(x, mod3, y_a, y_b, proj, w_up_gdn.astype(BF16),
                                         w_up_dil.astype(BF16), w_out.astype(BF16), g_ffn.reshape(1, d),
                                         w_router_split, router_bias.reshape(N_EXPERTS, 1))
    cnt = cnt[..., 0].astype(jnp.int32).reshape(-1, N_EXPERTS)
    return _moe(h2, comb_t, rank_t, cnt, x1, mod3, w_eg, w_eu, w_ed,
                w_sg.astype(BF16), w_su.astype(BF16), w_sd.astype(BF16))


def kernel(x, c, w_ada, b_ada, g_mix, w_in, gdn_conv_w, gdn_a_log, gdn_dt_bias, gdn_norm_g, dil_q_norm_g, dil_k_norm_g, w_up_gdn, w_up_dil, w_out, g_ffn, w_router, router_bias, w_exp_gate, w_exp_up, w_exp_down, w_sh_gate, w_sh_up, w_sh_down):
    for l in range(w_ada.shape[0]):
        cmod = _ada(c, w_ada[l], b_ada[l])
        x = _layer(x, cmod, g_mix[l], w_in[l], gdn_conv_w[l], gdn_a_log[l], gdn_dt_bias[l], gdn_norm_g[l],
                   dil_q_norm_g[l], dil_k_norm_g[l], w_up_gdn[l], w_up_dil[l], w_out[l], g_ffn[l],
                   w_router[l], router_bias[l], w_exp_gate[l], w_exp_up[l], w_exp_down[l],
                   w_sh_gate[l], w_sh_up[l], w_sh_down[l])
    return x
```

```python
import functools

import jax
import jax.numpy as jnp
from jax import lax
from jax.experimental import pallas as pl
from jax.experimental.pallas import tpu as pltpu

F32 = jnp.float32
BF16 = jnp.bfloat16

D_MODEL = 1024
GDN_HEADS = 8
GDN_DK = 128
GDN_CONV = 4
DIL_PATTERN = ((128, 1), (512, 4), (2048, 16))
DIL_HEADS_PER_GROUP = 4
DIL_HD = 128
DIL_BLOCK = 128
DIL_BATCH = 8
N_EXPERTS = 64
TOP_K = 8
N_GROUPS = 8
TOPK_GROUPS = 4
D_EXPERT = 256
ROUTE_SCALE = 2.5
EPS = 1e-6

GDN_W = GDN_HEADS * GDN_DK
DIL_HEADS = len(DIL_PATTERN) * DIL_HEADS_PER_GROUP
DIL_W = DIL_HEADS * DIL_HD
DIL_OUT_W = DIL_HEADS_PER_GROUP * DIL_HD
COL_GQ, COL_GK, COL_GV, COL_GZ = 0, GDN_W, 2 * GDN_W, 3 * GDN_W
COL_GA = 4 * GDN_W
COL_GB = COL_GA + D_MODEL
COL_DQ = COL_GB + D_MODEL
COL_DK = COL_DQ + DIL_W
COL_DV = COL_DK + DIL_W
PROJ_W = COL_DV + DIL_W
LANES = 128

VMEM_LIMIT = 56 * 1024 * 1024


def _params(*sem):
    return pltpu.CompilerParams(dimension_semantics=sem, vmem_limit_bytes=VMEM_LIMIT)


def _sigmoid(x):
    return 1.0 / (1.0 + jnp.exp(-x))


def _silu(x):
    return x * _sigmoid(x)


def _softplus(x):
    return jnp.maximum(x, 0.0) + jnp.log(1.0 + jnp.exp(-jnp.abs(x)))


def _mm(a, b):
    return jnp.dot(a.astype(BF16), b.astype(BF16), preferred_element_type=F32)


def _mm_nt(a, b):
    return lax.dot_general(a.astype(BF16), b.astype(BF16), (((1,), (1,)), ((), ())),
                           preferred_element_type=F32)


def _mm_tn(a, b):
    return lax.dot_general(a.astype(BF16), b.astype(BF16), (((0,), (0,)), ((), ())),
                           preferred_element_type=F32)


def _ada_kernel(c_ref, w_ref, b_ref, o_ref):
    cs = _silu(c_ref[...])
    w = w_ref[...]
    cs_hi = cs.astype(BF16)
    w_hi = w.astype(BF16)
    cs_lo = (cs - cs_hi.astype(F32)).astype(BF16)
    w_lo = (w - w_hi.astype(F32)).astype(BF16)
    dot = lambda a, b: jnp.dot(a, b, preferred_element_type=F32)
    o_ref[...] = dot(cs_hi, w_hi) + dot(cs_lo, w_hi) + dot(cs_hi, w_lo) + b_ref[...]


def _ada(c, w_ada, b_ada):
    bsz, d = c.shape
    n = w_ada.shape[1]
    tn = 1536
    return pl.pallas_call(
        _ada_kernel,
        grid=(n // tn,),
        in_specs=[pl.BlockSpec((bsz, d), lambda j: (0, 0)),
                  pl.BlockSpec((d, tn), lambda j: (0, j)),
                  pl.BlockSpec((1, tn), lambda j: (0, j))],
        out_specs=pl.BlockSpec((bsz, tn), lambda j: (0, j)),
        out_shape=jax.ShapeDtypeStruct((bsz, n), F32),
        compiler_params=_params("parallel"),
        name="ada",
    )(c, w_ada, b_ada.reshape(1, n))


def _inproj_kernel(x_ref, mod_ref, g_ref, w_ref, wba_ref, alog_ref, dtb_ref,
                   proj_ref, gates_ref, h_scr):
    j = pl.program_id(2)

    @pl.when(j == 0)
    def _():
        x = x_ref[0]
        sh1 = mod_ref[0, 0:1, :]
        sc1 = mod_ref[0, 1:2, :]
        ms = jnp.mean(x * x, axis=-1, keepdims=True)
        h = x * lax.rsqrt(ms + EPS) * g_ref[...] * (1.0 + sc1) + sh1
        hb = h.astype(BF16)
        h_scr[...] = hb
        ba = jnp.dot(hb, wba_ref[...], preferred_element_type=F32)
        lane = lax.broadcasted_iota(jnp.int32, ba.shape, 1)
        beta = _sigmoid(ba)
        g = -jnp.exp(alog_ref[...]) * _softplus(ba + dtb_ref[...])
        gates_ref[0] = jnp.where(lane < GDN_HEADS, beta, g)

    proj_ref[0] = jnp.dot(h_scr[...], w_ref[...], preferred_element_type=F32).astype(BF16)


def _inproj(x, mod3, g_mix, w_main, w_ba, alog_vec, dtb_vec):
    bsz, s, d = x.shape
    tm, tn = 1024, 2688
    return pl.pallas_call(
        _inproj_kernel,
        grid=(bsz, s // tm, PROJ_W // tn),
        in_specs=[pl.BlockSpec((1, tm, d), lambda b, i, j: (b, i, 0)),
                  pl.BlockSpec((1, 6, d), lambda b, i, j: (b, 0, 0)),
                  pl.BlockSpec((1, d), lambda b, i, j: (0, 0)),
                  pl.BlockSpec((d, tn), lambda b, i, j: (0, j)),
                  pl.BlockSpec((d, LANES), lambda b, i, j: (0, 0)),
                  pl.BlockSpec((1, LANES), lambda b, i, j: (0, 0)),
                  pl.BlockSpec((1, LANES), lambda b, i, j: (0, 0))],
        out_specs=[pl.BlockSpec((1, tm, tn), lambda b, i, j: (b, i, j)),
                   pl.BlockSpec((1, tm, LANES), lambda b, i, j: (b, i, 0))],
        out_shape=[jax.ShapeDtypeStruct((bsz, s, PROJ_W), BF16),
                   jax.ShapeDtypeStruct((bsz, s, LANES), F32)],
        scratch_shapes=[pltpu.VMEM((tm, d), BF16)],
        compiler_params=_params("parallel", "parallel", "arbitrary"),
        name="inproj",
    )(x, mod3, g_mix.reshape(1, d), w_main, w_ba, alog_vec, dtb_vec)


GDN_C = 128
GDN_HG = 8
GDN_TAIL = 8


def _cumsum_rows(g):
    row = lax.broadcasted_iota(jnp.int32, g.shape, 0)
    sft = 1
    while sft < g.shape[0]:
        g = g + jnp.where(row >= sft, pltpu.roll(g, sft, 0), 0.0)
        sft *= 2
    return g


def _unit_lower_inverses(mats):
    n = mats[0].shape[0]
    row = lax.broadcasted_iota(jnp.int32, (n, n), 0)
    col = lax.broadcasted_iota(jnp.int32, (n, n), 1)
    eye = (row == col).astype(F32)
    ts = None
    b = 1
    while b < n:
        off = ((row // (2 * b)) == (col // (2 * b))) & ((row // b) % 2 == 1) & ((col // b) % 2 == 0)
        if b == 1:
            ts = [eye - jnp.where(off, a, 0.0) for a in mats]
        else:
            tb = [t.astype(BF16) for t in ts]
            inner = [jnp.dot(jnp.where(off, a, 0.0).astype(BF16), t, preferred_element_type=F32)
                     for a, t in zip(mats, tb)]
            ts = [t - jnp.dot(t16, i.astype(BF16), preferred_element_type=F32)
                  for t, t16, i in zip(ts, tb, inner)]
        b *= 2
    return ts


def _gdn_kernel(q_ref, k_ref, v_ref, z_ref, gates_ref, cw_ref, ng_ref, o_ref,
                qprev, kprev, vprev, state):
    sblk = pl.program_id(1)
    c = GDN_C
    nh = GDN_HEADS
    w = nh * GDN_DK

    @pl.when(sblk == 0)
    def _():
        state[...] = jnp.zeros_like(state)
        for buf in (qprev, kprev, vprev):
            buf[...] = jnp.zeros_like(buf)

    taps = GDN_CONV - 1
    si = lax.broadcasted_iota(jnp.int32, (taps * c, c), 0)
    sm = lax.broadcasted_iota(jnp.int32, (taps * c, c), 1)
    shift_mat = jnp.where(sm == si % c - (taps - si // c), 1.0, 0.0).astype(BF16)
    trow = lax.broadcasted_iota(jnp.int32, (GDN_TAIL, w), 0)
    conv = []
    for idx, (ref, prev) in enumerate(((q_ref, qprev), (k_ref, kprev), (v_ref, vprev))):
        cur = ref[0]
        cur32 = cur.astype(F32)
        delayed = jnp.dot(shift_mat, cur, preferred_element_type=F32)
        acc = cur32 * cw_ref[idx, taps:taps + 1, :]
        head = jnp.zeros((GDN_TAIL, w), F32)
        tail = prev[...]
        for j in range(taps):
            acc = acc + delayed[j * c:(j + 1) * c, :] * cw_ref[idx, j:j + 1, :]
            head = head + jnp.where(trow < taps - j, pltpu.roll(tail, taps - j, 0), 0.0) * cw_ref[idx, j:j + 1, :]
        acc = jnp.concatenate([acc[:GDN_TAIL] + head, acc[GDN_TAIL:]], axis=0)
        prev[...] = cur32[c - GDN_TAIL:, :]
        conv.append(_silu(acc))
    qc, kc, vc = conv

    gates = gates_ref[0]
    gcum = _cumsum_rows(gates)
    row = lax.broadcasted_iota(jnp.int32, (c, c), 0)
    col = lax.broadcasted_iota(jnp.int32, (c, c), 1)
    eye = row == col
    incl = col <= row
    strict = col < row

    for hg in range(0, nh, GDN_HG):
        heads = range(hg, hg + GDN_HG)
        hsl = {h: slice(h * GDN_DK, (h + 1) * GDN_DK) for h in heads}
        q = [qc[:, hsl[h]] for h in heads]
        k = [kc[:, hsl[h]] for h in heads]
        v = [vc[:, hsl[h]] for h in heads]
        q = [x * lax.rsqrt(jnp.sum(x * x, axis=-1, keepdims=True) + EPS) * (GDN_DK ** -0.5) for x in q]
        k = [x * lax.rsqrt(jnp.sum(x * x, axis=-1, keepdims=True) + EPS) for x in k]
        beta = [jnp.broadcast_to(gates[:, h:h + 1], (c, GDN_DK)) for h in heads]
        gc = [jnp.broadcast_to(gcum[:, nh + h:nh + h + 1], (c, GDN_DK)) for h in heads]
        gc_row = [jnp.sum(jnp.where(eye, x, 0.0), axis=0, keepdims=True) for x in gc]
        decay = [jnp.exp(jnp.where(incl, x - y, -jnp.inf)) for x, y in zip(gc, gc_row)]
        egc = [jnp.exp(x) for x in gc]
        kb = [x * b for x, b in zip(k, beta)]
        scores = [_mm_nt(jnp.concatenate([a, b], axis=0), x) for a, b, x in zip(kb, q, k)]
        a_mat = [jnp.where(strict, s[:c] * d, 0.0) for s, d in zip(scores, decay)]
        qk = [s[c:] * d for s, d in zip(scores, decay)]
        t_inv = _unit_lower_inverses(a_mat)
        sol = [_mm(t, jnp.concatenate([x * b, y * e], axis=1))
               for t, x, b, y, e in zip(t_inv, v, beta, kb, egc)]

        st = [state[h] for h in heads]
        ws = [_mm(jnp.concatenate([s[:, GDN_DK:], x * e], axis=0), m)
              for s, x, e, m in zip(sol, q, egc, st)]
        v_new = [s[:, :GDN_DK] - x[:c] for s, x in zip(sol, ws)]
        o = [x[c:] + _mm(a, b) for x, a, b in zip(ws, qk, v_new)]
        gc_last = [x[c - 1:c, :] for x in gc]
        k_dec = [x * jnp.exp(l - g) for x, l, g in zip(k, gc_last, gc)]
        for n, h in enumerate(heads):
            state[h] = st[n] * jnp.exp(gc_last[n]) + _mm_tn(k_dec[n], v_new[n])
            y = o[n] * lax.rsqrt(jnp.mean(o[n] * o[n], axis=-1, keepdims=True) + EPS) * ng_ref[...]
            o_ref[0, :, hsl[h]] = (y * _silu(z_ref[0, :, hsl[h]].astype(F32))).astype(o_ref.dtype)


def _gdn(proj, gates, conv_w3, norm_g):
    bsz, s, _ = proj.shape
    w = GDN_W

    def col_spec(col0):
        return pl.BlockSpec((1, GDN_C, w), lambda b, i: (b, i, col0 // w))

    return pl.pallas_call(
        _gdn_kernel,
        grid=(bsz, s // GDN_C),
        in_specs=[col_spec(COL_GQ), col_spec(COL_GK), col_spec(COL_GV), col_spec(COL_GZ),
                  pl.BlockSpec((1, GDN_C, LANES), lambda b, i: (b, i, 0)),
                  pl.BlockSpec((3, GDN_CONV, w), lambda b, i: (0, 0, 0)),
                  pl.BlockSpec((1, GDN_DK), lambda b, i: (0, 0))],
        out_specs=pl.BlockSpec((1, GDN_C, w), lambda b, i: (b, i, 0)),
        out_shape=jax.ShapeDtypeStruct((bsz, s, GDN_W), BF16),
        scratch_shapes=[pltpu.VMEM((GDN_TAIL, w), F32),
                        pltpu.VMEM((GDN_TAIL, w), F32),
                        pltpu.VMEM((GDN_TAIL, w), F32),
                        pltpu.VMEM((GDN_HEADS, GDN_DK, GDN_DK), F32)],
        compiler_params=_params("parallel", "arbitrary"),
        name="gdn",
    )(proj, proj, proj, proj, gates, conv_w3, norm_g.reshape(1, GDN_DK))


def _dil_kernel(q0, q1, q2, k0, k1, k2, v0, v1, v2, qg_ref, kg_ref, o_ref, qs, ks, vs, os_, ls):
    s = qs.shape[1]
    blk = DIL_BLOCK
    ngrp = len(DIL_PATTERN)
    for gi, (q_ref, k_ref, v_ref) in enumerate(((q0, k0, v0), (q1, k1, v1), (q2, k2, v2))):
        q = q_ref[0].astype(F32)
        k = k_ref[0].astype(F32)
        q = q * lax.rsqrt(jnp.mean(q * q, axis=-1, keepdims=True) + EPS) * qg_ref[...]
        k = k * lax.rsqrt(jnp.mean(k * k, axis=-1, keepdims=True) + EPS) * kg_ref[...]
        qs[gi] = q * (DIL_HD ** -0.5)
        ks[gi] = k
        vs[gi] = v_ref[0].astype(F32)

    qi = lax.broadcasted_iota(jnp.int32, (blk, blk), 0)
    kj = lax.broadcasted_iota(jnp.int32, (blk, blk), 1)
    cur_ok = kj <= qi
    prev_ok = kj >= qi

    items = []
    for gi, (win, dil) in enumerate(DIL_PATTERN):
        assert win // dil == blk and (s // dil) % blk == 0
        for r in range(dil):
            for n in range((s // dil) // blk):
                items.append((gi, dil, r, n))

    def rows(dil, r, m):
        if dil == 1:
            return pl.ds(m * blk, blk)
        return pl.ds(m * blk * dil + r, blk, stride=dil)

    for b0 in range(0, len(items), DIL_BATCH):
        batch = items[b0:b0 + DIL_BATCH]
        qb = [qs[gi, rows(dil, r, n), :] for gi, dil, r, n in batch]
        kc = [ks[gi, rows(dil, r, n), :] for gi, dil, r, n in batch]
        kp = [ks[gi, rows(dil, r, n - 1), :] if n > 0 else None for gi, dil, r, n in batch]
        s_cur = [jnp.where(cur_ok, _mm_nt(q, k), -jnp.inf) for q, k in zip(qb, kc)]
        s_prev = [None if k is None else jnp.where(prev_ok, _mm_nt(q, k), -jnp.inf) for q, k in zip(qb, kp)]
        m = [jnp.max(a, axis=-1, keepdims=True) if b is None else
             jnp.maximum(jnp.max(a, axis=-1, keepdims=True), jnp.max(b, axis=-1, keepdims=True))
             for a, b in zip(s_cur, s_prev)]
        p_cur = [jnp.exp(a - mm) for a, mm in zip(s_cur, m)]
        p_prev = [None if b is None else jnp.exp(b - mm) for b, mm in zip(s_prev, m)]
        den = [jnp.sum(a, axis=-1, keepdims=True) if b is None else
               jnp.sum(a, axis=-1, keepdims=True) + jnp.sum(b, axis=-1, keepdims=True)
               for a, b in zip(p_cur, p_prev)]
        o = [_mm(p, vs[gi, rows(dil, r, n), :]) for p, (gi, dil, r, n) in zip(p_cur, batch)]
        o = [a if p is None else a + _mm(p, vs[gi, rows(dil, r, n - 1), :])
             for a, p, (gi, dil, r, n) in zip(o, p_prev, batch)]
        for (gi, dil, r, n), a, d, mm in zip(batch, o, den, m):
            os_[gi, rows(dil, r, n), :] = a / d
            ls[gi, rows(dil, r, n), :] = jnp.broadcast_to(mm + jnp.log(d), (blk, DIL_HD))

    lse = [ls[gi] for gi in range(ngrp)]
    mx = functools.reduce(jnp.maximum, lse)
    ex = [jnp.exp(l - mx) for l in lse]
    tot = functools.reduce(lambda a, b: a + b, ex)
    y = functools.reduce(lambda a, b: a + b, [ex[gi] / tot * os_[gi] for gi in range(ngrp)])
    o_ref[0] = y.astype(o_ref.dtype)


def _dil(proj, q_norm_g, k_norm_g):
    bsz, s, _ = proj.shape
    ngrp = len(DIL_PATTERN)

    def specs(col0):
        blk0 = col0 // DIL_HD
        return [pl.BlockSpec((1, s, DIL_HD), lambda b, h, g=g: (b, 0, blk0 + g * DIL_HEADS_PER_GROUP + h))
                for g in range(ngrp)]

    gain = pl.BlockSpec((1, DIL_HD), lambda b, h: (0, 0))
    return pl.pallas_call(
        _dil_kernel,
        grid=(bsz, DIL_HEADS_PER_GROUP),
        in_specs=specs(COL_DQ) + specs(COL_DK) + specs(COL_DV) + [gain, gain],
        out_specs=pl.BlockSpec((1, s, DIL_HD), lambda b, h: (b, 0, h)),
        out_shape=jax.ShapeDtypeStruct((bsz, s, DIL_OUT_W), BF16),
        scratch_shapes=[pltpu.VMEM((ngrp, s, DIL_HD), F32) for _ in range(5)],
        compiler_params=_params("parallel", "parallel"),
        name="dilattn",
    )(*([proj] * 9), q_norm_g.reshape(1, DIL_HD), k_norm_g.reshape(1, DIL_HD))


def _route(logits, bias):
    e, tm = logits.shape
    per = e // N_GROUPS
    scores = _sigmoid(logits)
    sel = scores + bias
    neg = -jnp.inf
    sub = lax.broadcasted_iota(jnp.int32, (per, tm), 0)
    gs_rows = []
    for g in range(N_GROUPS):
        blk = sel[g * per:(g + 1) * per, :]
        m1 = jnp.max(blk, axis=0, keepdims=True)
        i1 = jnp.min(jnp.where(blk == m1, sub, per), axis=0, keepdims=True)
        m2 = jnp.max(jnp.where(sub == i1, neg, blk), axis=0, keepdims=True)
        gs_rows.append(m1 + m2)
    gs = jnp.concatenate(gs_rows, axis=0)
    gi = lax.broadcasted_iota(jnp.int32, (N_GROUPS, tm), 0)
    gsel = jnp.zeros((N_GROUPS, tm), F32)
    for _ in range(TOPK_GROUPS):
        m = jnp.max(gs, axis=0, keepdims=True)
        idx = jnp.min(jnp.where(gs == m, gi, N_GROUPS), axis=0, keepdims=True)
        hit = gi == idx
        gsel = jnp.where(hit, 1.0, gsel)
        gs = jnp.where(hit, neg, gs)
    cand = jnp.concatenate(
        [jnp.where(gsel[g:g + 1, :] > 0.0, sel[g * per:(g + 1) * per, :], neg) for g in range(N_GROUPS)], axis=0)
    ei = lax.broadcasted_iota(jnp.int32, (e, tm), 0)
    chosen = jnp.zeros((e, tm), F32)
    for _ in range(TOP_K):
        m = jnp.max(cand, axis=0, keepdims=True)
        idx = jnp.min(jnp.where(cand == m, ei, e), axis=0, keepdims=True)
        hit = ei == idx
        chosen = jnp.where(hit, scores, chosen)
        cand = jnp.where(hit, neg, cand)
    return chosen / jnp.sum(chosen, axis=0, keepdims=True) * ROUTE_SCALE


def _merge_kernel(x_ref, mod_ref, ya_ref, yb_ref, ga_ref, gb_ref, wg_ref, wd_ref, wo_ref,
                  gf_ref, wr_ref, rb_ref, x1_ref, h2_ref, comb_ref, rank_ref, cnt_ref):
    gt1 = mod_ref[0, 2:3, :]
    sh2 = mod_ref[0, 3:4, :]
    sc2 = mod_ref[0, 4:5, :]
    ua = jnp.dot(ya_ref[0], wg_ref[...], preferred_element_type=F32)
    ub = jnp.dot(yb_ref[0], wd_ref[...], preferred_element_type=F32)
    merged = _sigmoid(ga_ref[0].astype(F32)) * ua + _sigmoid(gb_ref[0].astype(F32)) * ub
    x1 = x_ref[0] + gt1 * _mm(merged, wo_ref[...])
    x1_ref[0] = x1
    h2 = x1 * lax.rsqrt(jnp.mean(x1 * x1, axis=-1, keepdims=True) + EPS) * gf_ref[...] * (1.0 + sc2) + sh2
    h2_hi = h2.astype(BF16)
    h2_ref[0] = h2_hi
    h2_lo = (h2 - h2_hi.astype(F32)).astype(BF16)
    nt_dims = (((1,), (1,)), ((), ()))
    both = lax.dot_general(wr_ref[...], h2_hi, nt_dims, preferred_element_type=F32)
    logits = (both[:N_EXPERTS] + both[N_EXPERTS:]
              + lax.dot_general(wr_ref[:N_EXPERTS, :], h2_lo, nt_dims, preferred_element_type=F32))
    comb = _route(logits, rb_ref[...])
    comb_ref[0] = comb
    tb = MOE_TB
    before = (lax.broadcasted_iota(jnp.int32, (tb, tb), 0) < lax.broadcasted_iota(jnp.int32, (tb, tb), 1))
    before = before.astype(BF16)
    for j in range(comb.shape[1] // tb):
        picked = (comb[:, j * tb:(j + 1) * tb] > 0.0).astype(F32)
        rank_ref[0, :, j * tb:(j + 1) * tb] = jnp.dot(picked.astype(BF16), before, preferred_element_type=F32)
        cnt_ref[0, j] = jnp.broadcast_to(jnp.sum(picked, axis=1, keepdims=True), (N_EXPERTS, LANES))


def _merge(x, mod3, y_a, y_b, proj, w_up_gdn, w_up_dil, w_out, g_ffn, w_router_t, router_bias):
    bsz, s, d = x.shape
    tm = 512
    row = lambda w: pl.BlockSpec((1, tm, w), lambda b, i: (b, i, 0))
    full = lambda a: pl.BlockSpec(a.shape, lambda b, i: (0,) * a.ndim)
    ex = pl.BlockSpec((1, N_EXPERTS, tm), lambda b, i: (b, 0, i))
    return pl.pallas_call(
        _merge_kernel,
        grid=(bsz, s // tm),
        in_specs=[row(d),
                  pl.BlockSpec((1, 6, d), lambda b, i: (b, 0, 0)),
                  row(GDN_W), row(DIL_OUT_W),
                  pl.BlockSpec((1, tm, d), lambda b, i: (b, i, COL_GA // d)),
                  pl.BlockSpec((1, tm, d), lambda b, i: (b, i, COL_GB // d)),
                  full(w_up_gdn), full(w_up_dil), full(w_out), full(g_ffn), full(w_router_t),
                  full(router_bias)],
        out_specs=[row(d), row(d), ex, ex,
                   pl.BlockSpec((1, tm // MOE_TB, N_EXPERTS, LANES), lambda b, i: (b, i, 0, 0))],
        out_shape=[jax.ShapeDtypeStruct((bsz, s, d), F32),
                   jax.ShapeDtypeStruct((bsz, s, d), BF16),
                   jax.ShapeDtypeStruct((bsz, N_EXPERTS, s), F32),
                   jax.ShapeDtypeStruct((bsz, N_EXPERTS, s), F32),
                   jax.ShapeDtypeStruct((bsz, s // MOE_TB, N_EXPERTS, LANES), F32)],
        compiler_params=_params("parallel", "parallel"),
        name="merge_router",
    )(x, mod3, y_a, y_b, proj, proj, w_up_gdn, w_up_dil, w_out, g_ffn, w_router_t, router_bias)


MOE_TB = 256
MOE_CH = 16
MOE_RB = MOE_TB * TOP_K + N_EXPERTS * MOE_CH
MOE_CPB = MOE_RB // MOE_CH
MOE_TM = 512
MOE_CT = MOE_RB
MOE_FT = 512
MOE_CPT = MOE_FT // MOE_CH


def _moe_plan(cnt):
    nb, ne = cnt.shape
    i32 = jnp.int32
    nch = (cnt + MOE_CH - 1) // MOE_CH
    seg0 = jnp.cumsum(nch, axis=1) - nch
    pad = jnp.zeros((nb, LANES - ne), i32)
    meta = jnp.stack([jnp.concatenate([seg0 * MOE_CH, pad + MOE_RB], axis=1),
                      jnp.concatenate([nch * MOE_CH, pad], axis=1)], axis=1).astype(F32)
    meta = jnp.pad(meta, ((0, 0), (0, 6), (0, 0)))
    nch_e = nch.T
    cum_e = jnp.cumsum(nch_e, axis=1)
    tot_e = cum_e[:, -1]
    tiles_e = (tot_e + MOE_CPT - 1) // MOE_CPT
    tile_end = jnp.cumsum(tiles_e)
    ntiles = tile_end[-1]
    max_tiles = (nb * MOE_CPB) // MOE_CPT + ne
    i = jnp.arange(max_tiles, dtype=i32)
    te = jnp.sum((tile_end[None, :] <= i[:, None]).astype(i32), axis=1)
    te = jnp.minimum(te, jnp.sum((tile_end <= ntiles - 1).astype(i32)))
    te = jnp.minimum(te, ne - 1)
    oh_te = (te[:, None] == jnp.arange(ne, dtype=i32)[None, :]).astype(i32)
    pick = lambda tab: jnp.sum(oh_te[:, :, None] * tab[None, :, :], axis=1)
    tile_off_t = jnp.sum(oh_te * (tile_end - tiles_e)[None, :], axis=1)
    tot_t = jnp.sum(oh_te * tot_e[None, :], axis=1)
    cum_t, nch_t, seg0_t = pick(cum_e), pick(nch_e), pick(seg0.T)
    q = (i - tile_off_t)[:, None] * MOE_CPT + jnp.arange(MOE_CPT, dtype=i32)[None, :]
    valid = (q < tot_t[:, None]) & (i < ntiles)[:, None]
    blk = jnp.minimum(jnp.sum((cum_t[:, None, :] <= q[:, :, None]).astype(i32), axis=-1), nb - 1)
    oh_b = (blk[:, :, None] == jnp.arange(nb, dtype=i32)[None, None, :]).astype(i32)
    before = jnp.sum(oh_b * (cum_t - nch_t)[:, None, :], axis=-1)
    src = blk * MOE_CPB + jnp.sum(oh_b * seg0_t[:, None, :], axis=-1) + (q - before)
    spare = nb * MOE_CPB + (i % 2)[:, None] * MOE_CPT + jnp.arange(MOE_CPT, dtype=i32)[None, :]
    tbl_in = jnp.where(valid, src, nb * MOE_CPB + 2 * MOE_CPT).astype(i32).reshape(-1)
    tbl_out = jnp.where(valid, src, spare).astype(i32).reshape(-1)
    return meta, te.astype(i32), tbl_in, tbl_out, ntiles.astype(i32).reshape(1)


def _onehot_rows(meta_ref, rhs, r0, nrows, weighted):
    start = meta_ref[0, 0:1, :]
    plen = meta_ref[0, 1:2, :]
    r = (lax.broadcasted_iota(jnp.int32, (nrows, LANES), 0) + r0).astype(F32)
    owner = jnp.where((r >= start) & (r < start + plen), 1.0, 0.0)
    pos = r[:, :1] - jnp.sum(owner * start, axis=1, keepdims=True)
    got = jnp.dot(owner[:, :N_EXPERTS].astype(BF16), rhs, preferred_element_type=F32)
    tb = MOE_TB
    hit = (got[:, :tb] == pos) & (got[:, tb:2 * tb] > 0.5)
    return jnp.where(hit, got[:, 2 * tb:] if weighted else 1.0, 0.0).astype(BF16)


def _route_rhs(rank_ref, comb_ref, weighted):
    comb = comb_ref[0]
    picked = jnp.where(comb > 0.0, 1.0, 0.0)
    parts = [rank_ref[0], picked] + ([comb] if weighted else [])
    return jnp.concatenate(parts, axis=1).astype(BF16)


def _dispatch_kernel(meta_ref, h_ref, rank_ref, comb_ref, x_ref):
    @pl.when(pl.program_id(0) < pl.num_programs(0) - 1)
    def _():
        rhs = _route_rhs(rank_ref, comb_ref, weighted=False)
        h = h_ref[0]
        for r0 in range(0, MOE_RB, MOE_TM):
            n = min(MOE_TM, MOE_RB - r0)
            p = _onehot_rows(meta_ref, rhs, r0, n, weighted=False)
            x_ref[0, r0:r0 + n, :] = jnp.dot(p, h, preferred_element_type=F32).astype(BF16)

    @pl.when(pl.program_id(0) == pl.num_programs(0) - 1)
    def _():
        x_ref[...] = jnp.zeros_like(x_ref)


def _dispatch(h2, rank_t, comb_t, meta):
    bsz, s, d = h2.shape
    nsb = s // MOE_TB
    nblk = bsz * nsb
    tok = lambda i: (jnp.minimum(i, nblk - 1) // nsb, jnp.minimum(i, nblk - 1) % nsb)
    ex = pl.BlockSpec((1, N_EXPERTS, MOE_TB), lambda i: (tok(i)[0], 0, tok(i)[1]))
    return pl.pallas_call(
        _dispatch_kernel,
        grid=(nblk + 1,),
        in_specs=[pl.BlockSpec((1, 8, LANES), lambda i: (jnp.minimum(i, nblk - 1), 0, 0)),
                  pl.BlockSpec((1, MOE_TB, d), lambda i: (tok(i)[0], tok(i)[1], 0)),
                  ex, ex],
        out_specs=pl.BlockSpec((1, MOE_RB, d), lambda i: (i, 0, 0)),
        out_shape=jax.ShapeDtypeStruct((nblk + 1, MOE_RB, d), BF16),
        compiler_params=_params("parallel"),
        name="moe_dispatch",
    )(meta, h2, rank_t, comb_t)


def _ffn_kernel(te_ref, tin_ref, tout_ref, nt_ref, x_hbm, wg_ref, wu_ref, wd_ref, y_hbm,
                xbuf, ybuf, wgb, wub, wdb, sem_in, sem_out):
    i = pl.program_id(0)
    nt = nt_ref[0]
    slot = i % 2

    def copy_in(t, s, j):
        return pltpu.make_async_copy(x_hbm.at[tin_ref[t * MOE_CPT + j]],
                                     xbuf.at[s, pl.ds(j * MOE_CH, MOE_CH)], sem_in.at[s])

    def copy_out(t, s, j):
        return pltpu.make_async_copy(ybuf.at[s, pl.ds(j * MOE_CH, MOE_CH)],
                                     y_hbm.at[tout_ref[t * MOE_CPT + j]], sem_out.at[s])

    @pl.when(i == 0)
    def _():
        for j in range(MOE_CPT):
            copy_in(0, 0, j).start()

    @pl.when(i + 1 < nt)
    def _():
        for j in range(MOE_CPT):
            copy_in(i + 1, 1 - slot, j).start()

    @pl.when(i < nt)
    def _():
        for j in range(MOE_CPT):
            copy_in(i, slot, j).wait()

        @pl.when(i >= 2)
        def _():
            for j in range(MOE_CPT):
                copy_out(i - 2, slot, j).wait()

        @pl.when(jnp.logical_or(i == 0, te_ref[i] != te_ref[jnp.maximum(i - 1, 0)]))
        def _():
            wgb[...] = wg_ref[0].astype(BF16)
            wub[...] = wu_ref[0].astype(BF16)
            wdb[...] = wd_ref[0].astype(BF16)

        x = xbuf[slot]
        a = jnp.dot(x, wgb[...], preferred_element_type=F32)
        u = jnp.dot(x, wub[...], preferred_element_type=F32)
        ybuf[slot] = jnp.dot((_silu(a) * u).astype(BF16), wdb[...], preferred_element_type=F32).astype(BF16)
        for j in range(MOE_CPT):
            copy_out(i, slot, j).start()

        @pl.when(i == nt - 1)
        def _():
            @pl.when(i >= 1)
            def _():
                for j in range(MOE_CPT):
                    copy_out(i - 1, 1 - slot, j).wait()
            for j in range(MOE_CPT):
                copy_out(i, slot, j).wait()


def _ffn(x_rows, w_eg, w_eu, w_ed, te, tbl_in, tbl_out, ntiles):
    nblk, rb, d = x_rows.shape
    ne, _, de = w_eg.shape
    xc = x_rows.reshape(nblk * MOE_CPB, MOE_CH, d)
    max_tiles = te.shape[0]
    y = pl.pallas_call(
        _ffn_kernel,
        grid_spec=pltpu.PrefetchScalarGridSpec(
            num_scalar_prefetch=4,
            grid=(max_tiles,),
            in_specs=[pl.BlockSpec(memory_space=pl.ANY),
                      pl.BlockSpec((1, d, de), lambda i, te, *_: (te[i], 0, 0)),
                      pl.BlockSpec((1, d, de), lambda i, te, *_: (te[i], 0, 0)),
                      pl.BlockSpec((1, de, d), lambda i, te, *_: (te[i], 0, 0))],
            out_specs=pl.BlockSpec(memory_space=pl.ANY),
            scratch_shapes=[pltpu.VMEM((2, MOE_FT, d), BF16), pltpu.VMEM((2, MOE_FT, d), BF16),
                            pltpu.VMEM((d, de), BF16), pltpu.VMEM((d, de), BF16), pltpu.VMEM((de, d), BF16),
                            pltpu.SemaphoreType.DMA((2,)), pltpu.SemaphoreType.DMA((2,))]),
        out_shape=jax.ShapeDtypeStruct(xc.shape, BF16),
        input_output_aliases={4: 0},
        compiler_params=_params("arbitrary"),
        name="moe_ffn",
    )(te, tbl_in, tbl_out, ntiles, xc, w_eg, w_eu, w_ed)
    return y.reshape(nblk, rb, d)


def _combine_kernel(meta_ref, y_ref, rank_ref, comb_ref, h_ref, x1_ref, mod_ref,
                    sg_ref, su_ref, sd_ref, o_ref):
    rhs = _route_rhs(rank_ref, comb_ref, weighted=True)
    h = h_ref[0]
    a = jnp.dot(h, sg_ref[...], preferred_element_type=F32)
    u = jnp.dot(h, su_ref[...], preferred_element_type=F32)
    acc = _mm(_silu(a) * u, sd_ref[...])
    for r0 in range(0, MOE_RB, MOE_CT):
        n = min(MOE_CT, MOE_RB - r0)
        p = _onehot_rows(meta_ref, rhs, r0, n, weighted=True)
        acc = acc + lax.dot_general(p, y_ref[0, r0:r0 + n, :], (((0,), (0,)), ((), ())),
                                    preferred_element_type=F32)
    o_ref[0] = x1_ref[0] + mod_ref[0, 5:6, :] * acc


def _combine(y_rows, rank_t, comb_t, h2, x1, mod3, sg, su, sd, meta):
    bsz, s, d = x1.shape
    nsb = s // MOE_TB
    tok = lambda w: pl.BlockSpec((1, MOE_TB, w), lambda b, i: (b, i, 0))
    ex = pl.BlockSpec((1, N_EXPERTS, MOE_TB), lambda b, i: (b, 0, i))
    full = lambda a: pl.BlockSpec(a.shape, lambda b, i: (0,) * a.ndim)
    return pl.pallas_call(
        _combine_kernel,
        grid=(bsz, nsb),
        in_specs=[pl.BlockSpec((1, 8, LANES), lambda b, i: (b * nsb + i, 0, 0)),
                  pl.BlockSpec((1, MOE_RB, d), lambda b, i: (b * nsb + i, 0, 0)),
                  ex, ex, tok(d), tok(d),
                  pl.BlockSpec((1, 6, d), lambda b, i: (b, 0, 0)),
                  full(sg), full(su), full(sd)],
        out_specs=tok(d),
        out_shape=jax.ShapeDtypeStruct((bsz, s, d), F32),
        compiler_params=_params("parallel", "parallel"),
        name="moe_combine",
    )(meta, y_rows, rank_t, comb_t, h2, x1, mod3, sg, su, sd)


def _moe(h2, comb_t, rank_t, cnt, x1, mod3, w_eg, w_eu, w_ed, sg, su, sd):
    meta, te, tbl_in, tbl_out, ntiles = _moe_plan(cnt)
    x_rows = _dispatch(h2, rank_t, comb_t, meta)
    y_rows = _ffn(x_rows, w_eg, w_eu, w_ed, te, tbl_in, tbl_out, ntiles)
    return _combine(y_rows, rank_t, comb_t, h2, x1, mod3, sg, su, sd, meta)


def _layer(x, cmod, g_mix, w_in, conv_w, a_log, dt_bias, norm_g, qn_g, kn_g, w_up_gdn, w_up_dil, w_out,
           g_ffn, w_router, router_bias, w_eg, w_eu, w_ed, w_sg, w_su, w_sd):
    bsz, s, d = x.shape
    mod3 = cmod.reshape(bsz, 6, d)
    o_ba = 4 * GDN_W
    o_dq = o_ba + 2 * GDN_HEADS
    o_ga = o_dq + 3 * DIL_W
    w_in16 = w_in.astype(BF16)
    w_main = jnp.concatenate([w_in16[:, :o_ba], w_in16[:, o_ga:], w_in16[:, o_dq:o_ga]], axis=1)
    pad_hi = LANES - 2 * GDN_HEADS
    w_ba = jnp.pad(w_in16[:, o_ba:o_dq], ((0, 0), (0, pad_hi)))
    alog_vec = jnp.pad(a_log, (GDN_HEADS, pad_hi)).reshape(1, LANES)
    dtb_vec = jnp.pad(dt_bias, (GDN_HEADS, pad_hi)).reshape(1, LANES)
    proj, gates = _inproj(x, mod3, g_mix, w_main, w_ba, alog_vec, dtb_vec)

    conv_w3 = conv_w.reshape(GDN_CONV, 3, GDN_W).transpose(1, 0, 2)
    y_a = _gdn(proj, gates, conv_w3, norm_g)
    y_b = _dil(proj, qn_g, kn_g)

    wr_t = w_router.T
    wr_hi = wr_t.astype(BF16)
    w_router_split = jnp.concatenate([wr_hi, (wr_t - wr_hi.astype(F32)).astype(BF16)], axis=0)
    x1, h2, comb_t, rank_t, cnt = _merge(x, mod3, y_a, y_b, proj, w_up_gdn.astype(BF16),
                                         w_up_dil.astype(BF16), w_out.astype(BF16), g_ffn.reshape(1, d),
                                         w_router_split, router_bias.reshape(N_EXPERTS, 1))
    cnt = cnt[..., 0].astype(jnp.int32).reshape(-1, N_EXPERTS)
    return _moe(h2, comb_t, rank_t, cnt, x1, mod3, w_eg, w_eu, w_ed,
                w_sg.astype(BF16), w_su.astype(BF16), w_sd.astype(BF16))


def kernel(x, c, w_ada, b_ada, g_mix, w_in, gdn_conv_w, gdn_a_log, gdn_dt_bias, gdn_norm_g, dil_q_norm_g, dil_k_norm_g, w_up_gdn, w_up_dil, w_out, g_ffn, w_router, router_bias, w_exp_gate, w_exp_up, w_exp_down, w_sh_gate, w_sh_up, w_sh_down):
    for l in range(w_ada.shape[0]):
        cmod = _ada(c, w_ada[l], b_ada[l])
        x = _layer(x, cmod, g_mix[l], w_in[l], gdn_conv_w[l], gdn_a_log[l], gdn_dt_bias[l], gdn_norm_g[l],
                   dil_q_norm_g[l], dil_k_norm_g[l], w_up_gdn[l], w_up_dil[l], w_out[l], g_ffn[l],
                   w_router[l], router_bias[l], w_exp_gate[l], w_exp_up[l], w_exp_down[l],
                   w_sh_gate[l], w_sh_up[l], w_sh_down[l])
    return x
```

```python
import functools

import jax
import jax.numpy as jnp
from jax import lax
from jax.experimental import pallas as pl
from jax.experimental.pallas import tpu as pltpu

F32 = jnp.float32
BF16 = jnp.bfloat16

D_MODEL = 1024
GDN_HEADS = 8
GDN_DK = 128
GDN_CONV = 4
DIL_PATTERN = ((128, 1), (512, 4), (2048, 16))
DIL_HEADS_PER_GROUP = 4
DIL_HD = 128
DIL_BLOCK = 128
DIL_BATCH = 8
N_EXPERTS = 64
TOP_K = 8
N_GROUPS = 8
TOPK_GROUPS = 4
D_EXPERT = 256
ROUTE_SCALE = 2.5
EPS = 1e-6

GDN_W = GDN_HEADS * GDN_DK
DIL_HEADS = len(DIL_PATTERN) * DIL_HEADS_PER_GROUP
DIL_W = DIL_HEADS * DIL_HD
DIL_OUT_W = DIL_HEADS_PER_GROUP * DIL_HD
COL_GQ, COL_GK, COL_GV, COL_GZ = 0, GDN_W, 2 * GDN_W, 3 * GDN_W
COL_GA = 4 * GDN_W
COL_GB = COL_GA + D_MODEL
COL_DQ = COL_GB + D_MODEL
COL_DK = COL_DQ + DIL_W
COL_DV = COL_DK + DIL_W
PROJ_W = COL_DV + DIL_W
LANES = 128

VMEM_LIMIT = 56 * 1024 * 1024


def _params(*sem):
    return pltpu.CompilerParams(dimension_semantics=sem, vmem_limit_bytes=VMEM_LIMIT)


def _sigmoid(x):
    return 1.0 / (1.0 + jnp.exp(-x))


def _silu(x):
    return x * _sigmoid(x)


def _softplus(x):
    return jnp.maximum(x, 0.0) + jnp.log(1.0 + jnp.exp(-jnp.abs(x)))


def _mm(a, b):
    return jnp.dot(a.astype(BF16), b.astype(BF16), preferred_element_type=F32)


def _mm_nt(a, b):
    return lax.dot_general(a.astype(BF16), b.astype(BF16), (((1,), (1,)), ((), ())),
                           preferred_element_type=F32)


def _mm_tn(a, b):
    return lax.dot_general(a.astype(BF16), b.astype(BF16), (((0,), (0,)), ((), ())),
                           preferred_element_type=F32)


def _ada_kernel(c_ref, w_ref, b_ref, o_ref):
    cs = _silu(c_ref[...])
    w = w_ref[...]
    cs_hi = cs.astype(BF16)
    w_hi = w.astype(BF16)
    cs_lo = (cs - cs_hi.astype(F32)).astype(BF16)
    w_lo = (w - w_hi.astype(F32)).astype(BF16)
    dot = lambda a, b: jnp.dot(a, b, preferred_element_type=F32)
    o_ref[...] = dot(cs_hi, w_hi) + dot(cs_lo, w_hi) + dot(cs_hi, w_lo) + b_ref[...]


def _ada(c, w_ada, b_ada):
    bsz, d = c.shape
    n = w_ada.shape[1]
    tn = 1536
    return pl.pallas_call(
        _ada_kernel,
        grid=(n // tn,),
        in_specs=[pl.BlockSpec((bsz, d), lambda j: (0, 0)),
                  pl.BlockSpec((d, tn), lambda j: (0, j)),
                  pl.BlockSpec((1, tn), lambda j: (0, j))],
        out_specs=pl.BlockSpec((bsz, tn), lambda j: (0, j)),
        out_shape=jax.ShapeDtypeStruct((bsz, n), F32),
        compiler_params=_params("parallel"),
        name="ada",
    )(c, w_ada, b_ada.reshape(1, n))


def _inproj_kernel(x_ref, mod_ref, g_ref, w_ref, wba_ref, alog_ref, dtb_ref,
                   proj_ref, gates_ref, h_scr):
    j = pl.program_id(2)

    @pl.when(j == 0)
    def _():
        x = x_ref[0]
        sh1 = mod_ref[0, 0:1, :]
        sc1 = mod_ref[0, 1:2, :]
        ms = jnp.mean(x * x, axis=-1, keepdims=True)
        h = x * lax.rsqrt(ms + EPS) * g_ref[...] * (1.0 + sc1) + sh1
        hb = h.astype(BF16)
        h_scr[...] = hb
        ba = jnp.dot(hb, wba_ref[...], preferred_element_type=F32)
        lane = lax.broadcasted_iota(jnp.int32, ba.shape, 1)
        beta = _sigmoid(ba)
        g = -jnp.exp(alog_ref[...]) * _softplus(ba + dtb_ref[...])
        gates_ref[0] = jnp.where(lane < GDN_HEADS, beta, g)

    proj_ref[0] = jnp.dot(h_scr[...], w_ref[...], preferred_element_type=F32).astype(BF16)


def _inproj(x, mod3, g_mix, w_main, w_ba, alog_vec, dtb_vec):
    bsz, s, d = x.shape
    tm, tn = 1024, 2688
    return pl.pallas_call(
        _inproj_kernel,
        grid=(bsz, s // tm, PROJ_W // tn),
        in_specs=[pl.BlockSpec((1, tm, d), lambda b, i, j: (b, i, 0)),
                  pl.BlockSpec((1, 6, d), lambda b, i, j: (b, 0, 0)),
                  pl.BlockSpec((1, d), lambda b, i, j: (0, 0)),
                  pl.BlockSpec((d, tn), lambda b, i, j: (0, j)),
                  pl.BlockSpec((d, LANES), lambda b, i, j: (0, 0)),
                  pl.BlockSpec((1, LANES), lambda b, i, j: (0, 0)),
                  pl.BlockSpec((1, LANES), lambda b, i, j: (0, 0))],
        out_specs=[pl.BlockSpec((1, tm, tn), lambda b, i, j: (b, i, j)),
                   pl.BlockSpec((1, tm, LANES), lambda b, i, j: (b, i, 0))],
        out_shape=[jax.ShapeDtypeStruct((bsz, s, PROJ_W), BF16),
                   jax.ShapeDtypeStruct((bsz, s, LANES), F32)],
        scratch_shapes=[pltpu.VMEM((tm, d), BF16)],
        compiler_params=_params("parallel", "parallel", "arbitrary"),
        name="inproj",
    )(x, mod3, g_mix.reshape(1, d), w_main, w_ba, alog_vec, dtb_vec)


GDN_C = 128
GDN_HG = 8
GDN_TAIL = 8


def _cumsum_rows(g):
    row = lax.broadcasted_iota(jnp.int32, g.shape, 0)
    sft = 1
    while sft < g.shape[0]:
        g = g + jnp.where(row >= sft, pltpu.roll(g, sft, 0), 0.0)
        sft *= 2
    return g


def _unit_lower_inverses(mats):
    n = mats[0].shape[0]
    row = lax.broadcasted_iota(jnp.int32, (n, n), 0)
    col = lax.broadcasted_iota(jnp.int32, (n, n), 1)
    eye = (row == col).astype(F32)
    ts = None
    b = 1
    while b < n:
        off = ((row // (2 * b)) == (col // (2 * b))) & ((row // b) % 2 == 1) & ((col // b) % 2 == 0)
        if b == 1:
            ts = [eye - jnp.where(off, a, 0.0) for a in mats]
        else:
            tb = [t.astype(BF16) for t in ts]
            inner = [jnp.dot(jnp.where(off, a, 0.0).astype(BF16), t, preferred_element_type=F32)
                     for a, t in zip(mats, tb)]
            ts = [t - jnp.dot(t16, i.astype(BF16), preferred_element_type=F32)
                  for t, t16, i in zip(ts, tb, inner)]
        b *= 2
    return ts


def _gdn_kernel(q_ref, k_ref, v_ref, z_ref, gates_ref, cw_ref, ng_ref, o_ref,
                qprev, kprev, vprev, state):
    sblk = pl.program_id(1)
    c = GDN_C
    nh = GDN_HEADS
    w = nh * GDN_DK

    @pl.when(sblk == 0)
    def _():
        state[...] = jnp.zeros_like(state)
        for buf in (qprev, kprev, vprev):
            buf[...] = jnp.zeros_like(buf)

    taps = GDN_CONV - 1
    si = lax.broadcasted_iota(jnp.int32, (taps * c, c), 0)
    sm = lax.broadcasted_iota(jnp.int32, (taps * c, c), 1)
    shift_mat = jnp.where(sm == si % c - (taps - si // c), 1.0, 0.0).astype(BF16)
    trow = lax.broadcasted_iota(jnp.int32, (GDN_TAIL, w), 0)
    conv = []
    for idx, (ref, prev) in enumerate(((q_ref, qprev), (k_ref, kprev), (v_ref, vprev))):
        cur = ref[0]
        cur32 = cur.astype(F32)
        delayed = jnp.dot(shift_mat, cur, preferred_element_type=F32)
        acc = cur32 * cw_ref[idx, taps:taps + 1, :]
        head = jnp.zeros((GDN_TAIL, w), F32)
        tail = prev[...]
        for j in range(taps):
            acc = acc + delayed[j * c:(j + 1) * c, :] * cw_ref[idx, j:j + 1, :]
            head = head + jnp.where(trow < taps - j, pltpu.roll(tail, taps - j, 0), 0.0) * cw_ref[idx, j:j + 1, :]
        acc = jnp.concatenate([acc[:GDN_TAIL] + head, acc[GDN_TAIL:]], axis=0)
        prev[...] = cur32[c - GDN_TAIL:, :]
        conv.append(_silu(acc))
    qc, kc, vc = conv

    gates = gates_ref[0]
    gcum = _cumsum_rows(gates)
    row = lax.broadcasted_iota(jnp.int32, (c, c), 0)
    col = lax.broadcasted_iota(jnp.int32, (c, c), 1)
    eye = row == col
    incl = col <= row
    strict = col < row

    for hg in range(0, nh, GDN_HG):
        heads = range(hg, hg + GDN_HG)
        hsl = {h: slice(h * GDN_DK, (h + 1) * GDN_DK) for h in heads}
        q = [qc[:, hsl[h]] for h in heads]
        k = [kc[:, hsl[h]] for h in heads]
        v = [vc[:, hsl[h]] for h in heads]
        q = [x * lax.rsqrt(jnp.sum(x * x, axis=-1, keepdims=True) + EPS) * (GDN_DK ** -0.5) for x in q]
        k = [x * lax.rsqrt(jnp.sum(x * x, axis=-1, keepdims=True) + EPS) for x in k]
        beta = [jnp.broadcast_to(gates[:, h:h + 1], (c, GDN_DK)) for h in heads]
        gc = [jnp.broadcast_to(gcum[:, nh + h:nh + h + 1], (c, GDN_DK)) for h in heads]
        gc_row = [jnp.sum(jnp.where(eye, x, 0.0), axis=0, keepdims=True) for x in gc]
        decay = [jnp.exp(jnp.where(incl, x - y, -jnp.inf)) for x, y in zip(gc, gc_row)]
        egc = [jnp.exp(x) for x in gc]
        kb = [x * b for x, b in zip(k, beta)]
        scores = [_mm_nt(jnp.concatenate([a, b], axis=0), x) for a, b, x in zip(kb, q, k)]
        a_mat = [jnp.where(strict, s[:c] * d, 0.0) for s, d in zip(scores, decay)]
        qk = [s[c:] * d for s, d in zip(scores, decay)]
        t_inv = _unit_lower_inverses(a_mat)
        sol = [_mm(t, jnp.concatenate([x * b, y * e], axis=1))
               for t, x, b, y, e in zip(t_inv, v, beta, kb, egc)]

        st = [state[h] for h in heads]
        ws = [_mm(jnp.concatenate([s[:, GDN_DK:], x * e], axis=0), m)
              for s, x, e, m in zip(sol, q, egc, st)]
        v_new = [s[:, :GDN_DK] - x[:c] for s, x in zip(sol, ws)]
        o = [x[c:] + _mm(a, b) for x, a, b in zip(ws, qk, v_new)]
        gc_last = [x[c - 1:c, :] for x in gc]
        k_dec = [x * jnp.exp(l - g) for x, l, g in zip(k, gc_last, gc)]
        for n, h in enumerate(heads):
            state[h] = st[n] * jnp.exp(gc_last[n]) + _mm_tn(k_dec[n], v_new[n])
            y = o[n] * lax.rsqrt(jnp.mean(o[n] * o[n], axis=-1, keepdims=True) + EPS) * ng_ref[...]
            o_ref[0, :, hsl[h]] = (y * _silu(z_ref[0, :, hsl[h]].astype(F32))).astype(o_ref.dtype)


def _gdn(proj, gates, conv_w3, norm_g):
    bsz, s, _ = proj.shape
    w = GDN_W

    def col_spec(col0):
        return pl.BlockSpec((1, GDN_C, w), lambda b, i: (b, i, col0 // w))

    return pl.pallas_call(
        _gdn_kernel,
        grid=(bsz, s // GDN_C),
        in_specs=[col_spec(COL_GQ), col_spec(COL_GK), col_spec(COL_GV), col_spec(COL_GZ),
                  pl.BlockSpec((1, GDN_C, LANES), lambda b, i: (b, i, 0)),
                  pl.BlockSpec((3, GDN_CONV, w), lambda b, i: (0, 0, 0)),
                  pl.BlockSpec((1, GDN_DK), lambda b, i: (0, 0))],
        out_specs=pl.BlockSpec((1, GDN_C, w), lambda b, i: (b, i, 0)),
        out_shape=jax.ShapeDtypeStruct((bsz, s, GDN_W), BF16),
        scratch_shapes=[pltpu.VMEM((GDN_TAIL, w), F32),
                        pltpu.VMEM((GDN_TAIL, w), F32),
                        pltpu.VMEM((GDN_TAIL, w), F32),
                        pltpu.VMEM((GDN_HEADS, GDN_DK, GDN_DK), F32)],
        compiler_params=_params("parallel", "arbitrary"),
        name="gdn",
    )(proj, proj, proj, proj, gates, conv_w3, norm_g.reshape(1, GDN_DK))


def _dil_kernel(q0, q1, q2, k0, k1, k2, v0, v1, v2, qg_ref, kg_ref, o_ref, qs, ks, vs, os_, ls):
    s = qs.shape[1]
    blk = DIL_BLOCK
    ngrp = len(DIL_PATTERN)
    for gi, (q_ref, k_ref, v_ref) in enumerate(((q0, k0, v0), (q1, k1, v1), (q2, k2, v2))):
        q = q_ref[0].astype(F32)
        k = k_ref[0].astype(F32)
        q = q * lax.rsqrt(jnp.mean(q * q, axis=-1, keepdims=True) + EPS) * qg_ref[...]
        k = k * lax.rsqrt(jnp.mean(k * k, axis=-1, keepdims=True) + EPS) * kg_ref[...]
        qs[gi] = q * (DIL_HD ** -0.5)
        ks[gi] = k
        vs[gi] = v_ref[0].astype(F32)

    qi = lax.broadcasted_iota(jnp.int32, (blk, blk), 0)
    kj = lax.broadcasted_iota(jnp.int32, (blk, blk), 1)
    cur_ok = kj <= qi
    prev_ok = kj >= qi

    items = []
    for gi, (win, dil) in enumerate(DIL_PATTERN):
        assert win // dil == blk and (s // dil) % blk == 0
        for r in range(dil):
            for n in range((s // dil) // blk):
                items.append((gi, dil, r, n))

    def rows(dil, r, m):
        if dil == 1:
            return pl.ds(m * blk, blk)
        return pl.ds(m * blk * dil + r, blk, stride=dil)

    for b0 in range(0, len(items), DIL_BATCH):
        batch = items[b0:b0 + DIL_BATCH]
        qb = [qs[gi, rows(dil, r, n), :] for gi, dil, r, n in batch]
        kc = [ks[gi, rows(dil, r, n), :] for gi, dil, r, n in batch]
        kp = [ks[gi, rows(dil, r, n - 1), :] if n > 0 else None for gi, dil, r, n in batch]
        s_cur = [jnp.where(cur_ok, _mm_nt(q, k), -jnp.inf) for q, k in zip(qb, kc)]
        s_prev = [None if k is None else jnp.where(prev_ok, _mm_nt(q, k), -jnp.inf) for q, k in zip(qb, kp)]
        m = [jnp.max(a, axis=-1, keepdims=True) if b is None else
             jnp.maximum(jnp.max(a, axis=-1, keepdims=True), jnp.max(b, axis=-1, keepdims=True))
             for a, b in zip(s_cur, s_prev)]
        p_cur = [jnp.exp(a - mm) for a, mm in zip(s_cur, m)]
        p_prev = [None if b is None else jnp.exp(b - mm) for b, mm in zip(s_prev, m)]
        den = [jnp.sum(a, axis=-1, keepdims=True) if b is None else
               jnp.sum(a, axis=-1, keepdims=True) + jnp.sum(b, axis=-1, keepdims=True)
               for a, b in zip(p_cur, p_prev)]
        o = [_mm(p, vs[gi, rows(dil, r, n), :]) for p, (gi, dil, r, n) in zip(p_cur, batch)]
        o = [a if p is None else a + _mm(p, vs[gi, rows(dil, r, n - 1), :])
             for a, p, (gi, dil, r, n) in zip(o, p_prev, batch)]
        for (gi, dil, r, n), a, d, mm in zip(batch, o, den, m):
            os_[gi, rows(dil, r, n), :] = a / d
            ls[gi, rows(dil, r, n), :] = jnp.broadcast_to(mm + jnp.log(d), (blk, DIL_HD))

    lse = [ls[gi] for gi in range(ngrp)]
    mx = functools.reduce(jnp.maximum, lse)
    ex = [jnp.exp(l - mx) for l in lse]
    tot = functools.reduce(lambda a, b: a + b, ex)
    y = functools.reduce(lambda a, b: a + b, [ex[gi] / tot * os_[gi] for gi in range(ngrp)])
    o_ref[0] = y.astype(o_ref.dtype)


def _dil(proj, q_norm_g, k_norm_g):
    bsz, s, _ = proj.shape
    ngrp = len(DIL_PATTERN)

    def specs(col0):
        blk0 = col0 // DIL_HD
        return [pl.BlockSpec((1, s, DIL_HD), lambda b, h, g=g: (b, 0, blk0 + g * DIL_HEADS_PER_GROUP + h))
                for g in range(ngrp)]

    gain = pl.BlockSpec((1, DIL_HD), lambda b, h: (0, 0))
    return pl.pallas_call(
        _dil_kernel,
        grid=(bsz, DIL_HEADS_PER_GROUP),
        in_specs=specs(COL_DQ) + specs(COL_DK) + specs(COL_DV) + [gain, gain],
        out_specs=pl.BlockSpec((1, s, DIL_HD), lambda b, h: (b, 0, h)),
        out_shape=jax.ShapeDtypeStruct((bsz, s, DIL_OUT_W), BF16),
        scratch_shapes=[pltpu.VMEM((ngrp, s, DIL_HD), F32) for _ in range(5)],
        compiler_params=_params("parallel", "parallel"),
        name="dilattn",
    )(*([proj] * 9), q_norm_g.reshape(1, DIL_HD), k_norm_g.reshape(1, DIL_HD))


def _route(logits, bias):
    e, tm = logits.shape
    per = e // N_GROUPS
    scores = _sigmoid(logits)
    sel = scores + bias
    neg = -jnp.inf
    sub = lax.broadcasted_iota(jnp.int32, (per, tm), 0)
    gs_rows = []
    for g in range(N_GROUPS):
        blk = sel[g * per:(g + 1) * per, :]
        m1 = jnp.max(blk, axis=0, keepdims=True)
        i1 = jnp.min(jnp.where(blk == m1, sub, per), axis=0, keepdims=True)
        m2 = jnp.max(jnp.where(sub == i1, neg, blk), axis=0, keepdims=True)
        gs_rows.append(m1 + m2)
    gs = jnp.concatenate(gs_rows, axis=0)
    gi = lax.broadcasted_iota(jnp.int32, (N_GROUPS, tm), 0)
    gsel = jnp.zeros((N_GROUPS, tm), F32)
    for _ in range(TOPK_GROUPS):
        m = jnp.max(gs, axis=0, keepdims=True)
        idx = jnp.min(jnp.where(gs == m, gi, N_GROUPS), axis=0, keepdims=True)
        hit = gi == idx
        gsel = jnp.where(hit, 1.0, gsel)
        gs = jnp.where(hit, neg, gs)
    cand = jnp.concatenate(
        [jnp.where(gsel[g:g + 1, :] > 0.0, sel[g * per:(g + 1) * per, :], neg) for g in range(N_GROUPS)], axis=0)
    ei = lax.broadcasted_iota(jnp.int32, (e, tm), 0)
    chosen = jnp.zeros((e, tm), F32)
    for _ in range(TOP_K):
        m = jnp.max(cand, axis=0, keepdims=True)
        idx = jnp.min(jnp.where(cand == m, ei, e), axis=0, keepdims=True)
        hit = ei == idx
        chosen = jnp.where(hit, scores, chosen)
        cand = jnp.where(hit, neg, cand)
    return chosen / jnp.sum(chosen, axis=0, keepdims=True) * ROUTE_SCALE


def _merge_kernel(x_ref, mod_ref, ya_ref, yb_ref, ga_ref, gb_ref, wg_ref, wd_ref, wo_ref,
                  gf_ref, wr_ref, rb_ref, x1_ref, h2_ref, comb_ref, rank_ref, cnt_ref):
    gt1 = mod_ref[0, 2:3, :]
    sh2 = mod_ref[0, 3:4, :]
    sc2 = mod_ref[0, 4:5, :]
    ua = jnp.dot(ya_ref[0], wg_ref[...], preferred_element_type=F32)
    ub = jnp.dot(yb_ref[0], wd_ref[...], preferred_element_type=F32)
    merged = _sigmoid(ga_ref[0].astype(F32)) * ua + _sigmoid(gb_ref[0].astype(F32)) * ub
    x1 = x_ref[0] + gt1 * _mm(merged, wo_ref[...])
    x1_ref[0] = x1
    h2 = x1 * lax.rsqrt(jnp.mean(x1 * x1, axis=-1, keepdims=True) + EPS) * gf_ref[...] * (1.0 + sc2) + sh2
    h2_hi = h2.astype(BF16)
    h2_ref[0] = h2_hi
    h2_lo = (h2 - h2_hi.astype(F32)).astype(BF16)
    nt_dims = (((1,), (1,)), ((), ()))
    both = lax.dot_general(wr_ref[...], h2_hi, nt_dims, preferred_element_type=F32)
    logits = (both[:N_EXPERTS] + both[N_EXPERTS:]
              + lax.dot_general(wr_ref[:N_EXPERTS, :], h2_lo, nt_dims, preferred_element_type=F32))
    comb = _route(logits, rb_ref[...])
    comb_ref[0] = comb
    tb = MOE_TB
    before = (lax.broadcasted_iota(jnp.int32, (tb, tb), 0) < lax.broadcasted_iota(jnp.int32, (tb, tb), 1))
    before = before.astype(BF16)
    for j in range(comb.shape[1] // tb):
        picked = (comb[:, j * tb:(j + 1) * tb] > 0.0).astype(F32)
        rank_ref[0, :, j * tb:(j + 1) * tb] = jnp.dot(picked.astype(BF16), before, preferred_element_type=F32)
        cnt_ref[0, j] = jnp.broadcast_to(jnp.sum(picked, axis=1, keepdims=True), (N_EXPERTS, LANES))


def _merge(x, mod3, y_a, y_b, proj, w_up_gdn, w_up_dil, w_out, g_ffn, w_router_t, router_bias):
    bsz, s, d = x.shape
    tm = 512
    row = lambda w: pl.BlockSpec((1, tm, w), lambda b, i: (b, i, 0))
    full = lambda a: pl.BlockSpec(a.shape, lambda b, i: (0,) * a.ndim)
    ex = pl.BlockSpec((1, N_EXPERTS, tm), lambda b, i: (b, 0, i))
    return pl.pallas_call(
        _merge_kernel,
        grid=(bsz, s // tm),
        in_specs=[row(d),
                  pl.BlockSpec((1, 6, d), lambda b, i: (b, 0, 0)),
                  row(GDN_W), row(DIL_OUT_W),
                  pl.BlockSpec((1, tm, d), lambda b, i: (b, i, COL_GA // d)),
                  pl.BlockSpec((1, tm, d), lambda b, i: (b, i, COL_GB // d)),
                  full(w_up_gdn), full(w_up_dil), full(w_out), full(g_ffn), full(w_router_t),
                  full(router_bias)],
        out_specs=[row(d), row(d), ex, ex,
                   pl.BlockSpec((1, tm // MOE_TB, N_EXPERTS, LANES), lambda b, i: (b, i, 0, 0))],
        out_shape=[jax.ShapeDtypeStruct((bsz, s, d), F32),
                   jax.ShapeDtypeStruct((bsz, s, d), BF16),
                   jax.ShapeDtypeStruct((bsz, N_EXPERTS, s), F32),
                   jax.ShapeDtypeStruct((bsz, N_EXPERTS, s), F32),
                   jax.ShapeDtypeStruct((bsz, s // MOE_TB, N_EXPERTS, LANES), F32)],
        compiler_params=_params("parallel", "parallel"),
        name="merge_router",
    )(x, mod3, y_a, y_b, proj, proj, w_up_gdn, w_up_dil, w_out, g_ffn, w_router_t, router_bias)


MOE_TB = 256
MOE_CH = 16
MOE_RB = MOE_TB * TOP_K + N_EXPERTS * MOE_CH
MOE_CPB = MOE_RB // MOE_CH
MOE_TM = 512
MOE_CT = MOE_RB
MOE_FT = 512
MOE_CPT = MOE_FT // MOE_CH


def _moe_plan(cnt):
    nb, ne = cnt.shape
    i32 = jnp.int32
    nch = (cnt + MOE_CH - 1) // MOE_CH
    seg0 = jnp.cumsum(nch, axis=1) - nch
    pad = jnp.zeros((nb, LANES - ne), i32)
    meta = jnp.stack([jnp.concatenate([seg0 * MOE_CH, pad + MOE_RB], axis=1),
                      jnp.concatenate([nch * MOE_CH, pad], axis=1)], axis=1).astype(F32)
    meta = jnp.pad(meta, ((0, 0), (0, 6), (0, 0)))
    nch_e = nch.T
    cum_e = jnp.cumsum(nch_e, axis=1)
    tot_e = cum_e[:, -1]
    tiles_e = (tot_e + MOE_CPT - 1) // MOE_CPT
    tile_end = jnp.cumsum(tiles_e)
    ntiles = tile_end[-1]
    max_tiles = (nb * MOE_CPB) // MOE_CPT + ne
    i = jnp.arange(max_tiles, dtype=i32)
    te = jnp.sum((tile_end[None, :] <= i[:, None]).astype(i32), axis=1)
    te = jnp.minimum(te, jnp.sum((tile_end <= ntiles - 1).astype(i32)))
    te = jnp.minimum(te, ne - 1)
    oh_te = (te[:, None] == jnp.arange(ne, dtype=i32)[None, :]).astype(i32)
    pick = lambda tab: jnp.sum(oh_te[:, :, None] * tab[None, :, :], axis=1)
    tile_off_t = jnp.sum(oh_te * (tile_end - tiles_e)[None, :], axis=1)
    tot_t = jnp.sum(oh_te * tot_e[None, :], axis=1)
    cum_t, nch_t, seg0_t = pick(cum_e), pick(nch_e), pick(seg0.T)
    q = (i - tile_off_t)[:, None] * MOE_CPT + jnp.arange(MOE_CPT, dtype=i32)[None, :]
    valid = (q < tot_t[:, None]) & (i < ntiles)[:, None]
    blk = jnp.minimum(jnp.sum((cum_t[:, None, :] <= q[:, :, None]).astype(i32), axis=-1), nb - 1)
    oh_b = (blk[:, :, None] == jnp.arange(nb, dtype=i32)[None, None, :]).astype(i32)
    before = jnp.sum(oh_b * (cum_t - nch_t)[:, None, :], axis=-1)
    src = blk * MOE_CPB + jnp.sum(oh_b * seg0_t[:, None, :], axis=-1) + (q - before)
    spare = nb * MOE_CPB + (i % 2)[:, None] * MOE_CPT + jnp.arange(MOE_CPT, dtype=i32)[None, :]
    tbl_in = jnp.where(valid, src, nb * MOE_CPB + 2 * MOE_CPT).astype(i32).reshape(-1)
    tbl_out = jnp.where(valid, src, spare).astype(i32).reshape(-1)
    return meta, te.astype(i32), tbl_in, tbl_out, ntiles.astype(i32).reshape(1)


def _onehot_rows(meta_ref, rhs, r0, nrows, weighted):
    start = meta_ref[0, 0:1, :]
    plen = meta_ref[0, 1:2, :]
    r = (lax.broadcasted_iota(jnp.int32, (nrows, LANES), 0) + r0).astype(F32)
    owner = jnp.where((r >= start) & (r < start + plen), 1.0, 0.0)
    pos = r[:, :1] - jnp.sum(owner * start, axis=1, keepdims=True)
    got = jnp.dot(owner[:, :N_EXPERTS].astype(BF16), rhs, preferred_element_type=F32)
    tb = MOE_TB
    hit = (got[:, :tb] == pos) & (got[:, tb:2 * tb] > 0.5)
    return jnp.where(hit, got[:, 2 * tb:] if weighted else 1.0, 0.0).astype(BF16)


def _route_rhs(rank_ref, comb_ref, weighted):
    comb = comb_ref[0]
    picked = jnp.where(comb > 0.0, 1.0, 0.0)
    parts = [rank_ref[0], picked] + ([comb] if weighted else [])
    return jnp.concatenate(parts, axis=1).astype(BF16)


def _dispatch_kernel(meta_ref, h_ref, rank_ref, comb_ref, x_ref):
    @pl.when(pl.program_id(0) < pl.num_programs(0) - 1)
    def _():
        rhs = _route_rhs(rank_ref, comb_ref, weighted=False)
        h = h_ref[0]
        for r0 in range(0, MOE_RB, MOE_TM):
            n = min(MOE_TM, MOE_RB - r0)
            p = _onehot_rows(meta_ref, rhs, r0, n, weighted=False)
            x_ref[0, r0:r0 + n, :] = jnp.dot(p, h, preferred_element_type=F32).astype(BF16)

    @pl.when(pl.program_id(0) == pl.num_programs(0) - 1)
    def _():
        x_ref[...] = jnp.zeros_like(x_ref)


def _dispatch(h2, rank_t, comb_t, meta):
    bsz, s, d = h2.shape
    nsb = s // MOE_TB
    nblk = bsz * nsb
    tok = lambda i: (jnp.minimum(i, nblk - 1) // nsb, jnp.minimum(i, nblk - 1) % nsb)
    ex = pl.BlockSpec((1, N_EXPERTS, MOE_TB), lambda i: (tok(i)[0], 0, tok(i)[1]))
    return pl.pallas_call(
        _dispatch_kernel,
        grid=(nblk + 1,),
        in_specs=[pl.BlockSpec((1, 8, LANES), lambda i: (jnp.minimum(i, nblk - 1), 0, 0)),
                  pl.BlockSpec((1, MOE_TB, d), lambda i: (tok(i)[0], tok(i)[1], 0)),
                  ex, ex],
        out_specs=pl.BlockSpec((1, MOE_RB, d), lambda i: (i, 0, 0)),
        out_shape=jax.ShapeDtypeStruct((nblk + 1, MOE_RB, d), BF16),
        compiler_params=_params("parallel"),
        name="moe_dispatch",
    )(meta, h2, rank_t, comb_t)


def _ffn_kernel(te_ref, tin_ref, tout_ref, nt_ref, x_hbm, wg_ref, wu_ref, wd_ref, y_hbm,
                xbuf, ybuf, wgb, wub, wdb, sem_in, sem_out):
    i = pl.program_id(0)
    nt = nt_ref[0]
    slot = i % 2

    def copy_in(t, s, j):
        return pltpu.make_async_copy(x_hbm.at[tin_ref[t * MOE_CPT + j]],
                                     xbuf.at[s, pl.ds(j * MOE_CH, MOE_CH)], sem_in.at[s])

    def copy_out(t, s, j):
        return pltpu.make_async_copy(ybuf.at[s, pl.ds(j * MOE_CH, MOE_CH)],
                                     y_hbm.at[tout_ref[t * MOE_CPT + j]], sem_out.at[s])

    @pl.when(i == 0)
    def _():
        for j in range(MOE_CPT):
            copy_in(0, 0, j).start(priority=j % 2)

    @pl.when(i + 1 < nt)
    def _():
        for j in range(MOE_CPT):
            copy_in(i + 1, 1 - slot, j).start(priority=j % 2)

    @pl.when(i < nt)
    def _():
        for j in range(MOE_CPT):
            copy_in(i, slot, j).wait()

        @pl.when(i >= 2)
        def _():
            for j in range(MOE_CPT):
                copy_out(i - 2, slot, j).wait()

        @pl.when(jnp.logical_or(i == 0, te_ref[i] != te_ref[jnp.maximum(i - 1, 0)]))
        def _():
            wgb[...] = wg_ref[0].astype(BF16)
            wub[...] = wu_ref[0].astype(BF16)
            wdb[...] = wd_ref[0].astype(BF16)

        x = xbuf[slot]
        a = jnp.dot(x, wgb[...], preferred_element_type=F32)
        u = jnp.dot(x, wub[...], preferred_element_type=F32)
        ybuf[slot] = jnp.dot((_silu(a) * u).astype(BF16), wdb[...], preferred_element_type=F32).astype(BF16)
        for j in range(MOE_CPT):
            copy_out(i, slot, j).start(priority=j % 2)

        @pl.when(i == nt - 1)
        def _():
            @pl.when(i >= 1)
            def _():
                for j in range(MOE_CPT):
                    copy_out(i - 1, 1 - slot, j).wait()
            for j in range(MOE_CPT):
                copy_out(i, slot, j).wait()


def _ffn(x_rows, w_eg, w_eu, w_ed, te, tbl_in, tbl_out, ntiles):
    nblk, rb, d = x_rows.shape
    ne, _, de = w_eg.shape
    xc = x_rows.reshape(nblk * MOE_CPB, MOE_CH, d)
    max_tiles = te.shape[0]
    y = pl.pallas_call(
        _ffn_kernel,
        grid_spec=pltpu.PrefetchScalarGridSpec(
            num_scalar_prefetch=4,
            grid=(max_tiles,),
            in_specs=[pl.BlockSpec(memory_space=pl.ANY),
                      pl.BlockSpec((1, d, de), lambda i, te, *_: (te[i], 0, 0)),
                      pl.BlockSpec((1, d, de), lambda i, te, *_: (te[i], 0, 0)),
                      pl.BlockSpec((1, de, d), lambda i, te, *_: (te[i], 0, 0))],
            out_specs=pl.BlockSpec(memory_space=pl.ANY),
            scratch_shapes=[pltpu.VMEM((2, MOE_FT, d), BF16), pltpu.VMEM((2, MOE_FT, d), BF16),
                            pltpu.VMEM((d, de), BF16), pltpu.VMEM((d, de), BF16), pltpu.VMEM((de, d), BF16),
                            pltpu.SemaphoreType.DMA((2,)), pltpu.SemaphoreType.DMA((2,))]),
        out_shape=jax.ShapeDtypeStruct(xc.shape, BF16),
        input_output_aliases={4: 0},
        compiler_params=_params("arbitrary"),
        name="moe_ffn",
    )(te, tbl_in, tbl_out, ntiles, xc, w_eg, w_eu, w_ed)
    return y.reshape(nblk, rb, d)


def _combine_kernel(meta_ref, y_ref, rank_ref, comb_ref, h_ref, x1_ref, mod_ref,
                    sg_ref, su_ref, sd_ref, o_ref):
    rhs = _route_rhs(rank_ref, comb_ref, weighted=True)
    h = h_ref[0]
    a = jnp.dot(h, sg_ref[...], preferred_element_type=F32)
    u = jnp.dot(h, su_ref[...], preferred_element_type=F32)
    acc = _mm(_silu(a) * u, sd_ref[...])
    for r0 in range(0, MOE_RB, MOE_CT):
        n = min(MOE_CT, MOE_RB - r0)
        p = _onehot_rows(meta_ref, rhs, r0, n, weighted=True)
        acc = acc + lax.dot_general(p, y_ref[0, r0:r0 + n, :], (((0,), (0,)), ((), ())),
                                    preferred_element_type=F32)
    o_ref[0] = x1_ref[0] + mod_ref[0, 5:6, :] * acc


def _combine(y_rows, rank_t, comb_t, h2, x1, mod3, sg, su, sd, meta):
    bsz, s, d = x1.shape
    nsb = s // MOE_TB
    tok = lambda w: pl.BlockSpec((1, MOE_TB, w), lambda b, i: (b, i, 0))
    ex = pl.BlockSpec((1, N_EXPERTS, MOE_TB), lambda b, i: (b, 0, i))
    full = lambda a: pl.BlockSpec(a.shape, lambda b, i: (0,) * a.ndim)
    return pl.pallas_call(
        _combine_kernel,
        grid=(bsz, nsb),
        in_specs=[pl.BlockSpec((1, 8, LANES), lambda b, i: (b * nsb + i, 0, 0)),
                  pl.BlockSpec((1, MOE_RB, d), lambda b, i: (b * nsb + i, 0, 0)),
                  ex, ex, tok(d), tok(d),
                  pl.BlockSpec((1, 6, d), lambda b, i: (b, 0, 0)),
                  full(sg), full(su), full(sd)],
        out_specs=tok(d),
        out_shape=jax.ShapeDtypeStruct((bsz, s, d), F32),
        compiler_params=_params("parallel", "parallel"),
        name="moe_combine",
    )(meta, y_rows, rank_t, comb_t, h2, x1, mod3, sg, su, sd)


def _moe(h2, comb_t, rank_t, cnt, x1, mod3, w_eg, w_eu, w_ed, sg, su, sd):
    meta, te, tbl_in, tbl_out, ntiles = _moe_plan(cnt)
    x_rows = _dispatch(h2, rank_t, comb_t, meta)
    y_rows = _ffn(x_rows, w_eg, w_eu, w_ed, te, tbl_in, tbl_out, ntiles)
    return _combine(y_rows, rank_t, comb_t, h2, x1, mod3, sg, su, sd, meta)


def _layer(x, cmod, g_mix, w_in, conv_w, a_log, dt_bias, norm_g, qn_g, kn_g, w_up_gdn, w_up_dil, w_out,
           g_ffn, w_router, router_bias, w_eg, w_eu, w_ed, w_sg, w_su, w_sd):
    bsz, s, d = x.shape
    mod3 = cmod.reshape(bsz, 6, d)
    o_ba = 4 * GDN_W
    o_dq = o_ba + 2 * GDN_HEADS
    o_ga = o_dq + 3 * DIL_W
    w_in16 = w_in.astype(BF16)
    w_main = jnp.concatenate([w_in16[:, :o_ba], w_in16[:, o_ga:], w_in16[:, o_dq:o_ga]], axis=1)
    pad_hi = LANES - 2 * GDN_HEADS
    w_ba = jnp.pad(w_in16[:, o_ba:o_dq], ((0, 0), (0, pad_hi)))
    alog_vec = jnp.pad(a_log, (GDN_HEADS, pad_hi)).reshape(1, LANES)
    dtb_vec = jnp.pad(dt_bias, (GDN_HEADS, pad_hi)).reshape(1, LANES)
    proj, gates = _inproj(x, mod3, g_mix, w_main, w_ba, alog_vec, dtb_vec)

    conv_w3 = conv_w.reshape(GDN_CONV, 3, GDN_W).transpose(1, 0, 2)
    y_a = _gdn(proj, gates, conv_w3, norm_g)
    y_b = _dil(proj, qn_g, kn_g)

    wr_t = w_router.T
    wr_hi = wr_t.astype(BF16)
    w_router_split = jnp.concatenate([wr_hi, (wr_t - wr_hi.astype(F32)).astype(BF16)], axis=0)
    x1, h2, comb_t, rank_t, cnt = _merge(x, mod3, y_a, y_b, proj, w_up_gdn.astype(BF16),
                                         w_up_dil.astype(BF16), w_out.astype(BF16), g_ffn.reshape(1, d),
                                         w_router_split, router_bias.reshape(N_EXPERTS, 1))
    cnt = cnt[..., 0].astype(jnp.int32).reshape(-1, N_EXPERTS)
    return _moe(h2, comb_t, rank_t, cnt, x1, mod3, w_eg, w_eu, w_ed,
                w_sg.astype(BF16), w_su.astype(BF16), w_sd.astype(BF16))


def kernel(x, c, w_ada, b_ada, g_mix, w_in, gdn_conv_w, gdn_a_log, gdn_dt_bias, gdn_norm_g, dil_q_norm_g, dil_k_norm_g, w_up_gdn, w_up_dil, w_out, g_ffn, w_router, router_bias, w_exp_gate, w_exp_up, w_exp_down, w_sh_gate, w_sh_up, w_sh_down):
    for l in range(w_ada.shape[0]):
        cmod = _ada(c, w_ada[l], b_ada[l])
        x = _layer(x, cmod, g_mix[l], w_in[l], gdn_conv_w[l], gdn_a_log[l], gdn_dt_bias[l], gdn_norm_g[l],
                   dil_q_norm_g[l], dil_k_norm_g[l], w_up_gdn[l], w_up_dil[l], w_out[l], g_ffn[l],
                   w_router[l], router_bias[l], w_exp_gate[l], w_exp_up[l], w_exp_down[l],
                   w_sh_gate[l], w_sh_up[l], w_sh_down[l])
    return x
```
